```python
import math
import jax, jax.numpy as jnp
from jax import lax
import numpy as np

D_MODEL = 1024
BATCH = 32
SEQ = 2048
DEPTH = 1

CHUNK = 64
Q_BLOCK = 128
EPS = 1e-6
D_MIX = D_MODEL
M_HEADS = 4
M_WIDTH = D_MIX // 2
M_HEAD_DIM = M_WIDTH // M_HEADS
CONV_WIDTH = 4
A_HEADS = 4
A_WIDTH = D_MIX - M_WIDTH
A_HEAD_DIM = A_WIDTH // A_HEADS
A_QK_DIM = A_HEAD_DIM // 2
IN_SIZES = (2 * M_WIDTH, M_WIDTH, M_WIDTH, M_HEADS, M_HEADS, A_WIDTH, A_WIDTH, A_WIDTH)
IN_COLS = sum(IN_SIZES)
IN_SPLITS = tuple(int(s) for s in np.cumsum(IN_SIZES)[:-1])
N_GROUPS = 4
EXPERTS_PER_GROUP = 8
N_EXPERTS = N_GROUPS * EXPERTS_PER_GROUP
TOP_K = 2
D_EXPERT = D_MODEL // 2
MOE_BLOCK = 256

kernel_name = 'hymba_mlstm_diffattn_hmoe_block'


def rms_norm(x, w):
    xf = x.astype(jnp.float32)
    y = xf * lax.rsqrt(jnp.mean(xf * xf, axis=-1, keepdims=True) + EPS)
    return (y * w.astype(jnp.float32)).astype(x.dtype)


def causal_conv(x, w, b):
    S = x.shape[1]
    xp = jnp.pad(x, ((0, 0), (CONV_WIDTH - 1, 0), (0, 0)))
    out = b
    for j in range(CONV_WIDTH):
        out = out + xp[:, j:j + S] * w[j]
    return out


def mlstm_chunkwise(q, k, v, i_pre, f_pre):
    Bn, S, H, Dh = q.shape
    NC = S // CHUNK

    def to_chunks(t):
        t = t.astype(jnp.float32).reshape((Bn, NC, CHUNK) + t.shape[2:])
        return jnp.moveaxis(jnp.moveaxis(t, 1, 0), 3, 2)

    qc = to_chunks(q)
    kc = to_chunks(k) * (Dh ** -0.5)
    vc = to_chunks(v)
    ic = to_chunks(i_pre)
    fc = jax.nn.log_sigmoid(to_chunks(f_pre))
    causal = jnp.tril(jnp.ones((CHUNK, CHUNK), dtype=bool))

    def step(carry, inp):
        C, n, m = carry
        q_c, k_c, v_c, i_c, lf_c = inp
        b = jnp.cumsum(lf_c, axis=-1)
        log_d = jnp.where(causal, b[..., :, None] - b[..., None, :] + i_c[..., None, :], -jnp.inf)
        log_inter = b + m[..., None]
        m_t = jnp.maximum(log_inter, jnp.max(log_d, axis=-1))
        d = jnp.exp(log_d - m_t[..., None])
        inter = jnp.exp(log_inter - m_t)
        s = jnp.einsum('bhtd,bhsd->bhts', q_c, k_c) * d
        num = jnp.einsum('bhts,bhse->bhte', s, v_c) + inter[..., None] * jnp.einsum('bhtd,bhde->bhte', q_c, C)
        den = jnp.sum(s, axis=-1) + inter * jnp.einsum('bhtd,bhd->bht', q_c, n)
        h = num / jnp.maximum(jnp.abs(den), jnp.exp(-m_t))[..., None]
        b_end = b[..., -1]
        log_w = b_end[..., None] - b + i_c
        m_new = jnp.maximum(b_end + m, jnp.max(log_w, axis=-1))
        w = jnp.exp(log_w - m_new[..., None])
        decay = jnp.exp(b_end + m - m_new)
        C_new = decay[..., None, None] * C + jnp.einsum('bhs,bhsd,bhse->bhde', w, k_c, v_c)
        n_new = decay[..., None] * n + jnp.einsum('bhs,bhsd->bhd', w, k_c)
        return (C_new, n_new, m_new), h

    init = (jnp.zeros((Bn, H, Dh, Dh), jnp.float32),
            jnp.zeros((Bn, H, Dh), jnp.float32),
            jnp.zeros((Bn, H), jnp.float32))
    _, h = lax.scan(step, init, (qc, kc, vc, ic, fc))
    h = jnp.moveaxis(jnp.moveaxis(h, 0, 1), 2, 3)
    return h.reshape(Bn, S, H, Dh)


def diff_attention(q, k, v, q_norm_w, k_norm_w, lq1, lk1, lq2, lk2, a_norm_w, lambda_init):
    Bn, S = q.shape[:2]
    NQ = S // Q_BLOCK
    q = rms_norm(q, q_norm_w)
    k = rms_norm(k, k_norm_w)
    f32 = jnp.float32
    lam = (jnp.exp(jnp.sum(lq1.astype(f32) * lk1.astype(f32)))
           - jnp.exp(jnp.sum(lq2.astype(f32) * lk2.astype(f32))) + lambda_init)
    kt = jnp.transpose(k, (0, 2, 3, 1, 4))
    vt = jnp.transpose(v, (0, 2, 1, 3))
    qb = q.reshape(Bn, NQ, Q_BLOCK, A_HEADS, 2, A_QK_DIM).transpose(1, 0, 3, 4, 2, 5)
    key_chunk = jnp.arange(S) // CHUNK
    scale = A_QK_DIM ** -0.5

    def block(args):
        q_blk, idx = args
        q_chunk = (idx * Q_BLOCK + jnp.arange(Q_BLOCK)) // CHUNK
        mask = key_chunk[None, :] <= q_chunk[:, None]
        s = jnp.einsum('bhcqd,bhcsd->bhcqs', q_blk, kt).astype(f32) * scale
        p = jax.nn.softmax(jnp.where(mask, s, -jnp.inf), axis=-1)
        attn = p[:, :, 0] - lam * p[:, :, 1]
        return jnp.einsum('bhqs,bhsd->bhqd', attn.astype(v.dtype), vt)

    o = lax.map(block, (qb, jnp.arange(NQ)))
    o = o.transpose(1, 0, 3, 2, 4).reshape(Bn, S, A_HEADS, A_HEAD_DIM)
    o = rms_norm(o, a_norm_w) * (1.0 - lambda_init)
    return o.reshape(Bn, S, A_WIDTH)


def hybrid_mixer(h, w_in, b_igate, b_fgate, conv_w, conv_b, m_norm_w, q_norm_w, k_norm_w,
                 lq1, lk1, lq2, lk2, a_norm_w, w_out, lambda_init):
    Bn, S, _ = h.shape
    proj = h @ w_in
    mqk, mv, mo, mi, mf, aq, ak, av = jnp.split(proj, IN_SPLITS, axis=-1)
    qk = jax.nn.silu(causal_conv(mqk, conv_w, conv_b))
    mq, mk = jnp.split(qk, 2, axis=-1)
    shp = (Bn, S, M_HEADS, M_HEAD_DIM)
    hm = mlstm_chunkwise(mq.reshape(shp), mk.reshape(shp), mv.reshape(shp), mi + b_igate, mf + b_fgate)
    hm = jax.nn.sigmoid(mo).reshape(shp) * hm.astype(h.dtype)
    hm = rms_norm(hm, m_norm_w).reshape(Bn, S, M_WIDTH)
    ha = diff_attention(aq.reshape(Bn, S, A_HEADS, 2, A_QK_DIM), ak.reshape(Bn, S, A_HEADS, 2, A_QK_DIM),
                        av.reshape(Bn, S, A_HEADS, A_HEAD_DIM), q_norm_w, k_norm_w,
                        lq1, lk1, lq2, lk2, a_norm_w, lambda_init)
    return jnp.concatenate([hm, ha], axis=-1) @ w_out


def hierarchical_moe(h, w_group, b_group, w_expert, b_expert, w_gate, w_up, w_down):
    Bn, S, D = h.shape
    T = Bn * S
    xt = h.reshape(T, D)
    f32 = jnp.float32
    g_logits = (xt @ w_group).astype(f32) + b_group.astype(f32)
    g_prob = jax.nn.softmax(g_logits, axis=-1)
    g_sel = jnp.argmax(g_logits, axis=-1)
    g_w = jnp.take_along_axis(g_prob, g_sel[:, None], axis=1)[:, 0]
    e_logits = ((xt @ w_expert).astype(f32) + b_expert.astype(f32)).reshape(T, N_GROUPS, EXPERTS_PER_GROUP)
    e_in = jnp.take_along_axis(e_logits, g_sel[:, None, None], axis=1)[:, 0]
    top_vals, top_idx = lax.top_k(e_in, TOP_K)
    weights = g_w[:, None] * jax.nn.softmax(top_vals, axis=-1)
    eid = (g_sel[:, None] * EXPERTS_PER_GROUP + top_idx).reshape(-1)
    A = T * TOP_K
    tok = jnp.arange(A) // TOP_K
    counts = jax.ops.segment_sum(jnp.ones_like(eid), eid, num_segments=N_EXPERTS)
    padded = ((counts + MOE_BLOCK - 1) // MOE_BLOCK) * MOE_BLOCK
    pad_end = jnp.cumsum(padded)
    pad_start = pad_end - padded
    start = jnp.cumsum(counts) - counts
    order = jnp.argsort(eid)
    eid_sorted = eid[order]
    dest_sorted = pad_start[eid_sorted] + (jnp.arange(A) - start[eid_sorted])
    dest = jnp.zeros((A,), jnp.int32).at[order].set(dest_sorted.astype(jnp.int32))
    P = ((A + MOE_BLOCK - 1) // MOE_BLOCK) * MOE_BLOCK + N_EXPERTS * MOE_BLOCK
    NB = P // MOE_BLOCK
    block_start = jnp.arange(NB) * MOE_BLOCK
    block_expert = jnp.minimum(jnp.sum(pad_end[None, :] <= block_start[:, None], axis=1), N_EXPERTS - 1)
    x_buf = jnp.zeros((P, D), h.dtype).at[dest].set(xt[tok])

    def expert_block(args):
        xb, e = args
        hid = jax.nn.silu(xb @ w_gate[e]) * (xb @ w_up[e])
        return hid @ w_down[e]

    y_buf = lax.map(expert_block, (x_buf.reshape(NB, MOE_BLOCK, D), block_expert)).reshape(P, D)
    y = jnp.sum(y_buf[dest].reshape(T, TOP_K, D) * weights[..., None].astype(h.dtype), axis=1)
    return y.reshape(Bn, S, D)


def setup_inputs(seed: int = 0) -> dict:
    key = jax.random.key(seed)
    ks = jax.random.split(key, 24)
    nrm = jax.random.normal
    f32 = jnp.float32
    L = DEPTH
    gain = lambda k, n: 1.0 + 0.01 * nrm(k, (L, n), f32)
    return {
        'x': nrm(ks[0], (BATCH, SEQ, D_MODEL), f32),
        'norm1_w': gain(ks[1], D_MODEL),
        'w_in': nrm(ks[2], (L, D_MODEL, IN_COLS), f32) * D_MODEL ** -0.5,
        'b_igate': 0.1 * nrm(ks[3], (L, M_HEADS), f32),
        'b_fgate': jnp.linspace(3.0, 6.0, M_HEADS, dtype=f32)[None, :] + 0.1 * nrm(ks[4], (L, M_HEADS), f32),
        'conv_w': nrm(ks[5], (L, CONV_WIDTH, 2 * M_WIDTH), f32) * CONV_WIDTH ** -0.5,
        'conv_b': 0.01 * nrm(ks[6], (L, 2 * M_WIDTH), f32),
        'm_norm_w': gain(ks[7], M_HEAD_DIM),
        'q_norm_w': gain(ks[8], A_QK_DIM),
        'k_norm_w': gain(ks[9], A_QK_DIM),
        'lambda_q1': 0.1 * nrm(ks[10], (L, A_QK_DIM), f32),
        'lambda_k1': 0.1 * nrm(ks[11], (L, A_QK_DIM), f32),
        'lambda_q2': 0.1 * nrm(ks[12], (L, A_QK_DIM), f32),
        'lambda_k2': 0.1 * nrm(ks[13], (L, A_QK_DIM), f32),
        'a_norm_w': gain(ks[14], A_HEAD_DIM),
        'w_out': nrm(ks[15], (L, D_MIX, D_MODEL), f32) * D_MIX ** -0.5,
        'norm2_w': gain(ks[16], D_MODEL),
        'w_group': nrm(ks[17], (L, D_MODEL, N_GROUPS), f32) * D_MODEL ** -0.5,
        'b_group': 0.01 * nrm(ks[18], (L, N_GROUPS), f32),
        'w_expert': nrm(ks[19], (L, D_MODEL, N_EXPERTS), f32) * D_MODEL ** -0.5,
        'b_expert': 0.01 * nrm(ks[20], (L, N_EXPERTS), f32),
        'w_gate': nrm(ks[21], (L, N_EXPERTS, D_MODEL, D_EXPERT), f32) * D_MODEL ** -0.5,
        'w_up': nrm(ks[22], (L, N_EXPERTS, D_MODEL, D_EXPERT), f32) * D_MODEL ** -0.5,
        'w_down': nrm(ks[23], (L, N_EXPERTS, D_EXPERT, D_MODEL), f32) * D_EXPERT ** -0.5,
    }


def reference(x, norm1_w, w_in, b_igate, b_fgate, conv_w, conv_b, m_norm_w, q_norm_w, k_norm_w,
              lambda_q1, lambda_k1, lambda_q2, lambda_k2, a_norm_w, w_out, norm2_w,
              w_group, b_group, w_expert, b_expert, w_gate, w_up, w_down):
    for l in range(DEPTH):
        lambda_init = 0.8 - 0.6 * math.exp(-0.3 * l)
        h = rms_norm(x, norm1_w[l])
        x = x + hybrid_mixer(h, w_in[l], b_igate[l], b_fgate[l], conv_w[l], conv_b[l], m_norm_w[l],
                             q_norm_w[l], k_norm_w[l], lambda_q1[l], lambda_k1[l], lambda_q2[l],
                             lambda_k2[l], a_norm_w[l], w_out[l], lambda_init)
        h = rms_norm(x, norm2_w[l])
        x = x + hierarchical_moe(h, w_group[l], b_group[l], w_expert[l], b_expert[l],
                                 w_gate[l], w_up[l], w_down[l])
    return x
```

```python
import functools
import math

import jax
import jax.numpy as jnp
from jax import lax
from jax.experimental import pallas as pl
from jax.experimental.pallas import tpu as pltpu

F32 = jnp.float32
BF16 = jnp.bfloat16
EPS = 1e-6

D_MODEL = 1024
HEAD_DIM = 128
N_HEADS = 4
MIX_HALF = N_HEADS * HEAD_DIM
QK_DIM = 64
CONV_WIDTH = 4
ATTN_CHUNK = 64
N_GROUPS = 4
EXPERTS_PER_GROUP = 8
N_EXPERTS = N_GROUPS * EXPERTS_PER_GROUP
TOP_K = 2
D_EXPERT = D_MODEL // 2

LANES = 128
PROJ_COLS = 7 * MIX_HALF
VMEM_LIMIT = 48 * 1024 * 1024

TM_PROJ = 512
MLSTM_CHUNK = 128
TQ_ATTN = 256
TK_ATTN = 256
TM_OUT = 256
TM_ROWS = 256
MOE_BLOCK = 512


def _cparams(sem):
    return pltpu.CompilerParams(dimension_semantics=sem, vmem_limit_bytes=VMEM_LIMIT)


def _inproj_kernel(x_ref, nw_ref, w_ref, proj_ref, gate_ref):
    x = x_ref[...]
    ms = jnp.mean(x * x, axis=-1, keepdims=True)
    h = (x * lax.rsqrt(ms + EPS) * nw_ref[...]).astype(BF16)
    nb = MIX_HALF
    for j in range(PROJ_COLS // nb):
        proj_ref[:, j * nb:(j + 1) * nb] = jnp.dot(
            h, w_ref[:, j * nb:(j + 1) * nb], preferred_element_type=F32).astype(BF16)
    gate_ref[...] = jnp.dot(h, w_ref[:, PROJ_COLS:], preferred_element_type=F32)


def _inproj(x2d, nw, w):
    T = x2d.shape[0]
    tm = TM_PROJ
    return pl.pallas_call(
        _inproj_kernel,
        grid=(T // tm,),
        in_specs=[
            pl.BlockSpec((tm, D_MODEL), lambda i: (i, 0)),
            pl.BlockSpec((1, D_MODEL), lambda i: (0, 0)),
            pl.BlockSpec((D_MODEL, PROJ_COLS + LANES), lambda i: (0, 0)),
        ],
        out_specs=[
            pl.BlockSpec((tm, PROJ_COLS), lambda i: (i, 0)),
            pl.BlockSpec((tm, LANES), lambda i: (i, 0)),
        ],
        out_shape=[
            jax.ShapeDtypeStruct((T, PROJ_COLS), BF16),
            jax.ShapeDtypeStruct((T, LANES), F32),
        ],
        compiler_params=_cparams(("arbitrary",)),
        name="inproj",
    )(x2d, nw, w)


def _log_sigmoid(x):
    return jnp.minimum(x, 0.0) - jnp.log(1.0 + jnp.exp(-jnp.abs(x)))


def _cumsum_rows(x):
    n = x.shape[0]
    row = lax.broadcasted_iota(jnp.int32, x.shape, 0)
    s = 1
    while s < n:
        x = x + jnp.where(row >= s, pltpu.roll(x, s, axis=0), 0.0)
        s *= 2
    return x


def _mlstm_kernel(qp_ref, kp_ref, v_ref, o_ref, g_ref, cwq_ref, cwk_ref, cbq_ref, cbk_ref,
                  gb_ref, nw_ref, out_ref, qpad, kpad, *, seq, chunk):
    h = pl.program_id(1)
    halo = 8
    qpad[0:halo, :] = jnp.zeros((halo, HEAD_DIM), F32)
    kpad[0:halo, :] = jnp.zeros((halo, HEAD_DIM), F32)
    qpad[halo:halo + seq, :] = qp_ref[...].astype(F32)
    kpad[halo:halo + seq, :] = kp_ref[...].astype(F32)

    def conv_silu(pad_ref, r0, cw_ref, cb_ref):
        win = pad_ref[pl.ds(r0, chunk + halo), :]
        acc = cb_ref[...]
        for j in range(CONV_WIDTH):
            sh = CONV_WIDTH - 1 - j
            xs = win if sh == 0 else pltpu.roll(win, sh, axis=0)
            acc = acc + xs[halo:, :] * cw_ref[j:j + 1, :]
        return acc * jax.nn.sigmoid(acc)

    lane = lax.broadcasted_iota(jnp.int32, (chunk, LANES), 1)
    sub = lax.broadcasted_iota(jnp.int32, (LANES, chunk), 0)
    tri = (lax.broadcasted_iota(jnp.int32, (chunk, chunk), 0)
           >= lax.broadcasted_iota(jnp.int32, (chunk, chunk), 1))

    def sel_lane(x, idx):
        ln = lax.broadcasted_iota(jnp.int32, x.shape, 1)
        return jnp.sum(jnp.where(ln == idx, x, 0.0), axis=1, keepdims=True)

    def sel_sub(x, idx):
        return jnp.sum(jnp.where(sub == idx, x, 0.0), axis=0, keepdims=True)

    def step(c, carry):
        C, n, m = carry
        r0 = pl.multiple_of(c * chunk, chunk)
        q = conv_silu(qpad, r0, cwq_ref, cbq_ref)
        k = conv_silu(kpad, r0, cwk_ref, cbk_ref) * (HEAD_DIM ** -0.5)
        v = v_ref[pl.ds(r0, chunk), :]
        g = g_ref[pl.ds(r0, chunk), :] + gb_ref[...]
        bc = _cumsum_rows(_log_sigmoid(g))
        i_col = sel_lane(g, h)
        b_col = sel_lane(bc, h + N_HEADS)
        i_row = sel_sub(g.T, h)
        b_row = sel_sub(bc.T, h + N_HEADS)
        b_end = sel_lane(bc[chunk - 1:chunk, :], h + N_HEADS)

        log_d = jnp.where(tri, b_col - b_row + i_row, -jnp.inf)
        log_inter = b_col + m
        m_t = jnp.maximum(log_inter, jnp.max(log_d, axis=-1, keepdims=True))
        d = jnp.exp(log_d - m_t)
        inter = jnp.exp(log_inter - m_t)
        qb = q.astype(BF16)
        s = lax.dot_general(qb, k.astype(BF16), (((1,), (1,)), ((), ())),
                            preferred_element_type=F32) * d
        num = (jnp.dot(s.astype(BF16), v, preferred_element_type=F32)
               + inter * jnp.dot(qb, C.astype(BF16), preferred_element_type=F32))
        den = jnp.sum(s, axis=-1, keepdims=True) + inter * jnp.sum(q * n, axis=-1, keepdims=True)
        hh = num / jnp.maximum(jnp.abs(den), jnp.exp(-m_t))

        log_w = b_end - b_col + i_col
        m_new = jnp.maximum(b_end + m, jnp.max(log_w, axis=0, keepdims=True))
        kw = k * jnp.exp(log_w - m_new)
        decay = jnp.exp(b_end + m - m_new)
        C_new = decay * C + jnp.dot(kw.T.astype(BF16), v, preferred_element_type=F32)
        n_new = decay * n + jnp.sum(kw, axis=0, keepdims=True)

        hm = jax.nn.sigmoid(o_ref[pl.ds(r0, chunk), :].astype(F32)) * hh
        y = hm * lax.rsqrt(jnp.mean(hm * hm, axis=-1, keepdims=True) + EPS) * nw_ref[...]
        out_ref[pl.ds(r0, chunk), :] = y.astype(BF16)
        return C_new, n_new, m_new

    init = (jnp.zeros((HEAD_DIM, HEAD_DIM), F32), jnp.zeros((1, HEAD_DIM), F32), jnp.zeros((1, 1), F32))
    lax.fori_loop(0, seq // chunk, step, init)


def _mlstm(proj, gates, conv_w, conv_b, gate_b, m_norm_w, batch, seq):
    T = batch * seq
    chunk = min(MLSTM_CHUNK, seq)
    col = lambda off: (lambda b, h: (b, off + h))
    cst = lambda b, h: (0, 0)
    return pl.pallas_call(
        functools.partial(_mlstm_kernel, seq=seq, chunk=chunk),
        grid=(batch, N_HEADS),
        in_specs=[
            pl.BlockSpec((seq, HEAD_DIM), col(0)),
            pl.BlockSpec((seq, HEAD_DIM), col(N_HEADS)),
            pl.BlockSpec((seq, HEAD_DIM), col(2 * N_HEADS)),
            pl.BlockSpec((seq, HEAD_DIM), col(3 * N_HEADS)),
            pl.BlockSpec((seq, LANES), lambda b, h: (b, 0)),
            pl.BlockSpec((CONV_WIDTH, HEAD_DIM), lambda b, h: (0, h)),
            pl.BlockSpec((CONV_WIDTH, HEAD_DIM), lambda b, h: (0, N_HEADS + h)),
            pl.BlockSpec((1, HEAD_DIM), lambda b, h: (0, h)),
            pl.BlockSpec((1, HEAD_DIM), lambda b, h: (0, N_HEADS + h)),
            pl.BlockSpec((1, LANES), cst),
            pl.BlockSpec((1, HEAD_DIM), cst),
        ],
        out_specs=pl.BlockSpec((seq, HEAD_DIM), lambda b, h: (b, h)),
        out_shape=jax.ShapeDtypeStruct((T, MIX_HALF), BF16),
        scratch_shapes=[pltpu.VMEM((seq + 8, HEAD_DIM), F32), pltpu.VMEM((seq + 8, HEAD_DIM), F32)],
        compiler_params=_cparams(("arbitrary", "arbitrary")),
        name="mlstm",
    )(proj, proj, proj, proj, gates, conv_w, conv_w, conv_b, conv_b, gate_b, m_norm_w)


def _half_norm(x, w):
    lo = lax.broadcasted_iota(jnp.int32, x.shape, 1) < QK_DIM
    x2 = x * x
    s_lo = jnp.sum(jnp.where(lo, x2, 0.0), axis=-1, keepdims=True)
    s_hi = jnp.sum(jnp.where(lo, 0.0, x2), axis=-1, keepdims=True)
    ms = jnp.where(lo, s_lo, s_hi) * (1.0 / QK_DIM)
    return x * lax.rsqrt(ms + EPS) * w


def _attn_kernel(q_ref, k_ref, v_ref, qw_ref, kw_ref, lq1_ref, lk1_ref, lq2_ref, lk2_ref, anw_ref,
                 out_ref, kn_s, *, seq, tq, tk, lambda_init):
    qi = pl.program_id(2)
    nblk = 256 if seq % 256 == 0 else seq

    @pl.when(qi == 0)
    def _():
        def norm_block(i, _):
            r0 = pl.multiple_of(i * nblk, nblk)
            kn_s[pl.ds(r0, nblk), :] = _half_norm(k_ref[pl.ds(r0, nblk), :].astype(F32), kw_ref[...]).astype(BF16)
            return 0
        lax.fori_loop(0, seq // nblk, norm_block, 0)

    qn = _half_norm(q_ref[...].astype(F32), qw_ref[...])
    lo = lax.broadcasted_iota(jnp.int32, qn.shape, 1) < QK_DIM
    q2 = jnp.concatenate([jnp.where(lo, qn, 0.0), jnp.where(lo, 0.0, qn)], axis=0).astype(BF16)

    def tile(j, carry, masked):
        m_i, l_i, acc = carry
        k0 = pl.multiple_of(j * tk, tk)
        s = lax.dot_general(q2, kn_s[pl.ds(k0, tk), :], (((1,), (1,)), ((), ())),
                            preferred_element_type=F32)
        if masked:
            qrow = lax.broadcasted_iota(jnp.int32, s.shape, 0)
            qrow = jnp.where(qrow >= tq, qrow - tq, qrow)
            kcol = lax.broadcasted_iota(jnp.int32, s.shape, 1)
            s = jnp.where(kcol // ATTN_CHUNK <= qrow // ATTN_CHUNK, s, -jnp.inf)
        m_new = jnp.maximum(m_i, jnp.max(s, axis=-1, keepdims=True))
        alpha = jnp.exp(m_i - m_new)
        p = jnp.exp(s - m_new)
        l_new = alpha * l_i + jnp.sum(p, axis=-1, keepdims=True)
        acc_new = alpha * acc + jnp.dot(p.astype(BF16), v_ref[pl.ds(k0, tk), :], preferred_element_type=F32)
        return m_new, l_new, acc_new

    init = (jnp.full((2 * tq, 1), -jnp.inf, F32), jnp.zeros((2 * tq, 1), F32),
            jnp.zeros((2 * tq, HEAD_DIM), F32))
    carry = lax.fori_loop(0, qi, lambda j, c: tile(j, c, False), init)
    _, l_f, acc = tile(qi, carry, True)

    lam = (jnp.exp(jnp.sum(lq1_ref[...] * lk1_ref[...], axis=-1, keepdims=True))
           - jnp.exp(jnp.sum(lq2_ref[...] * lk2_ref[...], axis=-1, keepdims=True)) + lambda_init)
    o = acc[:tq] / l_f[:tq] - lam * (acc[tq:] / l_f[tq:])
    y = o * lax.rsqrt(jnp.mean(o * o, axis=-1, keepdims=True) + EPS) * anw_ref[...] * (1.0 - lambda_init)
    out_ref[...] = y.astype(BF16)


def _diff_attn(proj, qw, kw, lq1, lk1, lq2, lk2, a_norm_w, batch, seq, lambda_init):
    T = batch * seq
    tq = min(TQ_ATTN, seq)
    tk = tq
    nq = seq // tq
    cst = lambda b, h, i: (0, 0)
    return pl.pallas_call(
        functools.partial(_attn_kernel, seq=seq, tq=tq, tk=tk, lambda_init=lambda_init),
        grid=(batch, N_HEADS, nq),
        in_specs=[
            pl.BlockSpec((tq, HEAD_DIM), lambda b, h, i: (b * nq + i, 4 * N_HEADS + h)),
            pl.BlockSpec((seq, HEAD_DIM), lambda b, h, i: (b, 5 * N_HEADS + h)),
            pl.BlockSpec((seq, HEAD_DIM), lambda b, h, i: (b, 6 * N_HEADS + h)),
            pl.BlockSpec((1, HEAD_DIM), cst),
            pl.BlockSpec((1, HEAD_DIM), cst),
            pl.BlockSpec((1, QK_DIM), cst),
            pl.BlockSpec((1, QK_DIM), cst),
            pl.BlockSpec((1, QK_DIM), cst),
            pl.BlockSpec((1, QK_DIM), cst),
            pl.BlockSpec((1, HEAD_DIM), cst),
        ],
        out_specs=pl.BlockSpec((tq, HEAD_DIM), lambda b, h, i: (b * nq + i, h)),
        out_shape=jax.ShapeDtypeStruct((T, MIX_HALF), BF16),
        scratch_shapes=[pltpu.VMEM((seq, HEAD_DIM), BF16)],
        compiler_params=_cparams(("arbitrary", "arbitrary", "arbitrary")),
        name="diffattn",
    )(proj, proj, proj, qw, kw, lq1, lk1, lq2, lk2, a_norm_w)


def _outproj_router_kernel(hm_ref, ha_ref, x_ref, wo_ref, nw_ref, wr_hi_ref, wr_lo_ref, br_ref,
                           x1_ref, h2_ref, route_ref, counts_ref, carry):
    i = pl.program_id(0)
    tm = x_ref.shape[0]

    @pl.when(i == 0)
    def _():
        carry[...] = jnp.zeros_like(carry)

    mix = (jnp.dot(hm_ref[...], wo_ref[0:MIX_HALF, :], preferred_element_type=F32)
           + jnp.dot(ha_ref[...], wo_ref[MIX_HALF:, :], preferred_element_type=F32))
    x1 = x_ref[...] + mix
    x1_ref[...] = x1
    h2 = x1 * lax.rsqrt(jnp.mean(x1 * x1, axis=-1, keepdims=True) + EPS) * nw_ref[...]
    h2_ref[...] = h2

    h_hi = h2.astype(BF16)
    h_lo = (h2 - h_hi.astype(F32)).astype(BF16)
    logits = (jnp.dot(h_hi, wr_hi_ref[...], preferred_element_type=F32)
              + jnp.dot(h_lo, wr_hi_ref[...], preferred_element_type=F32)
              + jnp.dot(h_hi, wr_lo_ref[...], preferred_element_type=F32)) + br_ref[...]

    lane_i = lax.broadcasted_iota(jnp.int32, (tm, LANES), 1)
    lane = lane_i.astype(F32)
    big = float(LANES)
    gmask = (lane_i >= N_EXPERTS) & (lane_i < N_EXPERTS + N_GROUPS)
    gl = jnp.where(gmask, logits, -jnp.inf)
    gmax = jnp.max(gl, axis=-1, keepdims=True)
    g_sel = jnp.min(jnp.where(gl == gmax, lane, big), axis=-1, keepdims=True) - float(N_EXPERTS)
    g_w = 1.0 / jnp.sum(jnp.where(gmask, jnp.exp(logits - gmax), 0.0), axis=-1, keepdims=True)
    e_lo = g_sel * float(EXPERTS_PER_GROUP)
    emask = (lane >= e_lo) & (lane < e_lo + float(EXPERTS_PER_GROUP))
    el = jnp.where(emask, logits, -jnp.inf)
    v1 = jnp.max(el, axis=-1, keepdims=True)
    e1 = jnp.min(jnp.where(el == v1, lane, big), axis=-1, keepdims=True)
    el2 = jnp.where(lane == e1, -jnp.inf, el)
    v2 = jnp.max(el2, axis=-1, keepdims=True)
    e2 = jnp.min(jnp.where(el2 == v2, lane, big), axis=-1, keepdims=True)
    t2 = jnp.exp(v2 - v1)
    p1 = 1.0 / (1.0 + t2)
    p2 = t2 / (1.0 + t2)

    oh1 = (lane == e1).astype(F32)
    oh2 = (lane == e2).astype(F32)
    both = oh1 + oh2
    strict = (lax.broadcasted_iota(jnp.int32, (tm, tm), 0)
              > lax.broadcasted_iota(jnp.int32, (tm, tm), 1)).astype(BF16)
    prefix = jnp.dot(strict, both.astype(BF16), preferred_element_type=F32) + carry[...]
    r1 = jnp.sum(oh1 * prefix, axis=-1, keepdims=True)
    r2 = jnp.sum(oh2 * prefix, axis=-1, keepdims=True)
    carry[...] = carry[...] + jnp.sum(both, axis=0, keepdims=True)
    counts_ref[...] = carry[...]

    route = jnp.where(lane_i == 0, e1, 0.0)
    route = jnp.where(lane_i == 1, e2, route)
    route = jnp.where(lane_i == 2, r1, route)
    route = jnp.where(lane_i == 3, r2, route)
    route = jnp.where(lane_i == 4, g_w * p1, route)
    route = jnp.where(lane_i == 5, g_w * p2, route)
    route_ref[...] = route


def _outproj_router(hm, ha, x2d, wo, nw, wr_hi, wr_lo, br):
    T = x2d.shape[0]
    tm = min(TM_OUT, T)
    row = lambda i: (i, 0)
    cst = lambda i: (0, 0)
    return pl.pallas_call(
        _outproj_router_kernel,
        grid=(T // tm,),
        in_specs=[
            pl.BlockSpec((tm, MIX_HALF), row),
            pl.BlockSpec((tm, MIX_HALF), row),
            pl.BlockSpec((tm, D_MODEL), row),
            pl.BlockSpec((D_MODEL, D_MODEL), cst),
            pl.BlockSpec((1, D_MODEL), cst),
            pl.BlockSpec((D_MODEL, LANES), cst),
            pl.BlockSpec((D_MODEL, LANES), cst),
            pl.BlockSpec((1, LANES), cst),
        ],
        out_specs=[
            pl.BlockSpec((tm, D_MODEL), row),
            pl.BlockSpec((tm, D_MODEL), row),
            pl.BlockSpec((tm, LANES), row),
            pl.BlockSpec((1, LANES), cst),
        ],
        out_shape=[
            jax.ShapeDtypeStruct((T, D_MODEL), F32),
            jax.ShapeDtypeStruct((T, D_MODEL), F32),
            jax.ShapeDtypeStruct((T, LANES), F32),
            jax.ShapeDtypeStruct((1, LANES), F32),
        ],
        scratch_shapes=[pltpu.VMEM((1, LANES), F32)],
        compiler_params=_cparams(("arbitrary",)),
        name="outproj_router",
    )(hm, ha, x2d, wo, nw, wr_hi, wr_lo, br)


def _dest_copy(dest_hbm, dsm, dsem, step, slot):
    return pltpu.make_async_copy(dest_hbm.at[step], dsm.at[slot], dsem.at[slot])


def _load_dest(dest_hbm, dsm, dsem):
    i = pl.program_id(0)
    n = pl.num_programs(0)
    slot = lax.rem(i, 2)

    @pl.when(i == 0)
    def _():
        _dest_copy(dest_hbm, dsm, dsem, 0, 0).start()

    _dest_copy(dest_hbm, dsm, dsem, i, slot).wait()

    @pl.when(i + 1 < n)
    def _():
        _dest_copy(dest_hbm, dsm, dsem, i + 1, 1 - slot).start()

    return slot


def _scatter_kernel(dest_hbm, h2_ref, xbuf_in, xbuf_out, dsm, dsem, sem):
    del xbuf_in
    tm = h2_ref.shape[0]
    slot = _load_dest(dest_hbm, dsm, dsem)

    def row_copy(r, d):
        return pltpu.make_async_copy(h2_ref.at[pl.ds(r, 1)], xbuf_out.at[pl.ds(d, 1)], sem)

    def body(r, _):
        for kk in range(TOP_K):
            row_copy(r, dsm[slot, 0, TOP_K * r + kk]).start()
        return 0

    lax.fori_loop(0, tm, body, 0)
    for _ in range(TOP_K):
        pltpu.make_async_copy(h2_ref, xbuf_out.at[pl.ds(0, tm)], sem).wait()


def _scatter_rows(dest3, h2, xbuf):
    T = h2.shape[0]
    tm = min(TM_ROWS, T)
    return pl.pallas_call(
        _scatter_kernel,
        grid=(T // tm,),
        in_specs=[
            pl.BlockSpec(memory_space=pl.ANY),
            pl.BlockSpec((tm, D_MODEL), lambda i: (i, 0)),
            pl.BlockSpec(memory_space=pl.ANY),
        ],
        out_specs=pl.BlockSpec(memory_space=pl.ANY),
        out_shape=jax.ShapeDtypeStruct(xbuf.shape, xbuf.dtype),
        scratch_shapes=[
            pltpu.SMEM((2, 1, TOP_K * tm), jnp.int32),
            pltpu.SemaphoreType.DMA((2,)),
            pltpu.SemaphoreType.DMA,
        ],
        input_output_aliases={2: 0},
        compiler_params=_cparams(("arbitrary",)),
        name="scatter_rows",
    )(dest3, h2, xbuf)


def _expert_kernel(be_ref, nv_ref, x_ref, wg_ref, wu_ref, wd_ref, y_ref):
    b = pl.program_id(0)

    @pl.when(b < nv_ref[0])
    def _():
        x = x_ref[...].astype(BF16)
        g = jnp.dot(x, wg_ref[0], preferred_element_type=F32)
        u = jnp.dot(x, wu_ref[0], preferred_element_type=F32)
        hid = (g * jax.nn.sigmoid(g) * u).astype(BF16)
        y_ref[...] = jnp.dot(hid, wd_ref[0], preferred_element_type=F32)

    @pl.when(b >= nv_ref[0])
    def _():
        y_ref[...] = jnp.zeros_like(y_ref)


def _experts(block_expert, n_valid, xbuf, wg, wu, wd):
    P = xbuf.shape[0]
    blk = MOE_BLOCK
    nb = P // blk
    xmap = lambda b, be, nv: (jnp.minimum(b, nv[0] - 1), 0)
    wmap = lambda b, be, nv: (be[b], 0, 0)
    return pl.pallas_call(
        _expert_kernel,
        grid_spec=pltpu.PrefetchScalarGridSpec(
            num_scalar_prefetch=2,
            grid=(nb,),
            in_specs=[
                pl.BlockSpec((blk, D_MODEL), xmap),
                pl.BlockSpec((1, D_MODEL, D_EXPERT), wmap),
                pl.BlockSpec((1, D_MODEL, D_EXPERT), wmap),
                pl.BlockSpec((1, D_EXPERT, D_MODEL), wmap),
            ],
            out_specs=pl.BlockSpec((blk, D_MODEL), lambda b, be, nv: (b, 0)),
        ),
        out_shape=jax.ShapeDtypeStruct((P, D_MODEL), F32),
        compiler_params=_cparams(("arbitrary",)),
        name="experts",
    )(block_expert, n_valid, xbuf, wg, wu, wd)


def _combine_kernel(dest_hbm, x1_ref, route_ref, ybuf, out_ref, dsm, dsem, buf, sem):
    tm = x1_ref.shape[0]
    slot = _load_dest(dest_hbm, dsm, dsem)

    def body(r, _):
        for kk in range(TOP_K):
            d = dsm[slot, 0, TOP_K * r + kk]
            pltpu.make_async_copy(ybuf.at[pl.ds(d, 1)], buf.at[kk, pl.ds(r, 1)], sem).start()
        return 0

    lax.fori_loop(0, tm, body, 0)
    for kk in range(TOP_K):
        pltpu.make_async_copy(ybuf.at[pl.ds(0, tm)], buf.at[kk], sem).wait()
    w1 = route_ref[:, 4:5]
    w2 = route_ref[:, 5:6]
    out_ref[...] = x1_ref[...] + (buf[0] * w1 + buf[1] * w2)


def _combine(dest3, x1, route, ybuf):
    T = x1.shape[0]
    tm = min(TM_ROWS, T)
    return pl.pallas_call(
        _combine_kernel,
        grid=(T // tm,),
        in_specs=[
            pl.BlockSpec(memory_space=pl.ANY),
            pl.BlockSpec((tm, D_MODEL), lambda i: (i, 0)),
            pl.BlockSpec((tm, LANES), lambda i: (i, 0)),
            pl.BlockSpec(memory_space=pl.ANY),
        ],
        out_specs=pl.BlockSpec((tm, D_MODEL), lambda i: (i, 0)),
        out_shape=jax.ShapeDtypeStruct((T, D_MODEL), F32),
        scratch_shapes=[
            pltpu.SMEM((2, 1, TOP_K * tm), jnp.int32),
            pltpu.SemaphoreType.DMA((2,)),
            pltpu.VMEM((TOP_K, tm, D_MODEL), F32),
            pltpu.SemaphoreType.DMA,
        ],
        compiler_params=_cparams(("arbitrary",)),
        name="combine",
    )(dest3, x1, route, ybuf)


def _layer(x2d, batch, seq, lambda_init, norm1_w, w_in, b_igate, b_fgate, conv_w, conv_b, m_norm_w,
           q_norm_w, k_norm_w, lq1, lk1, lq2, lk2, a_norm_w, w_out, norm2_w,
           w_group, b_group, w_expert, b_expert, w_gate, w_up, w_down):
    T = x2d.shape[0]
    row = lambda v: v.reshape(1, -1).astype(F32)
    g0 = 2 * MIX_HALF + 2 * MIX_HALF
    g1 = g0 + 2 * N_HEADS
    w_proj = jnp.concatenate(
        [w_in[:, :g0], w_in[:, g1:], w_in[:, g0:g1], jnp.zeros((D_MODEL, LANES - 2 * N_HEADS), F32)],
        axis=1).astype(BF16)
    gate_b = jnp.concatenate([b_igate, b_fgate, jnp.zeros((LANES - 2 * N_HEADS,), F32)]).reshape(1, LANES)

    proj, gates = _inproj(x2d, row(norm1_w), w_proj)
    hm = _mlstm(proj, gates, conv_w, row(conv_b), gate_b, row(m_norm_w), batch, seq)
    qw = row(jnp.concatenate([q_norm_w, q_norm_w])) * (QK_DIM ** -0.5)
    kw = row(jnp.concatenate([k_norm_w, k_norm_w]))
    ha = _diff_attn(proj, qw, kw, row(lq1), row(lk1), row(lq2), row(lk2), row(a_norm_w),
                    batch, seq, lambda_init)

    w_router = jnp.concatenate(
        [w_expert, w_group, jnp.zeros((D_MODEL, LANES - N_EXPERTS - N_GROUPS), F32)], axis=1)
    wr_hi = w_router.astype(BF16)
    wr_lo = (w_router - wr_hi.astype(F32)).astype(BF16)
    b_router = jnp.concatenate(
        [b_expert, b_group, jnp.zeros((LANES - N_EXPERTS - N_GROUPS,), F32)]).reshape(1, LANES)
    x1, h2, route, counts = _outproj_router(hm, ha, x2d, w_out.astype(BF16), row(norm2_w),
                                            wr_hi, wr_lo, b_router)

    blk = MOE_BLOCK
    counts = counts[0, :N_EXPERTS].astype(jnp.int32)
    padded = ((counts + blk - 1) // blk) * blk
    pad_end = jnp.cumsum(padded)
    pad_start = pad_end - padded
    nb = (T * TOP_K) // blk + N_EXPERTS
    n_valid = (pad_end[-1] // blk).astype(jnp.int32)
    bstart = jnp.minimum(jnp.arange(nb, dtype=jnp.int32), n_valid - 1) * blk
    block_expert = jnp.minimum(
        jnp.sum(pad_end[None, :] <= bstart[:, None], axis=1), N_EXPERTS - 1).astype(jnp.int32)
    eid = route[:, 0:TOP_K].astype(jnp.int32)
    rank = route[:, TOP_K:2 * TOP_K].astype(jnp.int32)
    dest = pad_start[eid] + rank
    tm = min(TM_ROWS, T)
    dest3 = dest.reshape(T // tm, 1, TOP_K * tm)

    xbuf = _scatter_rows(dest3, h2, jnp.zeros((nb * blk, D_MODEL), F32))
    ybuf = _experts(block_expert, n_valid.reshape(1), xbuf,
                    w_gate.astype(BF16), w_up.astype(BF16), w_down.astype(BF16))
    return _combine(dest3, x1, route, ybuf)


def kernel(x, norm1_w, w_in, b_igate, b_fgate, conv_w, conv_b, m_norm_w, q_norm_w, k_norm_w,
           lambda_q1, lambda_k1, lambda_q2, lambda_k2, a_norm_w, w_out, norm2_w,
           w_group, b_group, w_expert, b_expert, w_gate, w_up, w_down):
    batch, seq, d = x.shape
    x2d = x.reshape(batch * seq, d)
    for l in range(norm1_w.shape[0]):
        lambda_init = 0.8 - 0.6 * math.exp(-0.3 * l)
        x2d = _layer(x2d, batch, seq, lambda_init, norm1_w[l], w_in[l], b_igate[l], b_fgate[l],
                     conv_w[l], conv_b[l], m_norm_w[l], q_norm_w[l], k_norm_w[l],
                     lambda_q1[l], lambda_k1[l], lambda_q2[l], lambda_k2[l], a_norm_w[l],
                     w_out[l], norm2_w[l], w_group[l], b_group[l], w_expert[l], b_expert[l],
                     w_gate[l], w_up[l], w_down[l])
    return x2d.reshape(batch, seq, d)
```

```python
import functools
import math

import jax
import jax.numpy as jnp
from jax import lax
from jax.experimental import pallas as pl
from jax.experimental.pallas import tpu as pltpu

F32 = jnp.float32
BF16 = jnp.bfloat16
EPS = 1e-6

D_MODEL = 1024
HEAD_DIM = 128
N_HEADS = 4
MIX_HALF = N_HEADS * HEAD_DIM
QK_DIM = 64
CONV_WIDTH = 4
ATTN_CHUNK = 64
N_GROUPS = 4
EXPERTS_PER_GROUP = 8
N_EXPERTS = N_GROUPS * EXPERTS_PER_GROUP
TOP_K = 2
D_EXPERT = D_MODEL // 2

LANES = 128
PROJ_COLS = 7 * MIX_HALF
VMEM_LIMIT = 48 * 1024 * 1024

TM_PROJ = 512
MLSTM_CHUNK = 128
TQ_ATTN = 256
TK_ATTN = 256
TM_OUT = 256
TM_ROWS = 256
MOE_BLOCK = 512


def _cparams(sem):
    return pltpu.CompilerParams(dimension_semantics=sem, vmem_limit_bytes=VMEM_LIMIT)


def _inproj_kernel(x_ref, nw_ref, w_ref, proj_ref, gate_ref):
    x = x_ref[...]
    ms = jnp.mean(x * x, axis=-1, keepdims=True)
    h = (x * lax.rsqrt(ms + EPS) * nw_ref[...]).astype(BF16)
    nb = MIX_HALF
    for j in range(PROJ_COLS // nb):
        proj_ref[:, j * nb:(j + 1) * nb] = jnp.dot(
            h, w_ref[:, j * nb:(j + 1) * nb], preferred_element_type=F32).astype(BF16)
    gate_ref[...] = jnp.dot(h, w_ref[:, PROJ_COLS:], preferred_element_type=F32)


def _inproj(x2d, nw, w):
    T = x2d.shape[0]
    tm = TM_PROJ
    return pl.pallas_call(
        _inproj_kernel,
        grid=(T // tm,),
        in_specs=[
            pl.BlockSpec((tm, D_MODEL), lambda i: (i, 0)),
            pl.BlockSpec((1, D_MODEL), lambda i: (0, 0)),
            pl.BlockSpec((D_MODEL, PROJ_COLS + LANES), lambda i: (0, 0)),
        ],
        out_specs=[
            pl.BlockSpec((tm, PROJ_COLS), lambda i: (i, 0)),
            pl.BlockSpec((tm, LANES), lambda i: (i, 0)),
        ],
        out_shape=[
            jax.ShapeDtypeStruct((T, PROJ_COLS), BF16),
            jax.ShapeDtypeStruct((T, LANES), F32),
        ],
        compiler_params=_cparams(("arbitrary",)),
        name="inproj",
    )(x2d, nw, w)


def _log_sigmoid(x):
    return jnp.minimum(x, 0.0) - jnp.log(1.0 + jnp.exp(-jnp.abs(x)))


def _cumsum_rows(x):
    n = x.shape[0]
    row = lax.broadcasted_iota(jnp.int32, x.shape, 0)
    s = 1
    while s < n:
        x = x + jnp.where(row >= s, pltpu.roll(x, s, axis=0), 0.0)
        s *= 2
    return x


def _mlstm_kernel(qp_ref, kp_ref, v_ref, o_ref, g_ref, cwq_ref, cwk_ref, cbq_ref, cbk_ref,
                  gb_ref, nw_ref, out_ref, qpad, kpad, *, seq, chunk):
    h = pl.program_id(1)
    halo = 8
    qpad[0:halo, :] = jnp.zeros((halo, HEAD_DIM), F32)
    kpad[0:halo, :] = jnp.zeros((halo, HEAD_DIM), F32)
    qpad[halo:halo + seq, :] = qp_ref[...].astype(F32)
    kpad[halo:halo + seq, :] = kp_ref[...].astype(F32)

    def conv_silu(pad_ref, r0, cw_ref, cb_ref):
        win = pad_ref[pl.ds(r0, chunk + halo), :]
        acc = cb_ref[...]
        for j in range(CONV_WIDTH):
            sh = CONV_WIDTH - 1 - j
            xs = win if sh == 0 else pltpu.roll(win, sh, axis=0)
            acc = acc + xs[halo:, :] * cw_ref[j:j + 1, :]
        return acc * jax.nn.sigmoid(acc)

    lane = lax.broadcasted_iota(jnp.int32, (chunk, LANES), 1)
    sub = lax.broadcasted_iota(jnp.int32, (LANES, chunk), 0)
    tri = (lax.broadcasted_iota(jnp.int32, (chunk, chunk), 0)
           >= lax.broadcasted_iota(jnp.int32, (chunk, chunk), 1))

    def sel_lane(x, idx):
        ln = lax.broadcasted_iota(jnp.int32, x.shape, 1)
        return jnp.sum(jnp.where(ln == idx, x, 0.0), axis=1, keepdims=True)

    def sel_sub(x, idx):
        return jnp.sum(jnp.where(sub == idx, x, 0.0), axis=0, keepdims=True)

    def step(c, carry):
        C, n, m = carry
        r0 = pl.multiple_of(c * chunk, chunk)
        q = conv_silu(qpad, r0, cwq_ref, cbq_ref)
        k = conv_silu(kpad, r0, cwk_ref, cbk_ref) * (HEAD_DIM ** -0.5)
        v = v_ref[pl.ds(r0, chunk), :]
        g = g_ref[pl.ds(r0, chunk), :] + gb_ref[...]
        bc = _cumsum_rows(_log_sigmoid(g))
        i_col = sel_lane(g, h)
        b_col = sel_lane(bc, h + N_HEADS)
        i_row = sel_sub(g.T, h)
        b_row = sel_sub(bc.T, h + N_HEADS)
        b_end = sel_lane(bc[chunk - 1:chunk, :], h + N_HEADS)

        log_d = jnp.where(tri, b_col - b_row + i_row, -jnp.inf)
        log_inter = b_col + m
        m_t = jnp.maximum(log_inter, jnp.max(log_d, axis=-1, keepdims=True))
        d = jnp.exp(log_d - m_t)
        inter = jnp.exp(log_inter - m_t)
        qb = q.astype(BF16)
        s = lax.dot_general(qb, k.astype(BF16), (((1,), (1,)), ((), ())),
                            preferred_element_type=F32) * d
        num = (jnp.dot(s.astype(BF16), v, preferred_element_type=F32)
               + inter * jnp.dot(qb, C.astype(BF16), preferred_element_type=F32))
        den = jnp.sum(s, axis=-1, keepdims=True) + inter * jnp.sum(q * n, axis=-1, keepdims=True)
        hh = num / jnp.maximum(jnp.abs(den), jnp.exp(-m_t))

        log_w = b_end - b_col + i_col
        m_new = jnp.maximum(b_end + m, jnp.max(log_w, axis=0, keepdims=True))
        kw = k * jnp.exp(log_w - m_new)
        decay = jnp.exp(b_end + m - m_new)
        C_new = decay * C + jnp.dot(kw.T.astype(BF16), v, preferred_element_type=F32)
        n_new = decay * n + jnp.sum(kw, axis=0, keepdims=True)

        hm = jax.nn.sigmoid(o_ref[pl.ds(r0, chunk), :].astype(F32)) * hh
        y = hm * lax.rsqrt(jnp.mean(hm * hm, axis=-1, keepdims=True) + EPS) * nw_ref[...]
        out_ref[pl.ds(r0, chunk), :] = y.astype(BF16)
        return C_new, n_new, m_new

    init = (jnp.zeros((HEAD_DIM, HEAD_DIM), F32), jnp.zeros((1, HEAD_DIM), F32), jnp.zeros((1, 1), F32))
    lax.fori_loop(0, seq // chunk, step, init)


def _mlstm(proj, gates, conv_w, conv_b, gate_b, m_norm_w, batch, seq):
    T = batch * seq
    chunk = min(MLSTM_CHUNK, seq)
    col = lambda off: (lambda b, h: (b, off + h))
    cst = lambda b, h: (0, 0)
    return pl.pallas_call(
        functools.partial(_mlstm_kernel, seq=seq, chunk=chunk),
        grid=(batch, N_HEADS),
        in_specs=[
            pl.BlockSpec((seq, HEAD_DIM), col(0)),
            pl.BlockSpec((seq, HEAD_DIM), col(N_HEADS)),
            pl.BlockSpec((seq, HEAD_DIM), col(2 * N_HEADS)),
            pl.BlockSpec((seq, HEAD_DIM), col(3 * N_HEADS)),
            pl.BlockSpec((seq, LANES), lambda b, h: (b, 0)),
            pl.BlockSpec((CONV_WIDTH, HEAD_DIM), lambda b, h: (0, h)),
            pl.BlockSpec((CONV_WIDTH, HEAD_DIM), lambda b, h: (0, N_HEADS + h)),
            pl.BlockSpec((1, HEAD_DIM), lambda b, h: (0, h)),
            pl.BlockSpec((1, HEAD_DIM), lambda b, h: (0, N_HEADS + h)),
            pl.BlockSpec((1, LANES), cst),
            pl.BlockSpec((1, HEAD_DIM), cst),
        ],
        out_specs=pl.BlockSpec((seq, HEAD_DIM), lambda b, h: (b, h)),
        out_shape=jax.ShapeDtypeStruct((T, MIX_HALF), BF16),
        scratch_shapes=[pltpu.VMEM((seq + 8, HEAD_DIM), F32), pltpu.VMEM((seq + 8, HEAD_DIM), F32)],
        compiler_params=_cparams(("arbitrary", "arbitrary")),
        name="mlstm",
    )(proj, proj, proj, proj, gates, conv_w, conv_w, conv_b, conv_b, gate_b, m_norm_w)


def _half_norm(x, w):
    lo = lax.broadcasted_iota(jnp.int32, x.shape, 1) < QK_DIM
    x2 = x * x
    s_lo = jnp.sum(jnp.where(lo, x2, 0.0), axis=-1, keepdims=True)
    s_hi = jnp.sum(jnp.where(lo, 0.0, x2), axis=-1, keepdims=True)
    ms = jnp.where(lo, s_lo, s_hi) * (1.0 / QK_DIM)
    return x * lax.rsqrt(ms + EPS) * w


def _attn_kernel(q_ref, k_ref, v_ref, qw_ref, kw_ref, lq1_ref, lk1_ref, lq2_ref, lk2_ref, anw_ref,
                 out_ref, kn_s, vt_s, *, seq, tq, lambda_init):
    nq = seq // tq

    def prep(i, _):
        r0 = pl.multiple_of(i * tq, tq)
        kn_s[pl.ds(r0, tq), :] = _half_norm(k_ref[pl.ds(r0, tq), :].astype(F32), kw_ref[...]).astype(BF16)
        vt_s[i] = v_ref[pl.ds(r0, tq), :].astype(F32).T.astype(BF16)
        return 0

    lax.fori_loop(0, nq, prep, 0)

    lam = (jnp.exp(jnp.sum(lq1_ref[...] * lk1_ref[...], axis=-1, keepdims=True))
           - jnp.exp(jnp.sum(lq2_ref[...] * lk2_ref[...], axis=-1, keepdims=True)) + lambda_init)
    first_map = lax.broadcasted_iota(jnp.int32, (HEAD_DIM, tq), 0) < QK_DIM
    diag_ok = (lax.broadcasted_iota(jnp.int32, (tq, tq), 0) // ATTN_CHUNK
               <= lax.broadcasted_iota(jnp.int32, (tq, tq), 1) // ATTN_CHUNK)

    for qi in range(nq):
        r0 = qi * tq
        qt = _half_norm(q_ref[r0:r0 + tq, :].astype(F32), qw_ref[...]).T
        outs = []
        for c in range(2):
            qc = (jnp.where(first_map, qt, 0.0) if c == 0 else jnp.where(first_map, 0.0, qt)).astype(BF16)
            s_d = jnp.where(diag_ok, jnp.dot(kn_s[r0:r0 + tq, :], qc, preferred_element_type=F32), -jnp.inf)
            m = jnp.max(s_d, axis=0, keepdims=True)
            if qi > 0:
                s_f = jnp.dot(kn_s[0:r0, :], qc, preferred_element_type=F32)
                m = jnp.maximum(m, jnp.max(s_f, axis=0, keepdims=True))
            p_d = jnp.exp2(s_d - m)
            l = jnp.sum(p_d, axis=0, keepdims=True)
            acc = jnp.dot(vt_s[qi], p_d.astype(BF16), preferred_element_type=F32)
            if qi > 0:
                p_f = jnp.exp2(s_f - m)
                l = l + jnp.sum(p_f, axis=0, keepdims=True)
                for j in range(qi):
                    acc = acc + jnp.dot(vt_s[j], p_f[j * tq:(j + 1) * tq].astype(BF16),
                                        preferred_element_type=F32)
            outs.append(acc * (1.0 / l))
        o = (outs[0] - lam * outs[1]).T
        y = o * lax.rsqrt(jnp.mean(o * o, axis=-1, keepdims=True) + EPS) * anw_ref[...] * (1.0 - lambda_init)
        out_ref[r0:r0 + tq, :] = y.astype(BF16)


def _diff_attn(proj, qw, kw, lq1, lk1, lq2, lk2, a_norm_w, batch, seq, lambda_init):
    T = batch * seq
    tq = min(TQ_ATTN, seq)
    cst = lambda b, h: (0, 0)
    return pl.pallas_call(
        functools.partial(_attn_kernel, seq=seq, tq=tq, lambda_init=lambda_init),
        grid=(batch, N_HEADS),
        in_specs=[
            pl.BlockSpec((seq, HEAD_DIM), lambda b, h: (b, 4 * N_HEADS + h)),
            pl.BlockSpec((seq, HEAD_DIM), lambda b, h: (b, 5 * N_HEADS + h)),
            pl.BlockSpec((seq, HEAD_DIM), lambda b, h: (b, 6 * N_HEADS + h)),
            pl.BlockSpec((1, HEAD_DIM), cst),
            pl.BlockSpec((1, HEAD_DIM), cst),
            pl.BlockSpec((1, QK_DIM), cst),
            pl.BlockSpec((1, QK_DIM), cst),
            pl.BlockSpec((1, QK_DIM), cst),
            pl.BlockSpec((1, QK_DIM), cst),
            pl.BlockSpec((1, HEAD_DIM), cst),
        ],
        out_specs=pl.BlockSpec((seq, HEAD_DIM), lambda b, h: (b, h)),
        out_shape=jax.ShapeDtypeStruct((T, MIX_HALF), BF16),
        scratch_shapes=[
            pltpu.VMEM((seq, HEAD_DIM), BF16),
            pltpu.VMEM((seq // tq, HEAD_DIM, tq), BF16),
        ],
        compiler_params=_cparams(("arbitrary", "arbitrary")),
        name="diffattn",
    )(proj, proj, proj, qw, kw, lq1, lk1, lq2, lk2, a_norm_w)


def _outproj_router_kernel(hm_ref, ha_ref, x_ref, wo_ref, nw_ref, wr_hi_ref, wr_lo_ref, br_ref,
                           x1_ref, h2_ref, route_ref, counts_ref, carry):
    i = pl.program_id(0)
    tm = x_ref.shape[0]

    @pl.when(i == 0)
    def _():
        carry[...] = jnp.zeros_like(carry)

    mix = (jnp.dot(hm_ref[...], wo_ref[0:MIX_HALF, :], preferred_element_type=F32)
           + jnp.dot(ha_ref[...], wo_ref[MIX_HALF:, :], preferred_element_type=F32))
    x1 = x_ref[...] + mix
    x1_ref[...] = x1
    h2 = x1 * lax.rsqrt(jnp.mean(x1 * x1, axis=-1, keepdims=True) + EPS) * nw_ref[...]
    h2_ref[...] = h2

    h_hi = h2.astype(BF16)
    h_lo = (h2 - h_hi.astype(F32)).astype(BF16)
    logits = (jnp.dot(h_hi, wr_hi_ref[...], preferred_element_type=F32)
              + jnp.dot(h_lo, wr_hi_ref[...], preferred_element_type=F32)
              + jnp.dot(h_hi, wr_lo_ref[...], preferred_element_type=F32)) + br_ref[...]

    lane_i = lax.broadcasted_iota(jnp.int32, (tm, LANES), 1)
    lane = lane_i.astype(F32)
    big = float(LANES)
    gmask = (lane_i >= N_EXPERTS) & (lane_i < N_EXPERTS + N_GROUPS)
    gl = jnp.where(gmask, logits, -jnp.inf)
    gmax = jnp.max(gl, axis=-1, keepdims=True)
    g_sel = jnp.min(jnp.where(gl == gmax, lane, big), axis=-1, keepdims=True) - float(N_EXPERTS)
    g_w = 1.0 / jnp.sum(jnp.where(gmask, jnp.exp(logits - gmax), 0.0), axis=-1, keepdims=True)
    e_lo = g_sel * float(EXPERTS_PER_GROUP)
    emask = (lane >= e_lo) & (lane < e_lo + float(EXPERTS_PER_GROUP))
    el = jnp.where(emask, logits, -jnp.inf)
    v1 = jnp.max(el, axis=-1, keepdims=True)
    e1 = jnp.min(jnp.where(el == v1, lane, big), axis=-1, keepdims=True)
    el2 = jnp.where(lane == e1, -jnp.inf, el)
    v2 = jnp.max(el2, axis=-1, keepdims=True)
    e2 = jnp.min(jnp.where(el2 == v2, lane, big), axis=-1, keepdims=True)
    t2 = jnp.exp(v2 - v1)
    p1 = 1.0 / (1.0 + t2)
    p2 = t2 / (1.0 + t2)

    oh1 = (lane == e1).astype(F32)
    oh2 = (lane == e2).astype(F32)
    both = oh1 + oh2
    strict = (lax.broadcasted_iota(jnp.int32, (tm, tm), 0)
              > lax.broadcasted_iota(jnp.int32, (tm, tm), 1)).astype(BF16)
    prefix = jnp.dot(strict, both.astype(BF16), preferred_element_type=F32) + carry[...]
    r1 = jnp.sum(oh1 * prefix, axis=-1, keepdims=True)
    r2 = jnp.sum(oh2 * prefix, axis=-1, keepdims=True)
    carry[...] = carry[...] + jnp.sum(both, axis=0, keepdims=True)
    counts_ref[...] = carry[...]

    route = jnp.where(lane_i == 0, e1, 0.0)
    route = jnp.where(lane_i == 1, e2, route)
    route = jnp.where(lane_i == 2, r1, route)
    route = jnp.where(lane_i == 3, r2, route)
    route = jnp.where(lane_i == 4, g_w * p1, route)
    route = jnp.where(lane_i == 5, g_w * p2, route)
    route_ref[...] = route


def _outproj_router(hm, ha, x2d, wo, nw, wr_hi, wr_lo, br):
    T = x2d.shape[0]
    tm = min(TM_OUT, T)
    row = lambda i: (i, 0)
    cst = lambda i: (0, 0)
    return pl.pallas_call(
        _outproj_router_kernel,
        grid=(T // tm,),
        in_specs=[
            pl.BlockSpec((tm, MIX_HALF), row),
            pl.BlockSpec((tm, MIX_HALF), row),
            pl.BlockSpec((tm, D_MODEL), row),
            pl.BlockSpec((D_MODEL, D_MODEL), cst),
            pl.BlockSpec((1, D_MODEL), cst),
            pl.BlockSpec((D_MODEL, LANES), cst),
            pl.BlockSpec((D_MODEL, LANES), cst),
            pl.BlockSpec((1, LANES), cst),
        ],
        out_specs=[
            pl.BlockSpec((tm, D_MODEL), row),
            pl.BlockSpec((tm, D_MODEL), row),
            pl.BlockSpec((tm, LANES), row),
            pl.BlockSpec((1, LANES), cst),
        ],
        out_shape=[
            jax.ShapeDtypeStruct((T, D_MODEL), F32),
            jax.ShapeDtypeStruct((T, D_MODEL), F32),
            jax.ShapeDtypeStruct((T, LANES), F32),
            jax.ShapeDtypeStruct((1, LANES), F32),
        ],
        scratch_shapes=[pltpu.VMEM((1, LANES), F32)],
        compiler_params=_cparams(("arbitrary",)),
        name="outproj_router",
    )(hm, ha, x2d, wo, nw, wr_hi, wr_lo, br)


def _dest_copy(dest_hbm, dsm, dsem, step, slot):
    return pltpu.make_async_copy(dest_hbm.at[step], dsm.at[slot], dsem.at[slot])


def _load_dest(dest_hbm, dsm, dsem):
    i = pl.program_id(0)
    n = pl.num_programs(0)
    slot = lax.rem(i, 2)

    @pl.when(i == 0)
    def _():
        _dest_copy(dest_hbm, dsm, dsem, 0, 0).start()

    _dest_copy(dest_hbm, dsm, dsem, i, slot).wait()

    @pl.when(i + 1 < n)
    def _():
        _dest_copy(dest_hbm, dsm, dsem, i + 1, 1 - slot).start()

    return slot


def _scatter_kernel(dest_hbm, h2_ref, xbuf_in, xbuf_out, dsm, dsem, sem):
    del xbuf_in
    tm = h2_ref.shape[0]
    slot = _load_dest(dest_hbm, dsm, dsem)

    def row_copy(r, d):
        return pltpu.make_async_copy(h2_ref.at[pl.ds(r, 1)], xbuf_out.at[pl.ds(d, 1)], sem)

    def body(r, _):
        for kk in range(TOP_K):
            row_copy(r, dsm[slot, 0, TOP_K * r + kk]).start()
        return 0

    lax.fori_loop(0, tm, body, 0)
    for _ in range(TOP_K):
        pltpu.make_async_copy(h2_ref, xbuf_out.at[pl.ds(0, tm)], sem).wait()


def _scatter_rows(dest3, h2, xbuf):
    T = h2.shape[0]
    tm = min(TM_ROWS, T)
    return pl.pallas_call(
        _scatter_kernel,
        grid=(T // tm,),
        in_specs=[
            pl.BlockSpec(memory_space=pl.ANY),
            pl.BlockSpec((tm, D_MODEL), lambda i: (i, 0)),
            pl.BlockSpec(memory_space=pl.ANY),
        ],
        out_specs=pl.BlockSpec(memory_space=pl.ANY),
        out_shape=jax.ShapeDtypeStruct(xbuf.shape, xbuf.dtype),
        scratch_shapes=[
            pltpu.SMEM((2, 1, TOP_K * tm), jnp.int32),
            pltpu.SemaphoreType.DMA((2,)),
            pltpu.SemaphoreType.DMA,
        ],
        input_output_aliases={2: 0},
        compiler_params=_cparams(("arbitrary",)),
        name="scatter_rows",
    )(dest3, h2, xbuf)


def _expert_kernel(be_ref, nv_ref, x_ref, wg_ref, wu_ref, wd_ref, y_ref):
    b = pl.program_id(0)

    @pl.when(b < nv_ref[0])
    def _():
        x = x_ref[...].astype(BF16)
        g = jnp.dot(x, wg_ref[0], preferred_element_type=F32)
        u = jnp.dot(x, wu_ref[0], preferred_element_type=F32)
        hid = (g * jax.nn.sigmoid(g) * u).astype(BF16)
        y_ref[...] = jnp.dot(hid, wd_ref[0], preferred_element_type=F32)

    @pl.when(b >= nv_ref[0])
    def _():
        y_ref[...] = jnp.zeros_like(y_ref)


def _experts(block_expert, n_valid, xbuf, wg, wu, wd):
    P = xbuf.shape[0]
    blk = MOE_BLOCK
    nb = P // blk
    xmap = lambda b, be, nv: (jnp.minimum(b, nv[0] - 1), 0)
    wmap = lambda b, be, nv: (be[b], 0, 0)
    return pl.pallas_call(
        _expert_kernel,
        grid_spec=pltpu.PrefetchScalarGridSpec(
            num_scalar_prefetch=2,
            grid=(nb,),
            in_specs=[
                pl.BlockSpec((blk, D_MODEL), xmap),
                pl.BlockSpec((1, D_MODEL, D_EXPERT), wmap),
                pl.BlockSpec((1, D_MODEL, D_EXPERT), wmap),
                pl.BlockSpec((1, D_EXPERT, D_MODEL), wmap),
            ],
            out_specs=pl.BlockSpec((blk, D_MODEL), lambda b, be, nv: (b, 0)),
        ),
        out_shape=jax.ShapeDtypeStruct((P, D_MODEL), F32),
        compiler_params=_cparams(("arbitrary",)),
        name="experts",
    )(block_expert, n_valid, xbuf, wg, wu, wd)


def _combine_kernel(dest_hbm, x1_ref, route_ref, ybuf, out_ref, dsm, dsem, buf, sem):
    tm = x1_ref.shape[0]
    slot = _load_dest(dest_hbm, dsm, dsem)

    def body(r, _):
        for kk in range(TOP_K):
            d = dsm[slot, 0, TOP_K * r + kk]
            pltpu.make_async_copy(ybuf.at[pl.ds(d, 1)], buf.at[kk, pl.ds(r, 1)], sem).start()
        return 0

    lax.fori_loop(0, tm, body, 0)
    for kk in range(TOP_K):
        pltpu.make_async_copy(ybuf.at[pl.ds(0, tm)], buf.at[kk], sem).wait()
    w1 = route_ref[:, 4:5]
    w2 = route_ref[:, 5:6]
    out_ref[...] = x1_ref[...] + (buf[0] * w1 + buf[1] * w2)


def _combine(dest3, x1, route, ybuf):
    T = x1.shape[0]
    tm = min(TM_ROWS, T)
    return pl.pallas_call(
        _combine_kernel,
        grid=(T // tm,),
        in_specs=[
            pl.BlockSpec(memory_space=pl.ANY),
            pl.BlockSpec((tm, D_MODEL), lambda i: (i, 0)),
            pl.BlockSpec((tm, LANES), lambda i: (i, 0)),
            pl.BlockSpec(memory_space=pl.ANY),
        ],
        out_specs=pl.BlockSpec((tm, D_MODEL), lambda i: (i, 0)),
        out_shape=jax.ShapeDtypeStruct((T, D_MODEL), F32),
        scratch_shapes=[
            pltpu.SMEM((2, 1, TOP_K * tm), jnp.int32),
            pltpu.SemaphoreType.DMA((2,)),
            pltpu.VMEM((TOP_K, tm, D_MODEL), F32),
            pltpu.SemaphoreType.DMA,
        ],
        compiler_params=_cparams(("arbitrary",)),
        name="combine",
    )(dest3, x1, route, ybuf)


def _layer(x2d, batch, seq, lambda_init, norm1_w, w_in, b_igate, b_fgate, conv_w, conv_b, m_norm_w,
           q_norm_w, k_norm_w, lq1, lk1, lq2, lk2, a_norm_w, w_out, norm2_w,
           w_group, b_group, w_expert, b_expert, w_gate, w_up, w_down):
    T = x2d.shape[0]
    row = lambda v: v.reshape(1, -1).astype(F32)
    g0 = 2 * MIX_HALF + 2 * MIX_HALF
    g1 = g0 + 2 * N_HEADS
    w_proj = jnp.concatenate(
        [w_in[:, :g0], w_in[:, g1:], w_in[:, g0:g1], jnp.zeros((D_MODEL, LANES - 2 * N_HEADS), F32)],
        axis=1).astype(BF16)
    gate_b = jnp.concatenate([b_igate, b_fgate, jnp.zeros((LANES - 2 * N_HEADS,), F32)]).reshape(1, LANES)

    proj, gates = _inproj(x2d, row(norm1_w), w_proj)
    hm = _mlstm(proj, gates, conv_w, row(conv_b), gate_b, row(m_norm_w), batch, seq)
    qw = row(jnp.concatenate([q_norm_w, q_norm_w])) * (QK_DIM ** -0.5 * math.log2(math.e))
    kw = row(jnp.concatenate([k_norm_w, k_norm_w]))
    ha = _diff_attn(proj, qw, kw, row(lq1), row(lk1), row(lq2), row(lk2), row(a_norm_w),
                    batch, seq, lambda_init)

    w_router = jnp.concatenate(
        [w_expert, w_group, jnp.zeros((D_MODEL, LANES - N_EXPERTS - N_GROUPS), F32)], axis=1)
    wr_hi = w_router.astype(BF16)
    wr_lo = (w_router - wr_hi.astype(F32)).astype(BF16)
    b_router = jnp.concatenate(
        [b_expert, b_group, jnp.zeros((LANES - N_EXPERTS - N_GROUPS,), F32)]).reshape(1, LANES)
    x1, h2, route, counts = _outproj_router(hm, ha, x2d, w_out.astype(BF16), row(norm2_w),
                                            wr_hi, wr_lo, b_router)

    blk = MOE_BLOCK
    counts = counts[0, :N_EXPERTS].astype(jnp.int32)
    padded = ((counts + blk - 1) // blk) * blk
    pad_end = jnp.cumsum(padded)
    pad_start = pad_end - padded
    nb = (T * TOP_K) // blk + N_EXPERTS
    n_valid = (pad_end[-1] // blk).astype(jnp.int32)
    bstart = jnp.minimum(jnp.arange(nb, dtype=jnp.int32), n_valid - 1) * blk
    block_expert = jnp.minimum(
        jnp.sum(pad_end[None, :] <= bstart[:, None], axis=1), N_EXPERTS - 1).astype(jnp.int32)
    eid = route[:, 0:TOP_K].astype(jnp.int32)
    rank = route[:, TOP_K:2 * TOP_K].astype(jnp.int32)
    dest = pad_start[eid] + rank
    tm = min(TM_ROWS, T)
    dest3 = dest.reshape(T // tm, 1, TOP_K * tm)

    xbuf = _scatter_rows(dest3, h2, jnp.zeros((nb * blk, D_MODEL), F32))
    ybuf = _experts(block_expert, n_valid.reshape(1), xbuf,
                    w_gate.astype(BF16), w_up.astype(BF16), w_down.astype(BF16))
    return _combine(dest3, x1, route, ybuf)


def kernel(x, norm1_w, w_in, b_igate, b_fgate, conv_w, conv_b, m_norm_w, q_norm_w, k_norm_w,
           lambda_q1, lambda_k1, lambda_q2, lambda_k2, a_norm_w, w_out, norm2_w,
           w_group, b_group, w_expert, b_expert, w_gate, w_up, w_down):
    batch, seq, d = x.shape
    x2d = x.reshape(batch * seq, d)
    for l in range(norm1_w.shape[0]):
        lambda_init = 0.8 - 0.6 * math.exp(-0.3 * l)
        x2d = _layer(x2d, batch, seq, lambda_init, norm1_w[l], w_in[l], b_igate[l], b_fgate[l],
                     conv_w[l], conv_b[l], m_norm_w[l], q_norm_w[l], k_norm_w[l],
                     lambda_q1[l], lambda_k1[l], lambda_q2[l], lambda_k2[l], a_norm_w[l],
                     w_out[l], norm2_w[l], w_group[l], b_group[l], w_expert[l], b_expert[l],
                     w_gate[l], w_up[l], w_down[l])
    return x2d.reshape(batch, seq, d)
```

```python
import functools
import math

import jax
import jax.numpy as jnp
from jax import lax
from jax.experimental import pallas as pl
from jax.experimental.pallas import tpu as pltpu

F32 = jnp.float32
BF16 = jnp.bfloat16
EPS = 1e-6

D_MODEL = 1024
HEAD_DIM = 128
N_HEADS = 4
MIX_HALF = N_HEADS * HEAD_DIM
QK_DIM = 64
CONV_WIDTH = 4
ATTN_CHUNK = 64
N_GROUPS = 4
EXPERTS_PER_GROUP = 8
N_EXPERTS = N_GROUPS * EXPERTS_PER_GROUP
TOP_K = 2
D_EXPERT = D_MODEL // 2

LANES = 128
PROJ_COLS = 7 * MIX_HALF
VMEM_LIMIT = 48 * 1024 * 1024

TM_PROJ = 512
MLSTM_CHUNK = 128
TQ_ATTN = 256
TK_ATTN = 256
TM_OUT = 256
TM_ROWS = 256
MOE_BLOCK = 512


def _cparams(sem):
    return pltpu.CompilerParams(dimension_semantics=sem, vmem_limit_bytes=VMEM_LIMIT)


def _inproj_kernel(x_ref, nw_ref, w_ref, proj_ref, gate_ref):
    x = x_ref[...]
    ms = jnp.mean(x * x, axis=-1, keepdims=True)
    h = (x * lax.rsqrt(ms + EPS) * nw_ref[...]).astype(BF16)
    nb = MIX_HALF
    for j in range(PROJ_COLS // nb):
        proj_ref[:, j * nb:(j + 1) * nb] = jnp.dot(
            h, w_ref[:, j * nb:(j + 1) * nb], preferred_element_type=F32).astype(BF16)
    gate_ref[...] = jnp.dot(h, w_ref[:, PROJ_COLS:], preferred_element_type=F32)


def _inproj(x2d, nw, w):
    T = x2d.shape[0]
    tm = TM_PROJ
    return pl.pallas_call(
        _inproj_kernel,
        grid=(T // tm,),
        in_specs=[
            pl.BlockSpec((tm, D_MODEL), lambda i: (i, 0)),
            pl.BlockSpec((1, D_MODEL), lambda i: (0, 0)),
            pl.BlockSpec((D_MODEL, PROJ_COLS + LANES), lambda i: (0, 0)),
        ],
        out_specs=[
            pl.BlockSpec((tm, PROJ_COLS), lambda i: (i, 0)),
            pl.BlockSpec((tm, LANES), lambda i: (i, 0)),
        ],
        out_shape=[
            jax.ShapeDtypeStruct((T, PROJ_COLS), BF16),
            jax.ShapeDtypeStruct((T, LANES), F32),
        ],
        compiler_params=_cparams(("arbitrary",)),
        name="inproj",
    )(x2d, nw, w)


def _log_sigmoid(x):
    return jnp.minimum(x, 0.0) - jnp.log(1.0 + jnp.exp(-jnp.abs(x)))


def _cumsum_rows(x):
    n = x.shape[0]
    row = lax.broadcasted_iota(jnp.int32, x.shape, 0)
    s = 1
    while s < n:
        x = x + jnp.where(row >= s, pltpu.roll(x, s, axis=0), 0.0)
        s *= 2
    return x


def _mlstm_kernel(qp_ref, kp_ref, v_ref, o_ref, g_ref, cwq_ref, cwk_ref, cbq_ref, cbk_ref,
                  gb_ref, nw_ref, out_ref, qpad, kpad, *, seq, chunk):
    h = pl.program_id(1)
    halo = 8
    qpad[0:halo, :] = jnp.zeros((halo, HEAD_DIM), F32)
    kpad[0:halo, :] = jnp.zeros((halo, HEAD_DIM), F32)
    qpad[halo:halo + seq, :] = qp_ref[...].astype(F32)
    kpad[halo:halo + seq, :] = kp_ref[...].astype(F32)

    def conv_silu(pad_ref, r0, cw_ref, cb_ref):
        win = pad_ref[pl.ds(r0, chunk + halo), :]
        acc = cb_ref[...]
        for j in range(CONV_WIDTH):
            sh = CONV_WIDTH - 1 - j
            xs = win if sh == 0 else pltpu.roll(win, sh, axis=0)
            acc = acc + xs[halo:, :] * cw_ref[j:j + 1, :]
        return acc * jax.nn.sigmoid(acc)

    lane = lax.broadcasted_iota(jnp.int32, (chunk, LANES), 1)
    sub = lax.broadcasted_iota(jnp.int32, (LANES, chunk), 0)
    tri = (lax.broadcasted_iota(jnp.int32, (chunk, chunk), 0)
           >= lax.broadcasted_iota(jnp.int32, (chunk, chunk), 1))

    def sel_lane(x, idx):
        ln = lax.broadcasted_iota(jnp.int32, x.shape, 1)
        return jnp.sum(jnp.where(ln == idx, x, 0.0), axis=1, keepdims=True)

    def sel_sub(x, idx):
        return jnp.sum(jnp.where(sub == idx, x, 0.0), axis=0, keepdims=True)

    def step(c, carry):
        C, n, m = carry
        r0 = pl.multiple_of(c * chunk, chunk)
        q = conv_silu(qpad, r0, cwq_ref, cbq_ref)
        k = conv_silu(kpad, r0, cwk_ref, cbk_ref) * (HEAD_DIM ** -0.5)
        v = v_ref[pl.ds(r0, chunk), :]
        g = g_ref[pl.ds(r0, chunk), :] + gb_ref[...]
        bc = _cumsum_rows(_log_sigmoid(g))
        i_col = sel_lane(g, h)
        b_col = sel_lane(bc, h + N_HEADS)
        i_row = sel_sub(g.T, h)
        b_row = sel_sub(bc.T, h + N_HEADS)
        b_end = sel_lane(bc[chunk - 1:chunk, :], h + N_HEADS)

        log_d = jnp.where(tri, b_col - b_row + i_row, -jnp.inf)
        log_inter = b_col + m
        m_t = jnp.maximum(log_inter, jnp.max(log_d, axis=-1, keepdims=True))
        d = jnp.exp(log_d - m_t)
        inter = jnp.exp(log_inter - m_t)
        qb = q.astype(BF16)
        s = lax.dot_general(qb, k.astype(BF16), (((1,), (1,)), ((), ())),
                            preferred_element_type=F32) * d
        num = (jnp.dot(s.astype(BF16), v, preferred_element_type=F32)
               + inter * jnp.dot(qb, C.astype(BF16), preferred_element_type=F32))
        den = jnp.sum(s, axis=-1, keepdims=True) + inter * jnp.sum(q * n, axis=-1, keepdims=True)
        hh = num / jnp.maximum(jnp.abs(den), jnp.exp(-m_t))

        log_w = b_end - b_col + i_col
        m_new = jnp.maximum(b_end + m, jnp.max(log_w, axis=0, keepdims=True))
        kw = k * jnp.exp(log_w - m_new)
        decay = jnp.exp(b_end + m - m_new)
        C_new = decay * C + jnp.dot(kw.T.astype(BF16), v, preferred_element_type=F32)
        n_new = decay * n + jnp.sum(kw, axis=0, keepdims=True)

        hm = jax.nn.sigmoid(o_ref[pl.ds(r0, chunk), :].astype(F32)) * hh
        y = hm * lax.rsqrt(jnp.mean(hm * hm, axis=-1, keepdims=True) + EPS) * nw_ref[...]
        out_ref[pl.ds(r0, chunk), :] = y.astype(BF16)
        return C_new, n_new, m_new

    init = (jnp.zeros((HEAD_DIM, HEAD_DIM), F32), jnp.zeros((1, HEAD_DIM), F32), jnp.zeros((1, 1), F32))
    lax.fori_loop(0, seq // chunk, step, init)


def _mlstm(proj, gates, conv_w, conv_b, gate_b, m_norm_w, batch, seq):
    T = batch * seq
    chunk = min(MLSTM_CHUNK, seq)
    col = lambda off: (lambda b, h: (b, off + h))
    cst = lambda b, h: (0, 0)
    return pl.pallas_call(
        functools.partial(_mlstm_kernel, seq=seq, chunk=chunk),
        grid=(batch, N_HEADS),
        in_specs=[
            pl.BlockSpec((seq, HEAD_DIM), col(0)),
            pl.BlockSpec((seq, HEAD_DIM), col(N_HEADS)),
            pl.BlockSpec((seq, HEAD_DIM), col(2 * N_HEADS)),
            pl.BlockSpec((seq, HEAD_DIM), col(3 * N_HEADS)),
            pl.BlockSpec((seq, LANES), lambda b, h: (b, 0)),
            pl.BlockSpec((CONV_WIDTH, HEAD_DIM), lambda b, h: (0, h)),
            pl.BlockSpec((CONV_WIDTH, HEAD_DIM), lambda b, h: (0, N_HEADS + h)),
            pl.BlockSpec((1, HEAD_DIM), lambda b, h: (0, h)),
            pl.BlockSpec((1, HEAD_DIM), lambda b, h: (0, N_HEADS + h)),
            pl.BlockSpec((1, LANES), cst),
            pl.BlockSpec((1, HEAD_DIM), cst),
        ],
        out_specs=pl.BlockSpec((seq, HEAD_DIM), lambda b, h: (b, h)),
        out_shape=jax.ShapeDtypeStruct((T, MIX_HALF), BF16),
        scratch_shapes=[pltpu.VMEM((seq + 8, HEAD_DIM), F32), pltpu.VMEM((seq + 8, HEAD_DIM), F32)],
        compiler_params=_cparams(("arbitrary", "arbitrary")),
        name="mlstm",
    )(proj, proj, proj, proj, gates, conv_w, conv_w, conv_b, conv_b, gate_b, m_norm_w)


def _half_norm(x, w):
    lo = lax.broadcasted_iota(jnp.int32, x.shape, 1) < QK_DIM
    x2 = x * x
    s_lo = jnp.sum(jnp.where(lo, x2, 0.0), axis=-1, keepdims=True)
    s_hi = jnp.sum(jnp.where(lo, 0.0, x2), axis=-1, keepdims=True)
    ms = jnp.where(lo, s_lo, s_hi) * (1.0 / QK_DIM)
    return x * lax.rsqrt(ms + EPS) * w


def _attn_kernel(q_ref, k_ref, v_ref, qw_ref, kw_ref, lq1_ref, lk1_ref, lq2_ref, lk2_ref, anw_ref,
                 out_ref, kn_s, vt_s, *, seq, tq, lambda_init):
    nq = seq // tq

    def prep(i, _):
        r0 = pl.multiple_of(i * tq, tq)
        kn_s[pl.ds(r0, tq), :] = _half_norm(k_ref[pl.ds(r0, tq), :].astype(F32), kw_ref[...]).astype(BF16)
        vt_s[i] = v_ref[pl.ds(r0, tq), :].astype(F32).T.astype(BF16)
        return 0

    lax.fori_loop(0, nq, prep, 0)

    lam = (jnp.exp(jnp.sum(lq1_ref[...] * lk1_ref[...], axis=-1, keepdims=True))
           - jnp.exp(jnp.sum(lq2_ref[...] * lk2_ref[...], axis=-1, keepdims=True)) + lambda_init)
    first_map = lax.broadcasted_iota(jnp.int32, (HEAD_DIM, tq), 0) < QK_DIM
    diag_ok = (lax.broadcasted_iota(jnp.int32, (tq, tq), 0) // ATTN_CHUNK
               <= lax.broadcasted_iota(jnp.int32, (tq, tq), 1) // ATTN_CHUNK)

    for qi in range(nq):
        r0 = qi * tq
        qt = _half_norm(q_ref[r0:r0 + tq, :].astype(F32), qw_ref[...]).T
        outs = []
        for c in range(2):
            qc = (jnp.where(first_map, qt, 0.0) if c == 0 else jnp.where(first_map, 0.0, qt)).astype(BF16)
            s_d = jnp.where(diag_ok, jnp.dot(kn_s[r0:r0 + tq, :], qc, preferred_element_type=F32), -jnp.inf)
            m = jnp.max(s_d, axis=0, keepdims=True)
            if qi > 0:
                s_f = jnp.dot(kn_s[0:r0, :], qc, preferred_element_type=F32)
                m = jnp.maximum(m, jnp.max(s_f, axis=0, keepdims=True))
            p_d = jnp.exp2(s_d - m)
            l = jnp.sum(p_d, axis=0, keepdims=True)
            acc = jnp.dot(vt_s[qi], p_d.astype(BF16), preferred_element_type=F32)
            if qi > 0:
                p_f = jnp.exp2(s_f - m)
                l = l + jnp.sum(p_f, axis=0, keepdims=True)
                for j in range(qi):
                    acc = acc + jnp.dot(vt_s[j], p_f[j * tq:(j + 1) * tq].astype(BF16),
                                        preferred_element_type=F32)
            outs.append(acc * (1.0 / l))
        o = (outs[0] - lam * outs[1]).T
        y = o * lax.rsqrt(jnp.mean(o * o, axis=-1, keepdims=True) + EPS) * anw_ref[...] * (1.0 - lambda_init)
        out_ref[r0:r0 + tq, :] = y.astype(BF16)


def _diff_attn(proj, qw, kw, lq1, lk1, lq2, lk2, a_norm_w, batch, seq, lambda_init):
    T = batch * seq
    tq = min(TQ_ATTN, seq)
    cst = lambda b, h: (0, 0)
    return pl.pallas_call(
        functools.partial(_attn_kernel, seq=seq, tq=tq, lambda_init=lambda_init),
        grid=(batch, N_HEADS),
        in_specs=[
            pl.BlockSpec((seq, HEAD_DIM), lambda b, h: (b, 4 * N_HEADS + h)),
            pl.BlockSpec((seq, HEAD_DIM), lambda b, h: (b, 5 * N_HEADS + h)),
            pl.BlockSpec((seq, HEAD_DIM), lambda b, h: (b, 6 * N_HEADS + h)),
            pl.BlockSpec((1, HEAD_DIM), cst),
            pl.BlockSpec((1, HEAD_DIM), cst),
            pl.BlockSpec((1, QK_DIM), cst),
            pl.BlockSpec((1, QK_DIM), cst),
            pl.BlockSpec((1, QK_DIM), cst),
            pl.BlockSpec((1, QK_DIM), cst),
            pl.BlockSpec((1, HEAD_DIM), cst),
        ],
        out_specs=pl.BlockSpec((seq, HEAD_DIM), lambda b, h: (b, h)),
        out_shape=jax.ShapeDtypeStruct((T, MIX_HALF), BF16),
        scratch_shapes=[
            pltpu.VMEM((seq, HEAD_DIM), BF16),
            pltpu.VMEM((seq // tq, HEAD_DIM, tq), BF16),
        ],
        compiler_params=_cparams(("arbitrary", "arbitrary")),
        name="diffattn",
    )(proj, proj, proj, qw, kw, lq1, lk1, lq2, lk2, a_norm_w)


def _outproj_router_kernel(hm_ref, ha_ref, x_ref, wo_ref, nw_ref, wr_hi_ref, wr_lo_ref, br_ref,
                           x1_ref, h2_ref, route_ref, counts_ref, carry):
    i = pl.program_id(0)
    tm = x_ref.shape[0]

    @pl.when(i == 0)
    def _():
        carry[...] = jnp.zeros_like(carry)

    mix = (jnp.dot(hm_ref[...], wo_ref[0:MIX_HALF, :], preferred_element_type=F32)
           + jnp.dot(ha_ref[...], wo_ref[MIX_HALF:, :], preferred_element_type=F32))
    x1 = x_ref[...] + mix
    x1_ref[...] = x1
    h2 = x1 * lax.rsqrt(jnp.mean(x1 * x1, axis=-1, keepdims=True) + EPS) * nw_ref[...]
    h2_ref[...] = h2

    h_hi = h2.astype(BF16)
    h_lo = (h2 - h_hi.astype(F32)).astype(BF16)
    logits = (jnp.dot(h_hi, wr_hi_ref[...], preferred_element_type=F32)
              + jnp.dot(h_lo, wr_hi_ref[...], preferred_element_type=F32)
              + jnp.dot(h_hi, wr_lo_ref[...], preferred_element_type=F32)) + br_ref[...]

    lane_i = lax.broadcasted_iota(jnp.int32, (tm, LANES), 1)
    lane = lane_i.astype(F32)
    big = float(LANES)
    gmask = (lane_i >= N_EXPERTS) & (lane_i < N_EXPERTS + N_GROUPS)
    gl = jnp.where(gmask, logits, -jnp.inf)
    gmax = jnp.max(gl, axis=-1, keepdims=True)
    g_sel = jnp.min(jnp.where(gl == gmax, lane, big), axis=-1, keepdims=True) - float(N_EXPERTS)
    g_w = 1.0 / jnp.sum(jnp.where(gmask, jnp.exp(logits - gmax), 0.0), axis=-1, keepdims=True)
    e_lo = g_sel * float(EXPERTS_PER_GROUP)
    emask = (lane >= e_lo) & (lane < e_lo + float(EXPERTS_PER_GROUP))
    el = jnp.where(emask, logits, -jnp.inf)
    v1 = jnp.max(el, axis=-1, keepdims=True)
    e1 = jnp.min(jnp.where(el == v1, lane, big), axis=-1, keepdims=True)
    el2 = jnp.where(lane == e1, -jnp.inf, el)
    v2 = jnp.max(el2, axis=-1, keepdims=True)
    e2 = jnp.min(jnp.where(el2 == v2, lane, big), axis=-1, keepdims=True)
    t2 = jnp.exp(v2 - v1)
    p1 = 1.0 / (1.0 + t2)
    p2 = t2 / (1.0 + t2)

    oh1 = (lane == e1).astype(F32)
    oh2 = (lane == e2).astype(F32)
    both = oh1 + oh2
    strict = (lax.broadcasted_iota(jnp.int32, (tm, tm), 0)
              > lax.broadcasted_iota(jnp.int32, (tm, tm), 1)).astype(BF16)
    prefix = jnp.dot(strict, both.astype(BF16), preferred_element_type=F32) + carry[...]
    r1 = jnp.sum(oh1 * prefix, axis=-1, keepdims=True)
    r2 = jnp.sum(oh2 * prefix, axis=-1, keepdims=True)
    carry[...] = carry[...] + jnp.sum(both, axis=0, keepdims=True)
    counts_ref[...] = carry[...]

    route = jnp.where(lane_i == 0, e1, 0.0)
    route = jnp.where(lane_i == 1, e2, route)
    route = jnp.where(lane_i == 2, r1, route)
    route = jnp.where(lane_i == 3, r2, route)
    route = jnp.where(lane_i == 4, g_w * p1, route)
    route = jnp.where(lane_i == 5, g_w * p2, route)
    route_ref[...] = route


def _outproj_router(hm, ha, x2d, wo, nw, wr_hi, wr_lo, br):
    T = x2d.shape[0]
    tm = min(TM_OUT, T)
    row = lambda i: (i, 0)
    cst = lambda i: (0, 0)
    return pl.pallas_call(
        _outproj_router_kernel,
        grid=(T // tm,),
        in_specs=[
            pl.BlockSpec((tm, MIX_HALF), row),
            pl.BlockSpec((tm, MIX_HALF), row),
            pl.BlockSpec((tm, D_MODEL), row),
            pl.BlockSpec((D_MODEL, D_MODEL), cst),
            pl.BlockSpec((1, D_MODEL), cst),
            pl.BlockSpec((D_MODEL, LANES), cst),
            pl.BlockSpec((D_MODEL, LANES), cst),
            pl.BlockSpec((1, LANES), cst),
        ],
        out_specs=[
            pl.BlockSpec((tm, D_MODEL), row),
            pl.BlockSpec((tm, D_MODEL), row),
            pl.BlockSpec((tm, LANES), row),
            pl.BlockSpec((1, LANES), cst),
        ],
        out_shape=[
            jax.ShapeDtypeStruct((T, D_MODEL), F32),
            jax.ShapeDtypeStruct((T, D_MODEL), F32),
            jax.ShapeDtypeStruct((T, LANES), F32),
            jax.ShapeDtypeStruct((1, LANES), F32),
        ],
        scratch_shapes=[pltpu.VMEM((1, LANES), F32)],
        compiler_params=_cparams(("arbitrary",)),
        name="outproj_router",
    )(hm, ha, x2d, wo, nw, wr_hi, wr_lo, br)


N_ISSUE_GROUPS = 8


def _expert_kernel(be_ref, nv_ref, idx_hbm, h2_hbm, wg_ref, wu_ref, wd_ref, out_hbm,
                   ism, isem, xg, gsem, yb, ssem):
    del be_ref
    b = pl.program_id(0)
    nv = nv_ref[0]
    blk = MOE_BLOCK
    grp = blk // N_ISSUE_GROUPS

    def idx_copy(blk_id):
        slot = lax.rem(blk_id, 4)
        return pltpu.make_async_copy(idx_hbm.at[blk_id], ism.at[slot], isem.at[slot])

    def issue_gather(blk_id, lo, hi):
        s4, s2 = lax.rem(blk_id, 4), lax.rem(blk_id, 2)
        for r in range(lo, hi):
            pltpu.make_async_copy(h2_hbm.at[pl.ds(ism[s4, 0, r], 1)], xg.at[s2, pl.ds(r, 1)],
                                  gsem.at[s2]).start()

    def issue_scatter(blk_id, lo, hi):
        s4, s2 = lax.rem(blk_id, 4), lax.rem(blk_id, 2)
        for r in range(lo, hi):
            pltpu.make_async_copy(yb.at[s2, pl.ds(r, 1)], out_hbm.at[pl.ds(ism[s4, 1, r], 1)],
                                  ssem.at[s2]).start()

    def wait_gather(blk_id):
        s2 = lax.rem(blk_id, 2)
        pltpu.make_async_copy(h2_hbm.at[pl.ds(0, blk)], xg.at[s2], gsem.at[s2]).wait()

    def wait_scatter(blk_id):
        s2 = lax.rem(blk_id, 2)
        pltpu.make_async_copy(yb.at[s2], out_hbm.at[pl.ds(0, blk)], ssem.at[s2]).wait()

    def step(first, last):
        if first:
            idx_copy(b).start()
            idx_copy(b).wait()
            issue_gather(b, 0, blk)
            if not last:
                idx_copy(b + 1).start()
        if not last:
            idx_copy(b + 1).wait()

            @pl.when(b + 2 < nv)
            def _():
                idx_copy(b + 2).start()

        @pl.when(b >= 2)
        def _():
            wait_scatter(b - 2)

        wait_gather(b)
        s2 = lax.rem(b, 2)

        def issue(g):
            if not last:
                issue_gather(b + 1, g * grp, (g + 1) * grp)
            if not first:
                issue_scatter(b - 1, g * grp, (g + 1) * grp)

        x = xg[s2].astype(BF16)
        half = D_EXPERT // 2
        acts = []
        for n in range(2):
            g_n = jnp.dot(x, wg_ref[0, :, n * half:(n + 1) * half], preferred_element_type=F32)
            issue(2 * n)
            u_n = jnp.dot(x, wu_ref[0, :, n * half:(n + 1) * half], preferred_element_type=F32)
            issue(2 * n + 1)
            acts.append((g_n * jax.nn.sigmoid(g_n) * u_n).astype(BF16))
        hid = jnp.concatenate(acts, axis=1)
        quarter = D_MODEL // 4
        for n in range(4):
            yb[s2, :, n * quarter:(n + 1) * quarter] = jnp.dot(
                hid, wd_ref[0, :, n * quarter:(n + 1) * quarter], preferred_element_type=F32)
            issue(4 + n)
        if last:
            issue_scatter(b, 0, blk)
            if not first:
                wait_scatter(b - 1)
            wait_scatter(b)

    for first in (True, False):
        for last in (True, False):
            cond_f = (b == 0) if first else (b > 0)
            cond_l = (b == nv - 1) if last else (b < nv - 1)
            pl.when(cond_f & cond_l)(functools.partial(step, first, last))

    @pl.when(b >= nv)
    def _():
        yb[0] = jnp.zeros((blk, D_MODEL), F32)
        row0 = pl.multiple_of(b * blk, blk)
        cp = pltpu.make_async_copy(yb.at[0], out_hbm.at[pl.ds(row0, blk)], ssem.at[0])
        cp.start()
        cp.wait()


def _experts(block_expert, n_valid, idx, h2, wg, wu, wd):
    nb, _, blk = idx.shape
    wmap = lambda b, be, nv: (be[b], 0, 0)
    return pl.pallas_call(
        _expert_kernel,
        grid_spec=pltpu.PrefetchScalarGridSpec(
            num_scalar_prefetch=2,
            grid=(nb,),
            in_specs=[
                pl.BlockSpec(memory_space=pl.ANY),
                pl.BlockSpec(memory_space=pl.ANY),
                pl.BlockSpec((1, D_MODEL, D_EXPERT), wmap),
                pl.BlockSpec((1, D_MODEL, D_EXPERT), wmap),
                pl.BlockSpec((1, D_EXPERT, D_MODEL), wmap),
            ],
            out_specs=pl.BlockSpec(memory_space=pl.ANY),
            scratch_shapes=[
                pltpu.SMEM((4, 2, blk), jnp.int32),
                pltpu.SemaphoreType.DMA((4,)),
                pltpu.VMEM((2, blk, D_MODEL), F32),
                pltpu.SemaphoreType.DMA((2,)),
                pltpu.VMEM((2, blk, D_MODEL), F32),
                pltpu.SemaphoreType.DMA((2,)),
            ],
        ),
        out_shape=jax.ShapeDtypeStruct((nb * blk, D_MODEL), F32),
        compiler_params=_cparams(("arbitrary",)),
        name="experts",
    )(block_expert, n_valid, idx, h2, wg, wu, wd)


def _combine_kernel(x1_ref, route_ref, y2_ref, out_ref):
    w1 = route_ref[:, 4:5]
    w2 = route_ref[:, 5:6]
    out_ref[...] = x1_ref[...] + (y2_ref[:, 0:D_MODEL] * w1 + y2_ref[:, D_MODEL:] * w2)


def _combine(x1, route, y2):
    T = x1.shape[0]
    tm = min(TM_ROWS, T)
    return pl.pallas_call(
        _combine_kernel,
        grid=(T // tm,),
        in_specs=[
            pl.BlockSpec((tm, D_MODEL), lambda i: (i, 0)),
            pl.BlockSpec((tm, LANES), lambda i: (i, 0)),
            pl.BlockSpec((tm, TOP_K * D_MODEL), lambda i: (i, 0)),
        ],
        out_specs=pl.BlockSpec((tm, D_MODEL), lambda i: (i, 0)),
        out_shape=jax.ShapeDtypeStruct((T, D_MODEL), F32),
        compiler_params=_cparams(("arbitrary",)),
        name="combine",
    )(x1, route, y2)


def _layer(x2d, batch, seq, lambda_init, norm1_w, w_in, b_igate, b_fgate, conv_w, conv_b, m_norm_w,
           q_norm_w, k_norm_w, lq1, lk1, lq2, lk2, a_norm_w, w_out, norm2_w,
           w_group, b_group, w_expert, b_expert, w_gate, w_up, w_down):
    T = x2d.shape[0]
    row = lambda v: v.reshape(1, -1).astype(F32)
    g0 = 2 * MIX_HALF + 2 * MIX_HALF
    g1 = g0 + 2 * N_HEADS
    w_proj = jnp.concatenate(
        [w_in[:, :g0], w_in[:, g1:], w_in[:, g0:g1], jnp.zeros((D_MODEL, LANES - 2 * N_HEADS), F32)],
        axis=1).astype(BF16)
    gate_b = jnp.concatenate([b_igate, b_fgate, jnp.zeros((LANES - 2 * N_HEADS,), F32)]).reshape(1, LANES)

    proj, gates = _inproj(x2d, row(norm1_w), w_proj)
    hm = _mlstm(proj, gates, conv_w, row(conv_b), gate_b, row(m_norm_w), batch, seq)
    qw = row(jnp.concatenate([q_norm_w, q_norm_w])) * (QK_DIM ** -0.5 * math.log2(math.e))
    kw = row(jnp.concatenate([k_norm_w, k_norm_w]))
    ha = _diff_attn(proj, qw, kw, row(lq1), row(lk1), row(lq2), row(lk2), row(a_norm_w),
                    batch, seq, lambda_init)

    w_router = jnp.concatenate(
        [w_expert, w_group, jnp.zeros((D_MODEL, LANES - N_EXPERTS - N_GROUPS), F32)], axis=1)
    wr_hi = w_router.astype(BF16)
    wr_lo = (w_router - wr_hi.astype(F32)).astype(BF16)
    b_router = jnp.concatenate(
        [b_expert, b_group, jnp.zeros((LANES - N_EXPERTS - N_GROUPS,), F32)]).reshape(1, LANES)
    x1, h2, route, counts = _outproj_router(hm, ha, x2d, w_out.astype(BF16), row(norm2_w),
                                            wr_hi, wr_lo, b_router)

    blk = MOE_BLOCK
    counts = counts[0, :N_EXPERTS].astype(jnp.int32)
    padded = ((counts + blk - 1) // blk) * blk
    pad_end = jnp.cumsum(padded)
    pad_start = pad_end - padded
    nb = (T * TOP_K) // blk + N_EXPERTS
    n_valid = (pad_end[-1] // blk).astype(jnp.int32)
    bstart = jnp.minimum(jnp.arange(nb, dtype=jnp.int32), n_valid - 1) * blk
    block_expert = jnp.minimum(
        jnp.sum(pad_end[None, :] <= bstart[:, None], axis=1), N_EXPERTS - 1).astype(jnp.int32)
    eid = route[:, 0:TOP_K].astype(jnp.int32)
    rank = route[:, TOP_K:2 * TOP_K].astype(jnp.int32)
    dest = (pad_start[eid] + rank).reshape(-1)
    n_assign = T * TOP_K
    n_pos = nb * blk
    inv = jnp.full((n_pos,), -1, jnp.int32).at[dest].set(
        jnp.arange(n_assign, dtype=jnp.int32), unique_indices=True)
    is_pad = inv < 0
    src_row = jnp.where(is_pad, 0, inv // TOP_K)
    spare = jnp.minimum(n_assign + jnp.cumsum(is_pad.astype(jnp.int32)) - 1, n_pos - 1)
    dst_row = jnp.where(is_pad, spare, inv)
    idx = jnp.stack([src_row.reshape(nb, blk), dst_row.reshape(nb, blk)], axis=1)

    y = _experts(block_expert, n_valid.reshape(1), idx, h2,
                 w_gate.astype(BF16), w_up.astype(BF16), w_down.astype(BF16))
    return _combine(x1, route, y.reshape(n_pos // TOP_K, TOP_K * D_MODEL))


def kernel(x, norm1_w, w_in, b_igate, b_fgate, conv_w, conv_b, m_norm_w, q_norm_w, k_norm_w,
           lambda_q1, lambda_k1, lambda_q2, lambda_k2, a_norm_w, w_out, norm2_w,
           w_group, b_group, w_expert, b_expert, w_gate, w_up, w_down):
    batch, seq, d = x.shape
    x2d = x.reshape(batch * seq, d)
    for l in range(norm1_w.shape[0]):
        lambda_init = 0.8 - 0.6 * math.exp(-0.3 * l)
        x2d = _layer(x2d, batch, seq, lambda_init, norm1_w[l], w_in[l], b_igate[l], b_fgate[l],
                     conv_w[l], conv_b[l], m_norm_w[l], q_norm_w[l], k_norm_w[l],
                     lambda_q1[l], lambda_k1[l], lambda_q2[l], lambda_k2[l], a_norm_w[l],
                     w_out[l], norm2_w[l], w_group[l], b_group[l], w_expert[l], b_expert[l],
                     w_gate[l], w_up[l], w_down[l])
    return x2d.reshape(batch, seq, d)
```

```python
import functools
import math

import jax
import jax.numpy as jnp
from jax import lax
from jax.experimental import pallas as pl
from jax.experimental.pallas import tpu as pltpu

F32 = jnp.float32
BF16 = jnp.bfloat16
EPS = 1e-6

D_MODEL = 1024
HEAD_DIM = 128
N_HEADS = 4
MIX_HALF = N_HEADS * HEAD_DIM
QK_DIM = 64
CONV_WIDTH = 4
ATTN_CHUNK = 64
N_GROUPS = 4
EXPERTS_PER_GROUP = 8
N_EXPERTS = N_GROUPS * EXPERTS_PER_GROUP
TOP_K = 2
D_EXPERT = D_MODEL // 2

LANES = 128
PROJ_COLS = 7 * MIX_HALF
VMEM_LIMIT = 48 * 1024 * 1024

TM_PROJ = 512
MLSTM_CHUNK = 128
TQ_ATTN = 256
TK_ATTN = 256
TM_OUT = 256
TM_ROWS = 256
MOE_BLOCK = 512


def _cparams(sem):
    return pltpu.CompilerParams(dimension_semantics=sem, vmem_limit_bytes=VMEM_LIMIT)


def _inproj_kernel(x_ref, nw_ref, w_ref, proj_ref, gate_ref):
    x = x_ref[...]
    ms = jnp.mean(x * x, axis=-1, keepdims=True)
    h = (x * lax.rsqrt(ms + EPS) * nw_ref[...]).astype(BF16)
    nb = MIX_HALF
    for j in range(PROJ_COLS // nb):
        proj_ref[:, j * nb:(j + 1) * nb] = jnp.dot(
            h, w_ref[:, j * nb:(j + 1) * nb], preferred_element_type=F32).astype(BF16)
    gate_ref[...] = jnp.dot(h, w_ref[:, PROJ_COLS:], preferred_element_type=F32)


def _inproj(x2d, nw, w):
    T = x2d.shape[0]
    tm = TM_PROJ
    return pl.pallas_call(
        _inproj_kernel,
        grid=(T // tm,),
        in_specs=[
            pl.BlockSpec((tm, D_MODEL), lambda i: (i, 0)),
            pl.BlockSpec((1, D_MODEL), lambda i: (0, 0)),
            pl.BlockSpec((D_MODEL, PROJ_COLS + LANES), lambda i: (0, 0)),
        ],
        out_specs=[
            pl.BlockSpec((tm, PROJ_COLS), lambda i: (i, 0)),
            pl.BlockSpec((tm, LANES), lambda i: (i, 0)),
        ],
        out_shape=[
            jax.ShapeDtypeStruct((T, PROJ_COLS), BF16),
            jax.ShapeDtypeStruct((T, LANES), F32),
        ],
        compiler_params=_cparams(("arbitrary",)),
        name="inproj",
    )(x2d, nw, w)


def _log_sigmoid(x):
    return jnp.minimum(x, 0.0) - jnp.log(1.0 + jnp.exp(-jnp.abs(x)))


def _cumsum_rows(x):
    n = x.shape[0]
    row = lax.broadcasted_iota(jnp.int32, x.shape, 0)
    s = 1
    while s < n:
        x = x + jnp.where(row >= s, pltpu.roll(x, s, axis=0), 0.0)
        s *= 2
    return x


def _mlstm_kernel(qp_ref, kp_ref, v_ref, o_ref, g_ref, cwq_ref, cwk_ref, cbq_ref, cbk_ref,
                  gb_ref, nw_ref, out_ref, qpad, kpad, *, seq, chunk):
    h = pl.program_id(1)
    halo = 8
    qpad[0:halo, :] = jnp.zeros((halo, HEAD_DIM), F32)
    kpad[0:halo, :] = jnp.zeros((halo, HEAD_DIM), F32)
    qpad[halo:halo + seq, :] = qp_ref[...].astype(F32)
    kpad[halo:halo + seq, :] = kp_ref[...].astype(F32)

    def conv_silu(pad_ref, r0, cw_ref, cb_ref):
        win = pad_ref[pl.ds(r0, chunk + halo), :]
        acc = cb_ref[...]
        for j in range(CONV_WIDTH):
            sh = CONV_WIDTH - 1 - j
            xs = win if sh == 0 else pltpu.roll(win, sh, axis=0)
            acc = acc + xs[halo:, :] * cw_ref[j:j + 1, :]
        return acc * jax.nn.sigmoid(acc)

    lane = lax.broadcasted_iota(jnp.int32, (chunk, LANES), 1)
    sub = lax.broadcasted_iota(jnp.int32, (LANES, chunk), 0)
    tri = (lax.broadcasted_iota(jnp.int32, (chunk, chunk), 0)
           >= lax.broadcasted_iota(jnp.int32, (chunk, chunk), 1))

    def sel_lane(x, idx):
        ln = lax.broadcasted_iota(jnp.int32, x.shape, 1)
        return jnp.sum(jnp.where(ln == idx, x, 0.0), axis=1, keepdims=True)

    def sel_sub(x, idx):
        return jnp.sum(jnp.where(sub == idx, x, 0.0), axis=0, keepdims=True)

    def step(c, carry):
        C, n, m = carry
        r0 = pl.multiple_of(c * chunk, chunk)
        q = conv_silu(qpad, r0, cwq_ref, cbq_ref)
        k = conv_silu(kpad, r0, cwk_ref, cbk_ref) * (HEAD_DIM ** -0.5)
        v = v_ref[pl.ds(r0, chunk), :]
        g = g_ref[pl.ds(r0, chunk), :] + gb_ref[...]
        bc = _cumsum_rows(_log_sigmoid(g))
        i_col = sel_lane(g, h)
        b_col = sel_lane(bc, h + N_HEADS)
        i_row = sel_sub(g.T, h)
        b_row = sel_sub(bc.T, h + N_HEADS)
        b_end = sel_lane(bc[chunk - 1:chunk, :], h + N_HEADS)

        log_d = jnp.where(tri, b_col - b_row + i_row, -jnp.inf)
        log_inter = b_col + m
        m_t = jnp.maximum(log_inter, jnp.max(log_d, axis=-1, keepdims=True))
        d = jnp.exp(log_d - m_t)
        inter = jnp.exp(log_inter - m_t)
        qb = q.astype(BF16)
        s = lax.dot_general(qb, k.astype(BF16), (((1,), (1,)), ((), ())),
                            preferred_element_type=F32) * d
        num = (jnp.dot(s.astype(BF16), v, preferred_element_type=F32)
               + inter * jnp.dot(qb, C.astype(BF16), preferred_element_type=F32))
        den = jnp.sum(s, axis=-1, keepdims=True) + inter * jnp.sum(q * n, axis=-1, keepdims=True)
        hh = num / jnp.maximum(jnp.abs(den), jnp.exp(-m_t))

        log_w = b_end - b_col + i_col
        m_new = jnp.maximum(b_end + m, jnp.max(log_w, axis=0, keepdims=True))
        kw = k * jnp.exp(log_w - m_new)
        decay = jnp.exp(b_end + m - m_new)
        C_new = decay * C + jnp.dot(kw.T.astype(BF16), v, preferred_element_type=F32)
        n_new = decay * n + jnp.sum(kw, axis=0, keepdims=True)

        hm = jax.nn.sigmoid(o_ref[pl.ds(r0, chunk), :].astype(F32)) * hh
        y = hm * lax.rsqrt(jnp.mean(hm * hm, axis=-1, keepdims=True) + EPS) * nw_ref[...]
        out_ref[pl.ds(r0, chunk), :] = y.astype(BF16)
        return C_new, n_new, m_new

    init = (jnp.zeros((HEAD_DIM, HEAD_DIM), F32), jnp.zeros((1, HEAD_DIM), F32), jnp.zeros((1, 1), F32))
    lax.fori_loop(0, seq // chunk, step, init)


def _mlstm(proj, gates, conv_w, conv_b, gate_b, m_norm_w, batch, seq):
    T = batch * seq
    chunk = min(MLSTM_CHUNK, seq)
    col = lambda off: (lambda b, h: (b, off + h))
    cst = lambda b, h: (0, 0)
    return pl.pallas_call(
        functools.partial(_mlstm_kernel, seq=seq, chunk=chunk),
        grid=(batch, N_HEADS),
        in_specs=[
            pl.BlockSpec((seq, HEAD_DIM), col(0)),
            pl.BlockSpec((seq, HEAD_DIM), col(N_HEADS)),
            pl.BlockSpec((seq, HEAD_DIM), col(2 * N_HEADS)),
            pl.BlockSpec((seq, HEAD_DIM), col(3 * N_HEADS)),
            pl.BlockSpec((seq, LANES), lambda b, h: (b, 0)),
            pl.BlockSpec((CONV_WIDTH, HEAD_DIM), lambda b, h: (0, h)),
            pl.BlockSpec((CONV_WIDTH, HEAD_DIM), lambda b, h: (0, N_HEADS + h)),
            pl.BlockSpec((1, HEAD_DIM), lambda b, h: (0, h)),
            pl.BlockSpec((1, HEAD_DIM), lambda b, h: (0, N_HEADS + h)),
            pl.BlockSpec((1, LANES), cst),
            pl.BlockSpec((1, HEAD_DIM), cst),
        ],
        out_specs=pl.BlockSpec((seq, HEAD_DIM), lambda b, h: (b, h)),
        out_shape=jax.ShapeDtypeStruct((T, MIX_HALF), BF16),
        scratch_shapes=[pltpu.VMEM((seq + 8, HEAD_DIM), F32), pltpu.VMEM((seq + 8, HEAD_DIM), F32)],
        compiler_params=_cparams(("arbitrary", "arbitrary")),
        name="mlstm",
    )(proj, proj, proj, proj, gates, conv_w, conv_w, conv_b, conv_b, gate_b, m_norm_w)


def _half_norm(x, w):
    lo = lax.broadcasted_iota(jnp.int32, x.shape, 1) < QK_DIM
    x2 = x * x
    s_lo = jnp.sum(jnp.where(lo, x2, 0.0), axis=-1, keepdims=True)
    s_hi = jnp.sum(jnp.where(lo, 0.0, x2), axis=-1, keepdims=True)
    ms = jnp.where(lo, s_lo, s_hi) * (1.0 / QK_DIM)
    return x * lax.rsqrt(ms + EPS) * w


def _attn_kernel(q_ref, k_ref, v_ref, qw_ref, kw_ref, lq1_ref, lk1_ref, lq2_ref, lk2_ref, anw_ref,
                 out_ref, kn_s, vt_s, *, seq, tq, lambda_init):
    nq = seq // tq

    def prep(i, _):
        r0 = pl.multiple_of(i * tq, tq)
        kn_s[pl.ds(r0, tq), :] = _half_norm(k_ref[pl.ds(r0, tq), :].astype(F32), kw_ref[...]).astype(BF16)
        vt_s[i] = v_ref[pl.ds(r0, tq), :].astype(F32).T.astype(BF16)
        return 0

    lax.fori_loop(0, nq, prep, 0)

    lam = (jnp.exp(jnp.sum(lq1_ref[...] * lk1_ref[...], axis=-1, keepdims=True))
           - jnp.exp(jnp.sum(lq2_ref[...] * lk2_ref[...], axis=-1, keepdims=True)) + lambda_init)
    first_map = lax.broadcasted_iota(jnp.int32, (HEAD_DIM, tq), 0) < QK_DIM
    diag_ok = (lax.broadcasted_iota(jnp.int32, (tq, tq), 0) // ATTN_CHUNK
               <= lax.broadcasted_iota(jnp.int32, (tq, tq), 1) // ATTN_CHUNK)

    for qi in range(nq):
        r0 = qi * tq
        qt = _half_norm(q_ref[r0:r0 + tq, :].astype(F32), qw_ref[...]).T
        outs = []
        for c in range(2):
            qc = (jnp.where(first_map, qt, 0.0) if c == 0 else jnp.where(first_map, 0.0, qt)).astype(BF16)
            s_d = jnp.where(diag_ok, jnp.dot(kn_s[r0:r0 + tq, :], qc, preferred_element_type=F32), -jnp.inf)
            m = jnp.max(s_d, axis=0, keepdims=True)
            if qi > 0:
                s_f = jnp.dot(kn_s[0:r0, :], qc, preferred_element_type=F32)
                m = jnp.maximum(m, jnp.max(s_f, axis=0, keepdims=True))
            p_d = jnp.exp2(s_d - m)
            l = jnp.sum(p_d, axis=0, keepdims=True)
            acc = jnp.dot(vt_s[qi], p_d.astype(BF16), preferred_element_type=F32)
            if qi > 0:
                p_f = jnp.exp2(s_f - m)
                l = l + jnp.sum(p_f, axis=0, keepdims=True)
                for j in range(qi):
                    acc = acc + jnp.dot(vt_s[j], p_f[j * tq:(j + 1) * tq].astype(BF16),
                                        preferred_element_type=F32)
            outs.append(acc * (1.0 / l))
        o = (outs[0] - lam * outs[1]).T
        y = o * lax.rsqrt(jnp.mean(o * o, axis=-1, keepdims=True) + EPS) * anw_ref[...] * (1.0 - lambda_init)
        out_ref[r0:r0 + tq, :] = y.astype(BF16)


def _diff_attn(proj, qw, kw, lq1, lk1, lq2, lk2, a_norm_w, batch, seq, lambda_init):
    T = batch * seq
    tq = min(TQ_ATTN, seq)
    cst = lambda b, h: (0, 0)
    return pl.pallas_call(
        functools.partial(_attn_kernel, seq=seq, tq=tq, lambda_init=lambda_init),
        grid=(batch, N_HEADS),
        in_specs=[
            pl.BlockSpec((seq, HEAD_DIM), lambda b, h: (b, 4 * N_HEADS + h)),
            pl.BlockSpec((seq, HEAD_DIM), lambda b, h: (b, 5 * N_HEADS + h)),
            pl.BlockSpec((seq, HEAD_DIM), lambda b, h: (b, 6 * N_HEADS + h)),
            pl.BlockSpec((1, HEAD_DIM), cst),
            pl.BlockSpec((1, HEAD_DIM), cst),
            pl.BlockSpec((1, QK_DIM), cst),
            pl.BlockSpec((1, QK_DIM), cst),
            pl.BlockSpec((1, QK_DIM), cst),
            pl.BlockSpec((1, QK_DIM), cst),
            pl.BlockSpec((1, HEAD_DIM), cst),
        ],
        out_specs=pl.BlockSpec((seq, HEAD_DIM), lambda b, h: (b, h)),
        out_shape=jax.ShapeDtypeStruct((T, MIX_HALF), BF16),
        scratch_shapes=[
            pltpu.VMEM((seq, HEAD_DIM), BF16),
            pltpu.VMEM((seq // tq, HEAD_DIM, tq), BF16),
        ],
        compiler_params=_cparams(("arbitrary", "arbitrary")),
        name="diffattn",
    )(proj, proj, proj, qw, kw, lq1, lk1, lq2, lk2, a_norm_w)


def _outproj_router_kernel(hm_ref, ha_ref, x_ref, wo_ref, nw_ref, wr_hi_ref, wr_lo_ref, br_ref,
                           x1_ref, h2_ref, route_ref, counts_ref, carry):
    i = pl.program_id(0)
    tm = x_ref.shape[0]

    @pl.when(i == 0)
    def _():
        carry[...] = jnp.zeros_like(carry)

    mix = (jnp.dot(hm_ref[...], wo_ref[0:MIX_HALF, :], preferred_element_type=F32)
           + jnp.dot(ha_ref[...], wo_ref[MIX_HALF:, :], preferred_element_type=F32))
    x1 = x_ref[...] + mix
    x1_ref[...] = x1
    h2 = x1 * lax.rsqrt(jnp.mean(x1 * x1, axis=-1, keepdims=True) + EPS) * nw_ref[...]
    h2_ref[...] = h2

    h_hi = h2.astype(BF16)
    h_lo = (h2 - h_hi.astype(F32)).astype(BF16)
    logits = (jnp.dot(h_hi, wr_hi_ref[...], preferred_element_type=F32)
              + jnp.dot(h_lo, wr_hi_ref[...], preferred_element_type=F32)
              + jnp.dot(h_hi, wr_lo_ref[...], preferred_element_type=F32)) + br_ref[...]

    lane_i = lax.broadcasted_iota(jnp.int32, (tm, LANES), 1)
    lane = lane_i.astype(F32)
    big = float(LANES)
    gmask = (lane_i >= N_EXPERTS) & (lane_i < N_EXPERTS + N_GROUPS)
    gl = jnp.where(gmask, logits, -jnp.inf)
    gmax = jnp.max(gl, axis=-1, keepdims=True)
    g_sel = jnp.min(jnp.where(gl == gmax, lane, big), axis=-1, keepdims=True) - float(N_EXPERTS)
    g_w = 1.0 / jnp.sum(jnp.where(gmask, jnp.exp(logits - gmax), 0.0), axis=-1, keepdims=True)
    e_lo = g_sel * float(EXPERTS_PER_GROUP)
    emask = (lane >= e_lo) & (lane < e_lo + float(EXPERTS_PER_GROUP))
    el = jnp.where(emask, logits, -jnp.inf)
    v1 = jnp.max(el, axis=-1, keepdims=True)
    e1 = jnp.min(jnp.where(el == v1, lane, big), axis=-1, keepdims=True)
    el2 = jnp.where(lane == e1, -jnp.inf, el)
    v2 = jnp.max(el2, axis=-1, keepdims=True)
    e2 = jnp.min(jnp.where(el2 == v2, lane, big), axis=-1, keepdims=True)
    t2 = jnp.exp(v2 - v1)
    p1 = 1.0 / (1.0 + t2)
    p2 = t2 / (1.0 + t2)

    oh1 = (lane == e1).astype(F32)
    oh2 = (lane == e2).astype(F32)
    both = oh1 + oh2
    strict = (lax.broadcasted_iota(jnp.int32, (tm, tm), 0)
              > lax.broadcasted_iota(jnp.int32, (tm, tm), 1)).astype(BF16)
    prefix = jnp.dot(strict, both.astype(BF16), preferred_element_type=F32) + carry[...]
    r1 = jnp.sum(oh1 * prefix, axis=-1, keepdims=True)
    r2 = jnp.sum(oh2 * prefix, axis=-1, keepdims=True)
    carry[...] = carry[...] + jnp.sum(both, axis=0, keepdims=True)
    counts_ref[...] = carry[...]

    route = jnp.where(lane_i == 0, e1, 0.0)
    route = jnp.where(lane_i == 1, e2, route)
    route = jnp.where(lane_i == 2, r1, route)
    route = jnp.where(lane_i == 3, r2, route)
    route = jnp.where(lane_i == 4, g_w * p1, route)
    route = jnp.where(lane_i == 5, g_w * p2, route)
    route_ref[...] = route


def _outproj_router(hm, ha, x2d, wo, nw, wr_hi, wr_lo, br):
    T = x2d.shape[0]
    tm = min(TM_OUT, T)
    row = lambda i: (i, 0)
    cst = lambda i: (0, 0)
    return pl.pallas_call(
        _outproj_router_kernel,
        grid=(T // tm,),
        in_specs=[
            pl.BlockSpec((tm, MIX_HALF), row),
            pl.BlockSpec((tm, MIX_HALF), row),
            pl.BlockSpec((tm, D_MODEL), row),
            pl.BlockSpec((D_MODEL, D_MODEL), cst),
            pl.BlockSpec((1, D_MODEL), cst),
            pl.BlockSpec((D_MODEL, LANES), cst),
            pl.BlockSpec((D_MODEL, LANES), cst),
            pl.BlockSpec((1, LANES), cst),
        ],
        out_specs=[
            pl.BlockSpec((tm, D_MODEL), row),
            pl.BlockSpec((tm, D_MODEL), row),
            pl.BlockSpec((tm, LANES), row),
            pl.BlockSpec((1, LANES), cst),
        ],
        out_shape=[
            jax.ShapeDtypeStruct((T, D_MODEL), F32),
            jax.ShapeDtypeStruct((T, D_MODEL), F32),
            jax.ShapeDtypeStruct((T, LANES), F32),
            jax.ShapeDtypeStruct((1, LANES), F32),
        ],
        scratch_shapes=[pltpu.VMEM((1, LANES), F32)],
        compiler_params=_cparams(("arbitrary",)),
        name="outproj_router",
    )(hm, ha, x2d, wo, nw, wr_hi, wr_lo, br)


ROW_TILE = D_MODEL // LANES


def _to_tiles(dst_ref, x, n):
    for s in range(ROW_TILE):
        dst_ref[pl.ds(s, n, stride=ROW_TILE), :] = x[:, s * LANES:(s + 1) * LANES]


def _tile_piece(src_ref, s, n, base=0):
    return src_ref[pl.ds(base * ROW_TILE + s, n, stride=ROW_TILE), :]


def _row_dma(src, src_row, dst, dst_row, sem):
    def tile(row):
        start = row * ROW_TILE
        return pl.ds(start if isinstance(row, int) else pl.multiple_of(start, ROW_TILE), ROW_TILE)
    return pltpu.make_async_copy(src.at[tile(src_row)], dst.at[tile(dst_row)], sem)


def _dest_copy(dest_hbm, dsm, dsem, step):
    slot = lax.rem(step, 2)
    return pltpu.make_async_copy(dest_hbm.at[step], dsm.at[slot], dsem.at[slot])


def _scatter_kernel(pend_ref, padded_ref, dest_hbm, h2_ref, xbuf, dsm, dsem, stage, ssem, zbuf, zsem):
    i = pl.program_id(0)
    n = pl.num_programs(0)
    tm = h2_ref.shape[0] // 2
    blk = MOE_BLOCK

    @pl.when(i == 0)
    def _():
        zbuf[...] = jnp.zeros_like(zbuf)

        def zero_copy(e):
            row0 = pl.multiple_of((pend_ref[e] - blk) * ROW_TILE, blk * ROW_TILE)
            return pltpu.make_async_copy(zbuf, xbuf.at[pl.ds(row0, blk * ROW_TILE)], zsem)

        for e in range(N_EXPERTS):
            pl.when(padded_ref[e] > 0)(lambda e=e: zero_copy(e).start())
        for e in range(N_EXPERTS):
            pl.when(padded_ref[e] > 0)(lambda e=e: zero_copy(e).wait())

        def zero_tail(bi, _):
            row0 = pl.multiple_of(bi * (blk * ROW_TILE), blk * ROW_TILE)
            cp = pltpu.make_async_copy(zbuf, xbuf.at[pl.ds(row0, blk * ROW_TILE)], zsem)
            cp.start()
            cp.wait()
            return 0

        lax.fori_loop(pend_ref[N_EXPERTS - 1] // blk, xbuf.shape[0] // (blk * ROW_TILE), zero_tail, 0)
        _dest_copy(dest_hbm, dsm, dsem, i).start()

    _dest_copy(dest_hbm, dsm, dsem, i).wait()

    @pl.when(i + 1 < n)
    def _():
        _dest_copy(dest_hbm, dsm, dsem, i + 1).start()

    slot = lax.rem(i, 2)

    def drain(half):
        for _ in range(TOP_K):
            pltpu.make_async_copy(stage.at[half], xbuf.at[pl.ds(0, tm * ROW_TILE)], ssem.at[half]).wait()

    for half in range(2):
        pl.when(i > 0)(functools.partial(drain, half))
        _to_tiles(stage.at[half], h2_ref[half * tm:(half + 1) * tm, :], tm)
        for r in range(tm):
            for kk in range(TOP_K):
                d = dsm[slot, 0, (half * tm + r) * TOP_K + kk]
                _row_dma(stage.at[half], r, xbuf, d, ssem.at[half]).start(priority=kk)

    @pl.when(i == n - 1)
    def _():
        for half in range(2):
            drain(half)


def _scatter_rows(pad_end, padded, dest3, h2, n_pos):
    T = h2.shape[0]
    n_steps = dest3.shape[0]
    tm2 = T // n_steps
    return pl.pallas_call(
        _scatter_kernel,
        grid_spec=pltpu.PrefetchScalarGridSpec(
            num_scalar_prefetch=2,
            grid=(n_steps,),
            in_specs=[
                pl.BlockSpec(memory_space=pl.ANY),
                pl.BlockSpec((tm2, D_MODEL), lambda i, pe, pd: (i, 0)),
            ],
            out_specs=pl.BlockSpec(memory_space=pl.ANY),
            scratch_shapes=[
                pltpu.SMEM((2, 1, dest3.shape[2]), jnp.int32),
                pltpu.SemaphoreType.DMA((2,)),
                pltpu.VMEM((2, (tm2 // 2) * ROW_TILE, LANES), F32),
                pltpu.SemaphoreType.DMA((2,)),
                pltpu.VMEM((MOE_BLOCK * ROW_TILE, LANES), F32),
                pltpu.SemaphoreType.DMA,
            ],
        ),
        out_shape=jax.ShapeDtypeStruct((n_pos * ROW_TILE, LANES), F32),
        compiler_params=_cparams(("arbitrary",)),
        name="scatter_rows",
    )(pad_end, padded, dest3, h2)


def _expert_kernel(be_ref, nv_ref, x_ref, wg_ref, wu_ref, wd_ref, y_ref):
    del be_ref
    b = pl.program_id(0)
    blk = MOE_BLOCK

    @pl.when(b < nv_ref[0])
    def _():
        x = jnp.concatenate([_tile_piece(x_ref, s, blk) for s in range(ROW_TILE)], axis=1).astype(BF16)
        g = jnp.dot(x, wg_ref[0], preferred_element_type=F32)
        u = jnp.dot(x, wu_ref[0], preferred_element_type=F32)
        hid = (g * jax.nn.sigmoid(g) * u).astype(BF16)
        _to_tiles(y_ref, jnp.dot(hid, wd_ref[0], preferred_element_type=F32), blk)

    @pl.when(b >= nv_ref[0])
    def _():
        y_ref[...] = jnp.zeros_like(y_ref)


def _experts(block_expert, n_valid, xbuf, wg, wu, wd):
    blk = MOE_BLOCK
    nb = xbuf.shape[0] // (blk * ROW_TILE)
    xmap = lambda b, be, nv: (jnp.minimum(b, nv[0] - 1), 0)
    wmap = lambda b, be, nv: (be[b], 0, 0)
    return pl.pallas_call(
        _expert_kernel,
        grid_spec=pltpu.PrefetchScalarGridSpec(
            num_scalar_prefetch=2,
            grid=(nb,),
            in_specs=[
                pl.BlockSpec((blk * ROW_TILE, LANES), xmap),
                pl.BlockSpec((1, D_MODEL, D_EXPERT), wmap),
                pl.BlockSpec((1, D_MODEL, D_EXPERT), wmap),
                pl.BlockSpec((1, D_EXPERT, D_MODEL), wmap),
            ],
            out_specs=pl.BlockSpec((blk * ROW_TILE, LANES), lambda b, be, nv: (b, 0)),
        ),
        out_shape=jax.ShapeDtypeStruct(xbuf.shape, F32),
        compiler_params=_cparams(("arbitrary",)),
        name="experts",
    )(block_expert, n_valid, xbuf, wg, wu, wd)


def _combine_kernel(dest_hbm, x1_ref, route_ref, ybuf, out_ref, dsm, dsem, buf, gsem):
    i = pl.program_id(0)
    n = pl.num_programs(0)
    tm = x1_ref.shape[0] // 2

    def issue_gather(step, half):
        slot = lax.rem(step, 2)
        for r in range(tm):
            for kk in range(TOP_K):
                d = dsm[slot, 0, (half * tm + r) * TOP_K + kk]
                _row_dma(ybuf, d, buf.at[half], kk * tm + r, gsem.at[half]).start(priority=kk)

    def combine_tile(half):
        for _ in range(TOP_K):
            pltpu.make_async_copy(ybuf.at[pl.ds(0, tm * ROW_TILE)],
                                  buf.at[half, pl.ds(0, tm * ROW_TILE)], gsem.at[half]).wait()
        rows = slice(half * tm, (half + 1) * tm)
        w1 = route_ref[rows, 4:5]
        w2 = route_ref[rows, 5:6]
        for s in range(ROW_TILE):
            cols = slice(s * LANES, (s + 1) * LANES)
            out_ref[rows, cols] = x1_ref[rows, cols] + (_tile_piece(buf.at[half], s, tm) * w1
                                                        + _tile_piece(buf.at[half], s, tm, base=tm) * w2)

    @pl.when(i == 0)
    def _():
        _dest_copy(dest_hbm, dsm, dsem, i).start()
        _dest_copy(dest_hbm, dsm, dsem, i).wait()
        issue_gather(i, 0)

    @pl.when(i + 1 < n)
    def _():
        _dest_copy(dest_hbm, dsm, dsem, i + 1).start()

    issue_gather(i, 1)
    combine_tile(0)

    @pl.when(i + 1 < n)
    def _():
        _dest_copy(dest_hbm, dsm, dsem, i + 1).wait()
        issue_gather(i + 1, 0)

    combine_tile(1)


def _combine(dest3, x1, route, ybuf):
    T = x1.shape[0]
    n_steps = dest3.shape[0]
    tm2 = T // n_steps
    return pl.pallas_call(
        _combine_kernel,
        grid=(n_steps,),
        in_specs=[
            pl.BlockSpec(memory_space=pl.ANY),
            pl.BlockSpec((tm2, D_MODEL), lambda i: (i, 0)),
            pl.BlockSpec((tm2, LANES), lambda i: (i, 0)),
            pl.BlockSpec(memory_space=pl.ANY),
        ],
        out_specs=pl.BlockSpec((tm2, D_MODEL), lambda i: (i, 0)),
        out_shape=jax.ShapeDtypeStruct((T, D_MODEL), F32),
        scratch_shapes=[
            pltpu.SMEM((2, 1, dest3.shape[2]), jnp.int32),
            pltpu.SemaphoreType.DMA((2,)),
            pltpu.VMEM((2, TOP_K * (tm2 // 2) * ROW_TILE, LANES), F32),
            pltpu.SemaphoreType.DMA((2,)),
        ],
        compiler_params=_cparams(("arbitrary",)),
        name="combine",
    )(dest3, x1, route, ybuf)


def _layer(x2d, batch, seq, lambda_init, norm1_w, w_in, b_igate, b_fgate, conv_w, conv_b, m_norm_w,
           q_norm_w, k_norm_w, lq1, lk1, lq2, lk2, a_norm_w, w_out, norm2_w,
           w_group, b_group, w_expert, b_expert, w_gate, w_up, w_down):
    T = x2d.shape[0]
    row = lambda v: v.reshape(1, -1).astype(F32)
    g0 = 2 * MIX_HALF + 2 * MIX_HALF
    g1 = g0 + 2 * N_HEADS
    w_proj = jnp.concatenate(
        [w_in[:, :g0], w_in[:, g1:], w_in[:, g0:g1], jnp.zeros((D_MODEL, LANES - 2 * N_HEADS), F32)],
        axis=1).astype(BF16)
    gate_b = jnp.concatenate([b_igate, b_fgate, jnp.zeros((LANES - 2 * N_HEADS,), F32)]).reshape(1, LANES)

    proj, gates = _inproj(x2d, row(norm1_w), w_proj)
    hm = _mlstm(proj, gates, conv_w, row(conv_b), gate_b, row(m_norm_w), batch, seq)
    qw = row(jnp.concatenate([q_norm_w, q_norm_w])) * (QK_DIM ** -0.5 * math.log2(math.e))
    kw = row(jnp.concatenate([k_norm_w, k_norm_w]))
    ha = _diff_attn(proj, qw, kw, row(lq1), row(lk1), row(lq2), row(lk2), row(a_norm_w),
                    batch, seq, lambda_init)

    w_router = jnp.concatenate(
        [w_expert, w_group, jnp.zeros((D_MODEL, LANES - N_EXPERTS - N_GROUPS), F32)], axis=1)
    wr_hi = w_router.astype(BF16)
    wr_lo = (w_router - wr_hi.astype(F32)).astype(BF16)
    b_router = jnp.concatenate(
        [b_expert, b_group, jnp.zeros((LANES - N_EXPERTS - N_GROUPS,), F32)]).reshape(1, LANES)
    x1, h2, route, counts = _outproj_router(hm, ha, x2d, w_out.astype(BF16), row(norm2_w),
                                            wr_hi, wr_lo, b_router)

    blk = MOE_BLOCK
    counts = counts[0, :N_EXPERTS].astype(jnp.int32)
    padded = ((counts + blk - 1) // blk) * blk
    pad_end = jnp.cumsum(padded)
    pad_start = pad_end - padded
    nb = (T * TOP_K) // blk + N_EXPERTS
    n_valid = (pad_end[-1] // blk).astype(jnp.int32)
    bstart = jnp.minimum(jnp.arange(nb, dtype=jnp.int32), n_valid - 1) * blk
    block_expert = jnp.minimum(
        jnp.sum(pad_end[None, :] <= bstart[:, None], axis=1), N_EXPERTS - 1).astype(jnp.int32)
    eid = route[:, 0:TOP_K].astype(jnp.int32)
    rank = route[:, TOP_K:2 * TOP_K].astype(jnp.int32)
    dest = pad_start[eid] + rank
    tm2 = min(2 * TM_ROWS, T)
    dest3 = dest.reshape(T // tm2, 1, tm2 * TOP_K)

    xbuf = _scatter_rows(pad_end.astype(jnp.int32), padded - counts, dest3, h2, nb * blk)
    ybuf = _experts(block_expert, n_valid.reshape(1), xbuf,
                    w_gate.astype(BF16), w_up.astype(BF16), w_down.astype(BF16))
    return _combine(dest3, x1, route, ybuf)


def kernel(x, norm1_w, w_in, b_igate, b_fgate, conv_w, conv_b, m_norm_w, q_norm_w, k_norm_w,
           lambda_q1, lambda_k1, lambda_q2, lambda_k2, a_norm_w, w_out, norm2_w,
           w_group, b_group, w_expert, b_expert, w_gate, w_up, w_down):
    batch, seq, d = x.shape
    x2d = x.reshape(batch * seq, d)
    for l in range(norm1_w.shape[0]):
        lambda_init = 0.8 - 0.6 * math.exp(-0.3 * l)
        x2d = _layer(x2d, batch, seq, lambda_init, norm1_w[l], w_in[l], b_igate[l], b_fgate[l],
                     conv_w[l], conv_b[l], m_norm_w[l], q_norm_w[l], k_norm_w[l],
                     lambda_q1[l], lambda_k1[l], lambda_q2[l], lambda_k2[l], a_norm_w[l],
                     w_out[l], norm2_w[l], w_group[l], b_group[l], w_expert[l], b_expert[l],
                     w_gate[l], w_up[l], w_down[l])
    return x2d.reshape(batch, seq, d)
```

```python
import functools
import math

import jax
import jax.numpy as jnp
from jax import lax
from jax.experimental import pallas as pl
from jax.experimental.pallas import tpu as pltpu

F32 = jnp.float32
BF16 = jnp.bfloat16
EPS = 1e-6

D_MODEL = 1024
HEAD_DIM = 128
N_HEADS = 4
MIX_HALF = N_HEADS * HEAD_DIM
QK_DIM = 64
CONV_WIDTH = 4
ATTN_CHUNK = 64
N_GROUPS = 4
EXPERTS_PER_GROUP = 8
N_EXPERTS = N_GROUPS * EXPERTS_PER_GROUP
TOP_K = 2
D_EXPERT = D_MODEL // 2

LANES = 128
PROJ_COLS = 7 * MIX_HALF
VMEM_LIMIT = 48 * 1024 * 1024

TM_PROJ = 512
MLSTM_CHUNK = 128
TQ_ATTN = 256
TK_ATTN = 256
TM_OUT = 256
TM_ROWS = 256
MOE_BLOCK = 512


def _cparams(sem):
    return pltpu.CompilerParams(dimension_semantics=sem, vmem_limit_bytes=VMEM_LIMIT)


CONV_HALO = 8


def _inproj_kernel(x_ref, nw_ref, w_ref, cw_ref, cb_ref, proj_ref, gate_ref, tail, *, tiles_per_seq):
    tm = x_ref.shape[0]

    @pl.when(lax.rem(pl.program_id(0), tiles_per_seq) == 0)
    def _():
        tail[...] = jnp.zeros_like(tail)

    x = x_ref[...]
    ms = jnp.mean(x * x, axis=-1, keepdims=True)
    h = (x * lax.rsqrt(ms + EPS) * nw_ref[...]).astype(BF16)
    nb = MIX_HALF
    for j in range(PROJ_COLS // nb):
        cols = slice(j * nb, (j + 1) * nb)
        pre = jnp.dot(h, w_ref[:, cols], preferred_element_type=F32)
        if j < 2:
            win = jnp.concatenate([tail[:, cols], pre], axis=0)
            acc = cb_ref[:, cols]
            for t in range(CONV_WIDTH):
                sh = CONV_WIDTH - 1 - t
                xs = win if sh == 0 else pltpu.roll(win, sh, axis=0)
                acc = acc + xs[CONV_HALO:, :] * cw_ref[t:t + 1, cols]
            tail[:, cols] = pre[tm - CONV_HALO:, :]
            pre = acc * jax.nn.sigmoid(acc)
            if j == 1:
                pre = pre * (HEAD_DIM ** -0.5)
        proj_ref[:, cols] = pre.astype(BF16)
    gate_ref[...] = jnp.dot(h, w_ref[:, PROJ_COLS:], preferred_element_type=F32)


def _inproj(x2d, nw, w, conv_w, conv_b, seq):
    T = x2d.shape[0]
    tm = min(TM_PROJ, seq)
    return pl.pallas_call(
        functools.partial(_inproj_kernel, tiles_per_seq=seq // tm),
        grid=(T // tm,),
        in_specs=[
            pl.BlockSpec((tm, D_MODEL), lambda i: (i, 0)),
            pl.BlockSpec((1, D_MODEL), lambda i: (0, 0)),
            pl.BlockSpec((D_MODEL, PROJ_COLS + LANES), lambda i: (0, 0)),
            pl.BlockSpec((CONV_WIDTH, 2 * MIX_HALF), lambda i: (0, 0)),
            pl.BlockSpec((1, 2 * MIX_HALF), lambda i: (0, 0)),
        ],
        out_specs=[
            pl.BlockSpec((tm, PROJ_COLS), lambda i: (i, 0)),
            pl.BlockSpec((tm, LANES), lambda i: (i, 0)),
        ],
        out_shape=[
            jax.ShapeDtypeStruct((T, PROJ_COLS), BF16),
            jax.ShapeDtypeStruct((T, LANES), F32),
        ],
        scratch_shapes=[pltpu.VMEM((CONV_HALO, 2 * MIX_HALF), F32)],
        compiler_params=_cparams(("arbitrary",)),
        name="inproj",
    )(x2d, nw, w, conv_w, conv_b)


def _log_sigmoid(x):
    return jnp.minimum(x, 0.0) - jnp.log(1.0 + jnp.exp(-jnp.abs(x)))


def _cumsum_lanes(x):
    n = x.shape[1]
    lane = lax.broadcasted_iota(jnp.int32, x.shape, 1)
    s = 1
    while s < n:
        x = x + jnp.where(lane >= s, pltpu.roll(x, s, axis=1), 0.0)
        s *= 2
    return x


def _mlstm_kernel(p_ref, g_ref, gb_ref, nw_ref, out_ref, gt_s, bt_s, bc_s, *c_s, seq, chunk):
    for c_h in c_s:
        c_h[...] = jnp.zeros_like(c_h)
    src_le_out = (lax.broadcasted_iota(jnp.int32, (chunk, chunk), 0)
                  <= lax.broadcasted_iota(jnp.int32, (chunk, chunk), 1))
    gate_rows = 2 * N_HEADS
    nt = (((1,), (1,)), ((), ()))
    split_row = lax.broadcasted_iota(jnp.int32, (16, HEAD_DIM), 0)

    n_chunks = seq // chunk
    for c in range(n_chunks):
        g = g_ref[c * chunk:(c + 1) * chunk, :] + gb_ref[...]
        gt_s[c * gate_rows:(c + 1) * gate_rows, :] = g.T[0:gate_rows, :]
    bt_s[...] = _cumsum_lanes(_log_sigmoid(gt_s[...]))
    for c in range(n_chunks):
        b_t = bt_s[c * gate_rows:(c + 1) * gate_rows, :]
        bc_s[c * chunk:(c + 1) * chunk, :] = pltpu.roll(
            jnp.concatenate([b_t, jnp.zeros((LANES - gate_rows, chunk), F32)], axis=0).T,
            LANES - N_HEADS, axis=1)

    def step(c, carry):
        n_all, m_row = carry
        r0 = pl.multiple_of(c * chunk, chunk)
        rows = pl.ds(r0, chunk)
        g = g_ref[rows, :] + gb_ref[...]
        b_t = bt_s[pl.ds(pl.multiple_of(c * gate_rows, gate_rows), gate_rows), :]
        b_c = bc_s[rows, :]
        a_all = g - b_c
        b_end = b_c[chunk - 1:chunk, :]
        log_w = b_end + a_all
        m_new = jnp.maximum(b_end + m_row, jnp.max(log_w, axis=0, keepdims=True))
        w_all = jnp.exp(log_w - m_new)
        decay_row = jnp.exp(b_end + m_row - m_new)

        n_new = []
        for h in range(N_HEADS):
            n = n_all[h]
            ct = c_s[h][...]
            q = p_ref[rows, h * HEAD_DIM:(h + 1) * HEAD_DIM]
            k = p_ref[rows, MIX_HALF + h * HEAD_DIM:MIX_HALF + (h + 1) * HEAD_DIM]
            v = p_ref[rows, 2 * MIX_HALF + h * HEAD_DIM:2 * MIX_HALF + (h + 1) * HEAD_DIM]
            o = p_ref[rows, 3 * MIX_HALF + h * HEAD_DIM:3 * MIX_HALF + (h + 1) * HEAD_DIM]
            b_row = b_t[N_HEADS + h:N_HEADS + h + 1, :]
            m = m_row[:, h:h + 1]
            decay = decay_row[:, h:h + 1]

            log_d = jnp.where(src_le_out, b_row + a_all[:, h:h + 1], -jnp.inf)
            log_inter = b_row + m
            m_t = jnp.maximum(log_inter, jnp.max(log_d, axis=0, keepdims=True))
            inter = jnp.exp(log_inter - m_t)
            st = lax.dot_general(k, q, nt, preferred_element_type=F32) * jnp.exp(log_d - m_t)
            vt = v.astype(F32).T.astype(BF16)
            num = (jnp.dot(vt, st.astype(BF16), preferred_element_type=F32)
                   + inter * lax.dot_general(ct.astype(BF16), q, nt, preferred_element_type=F32))
            n_hi = n.astype(BF16).astype(F32)
            n_mat = jnp.where(split_row == 0, n_hi, jnp.where(split_row == 1, n - n_hi, 0.0)).astype(BF16)
            nq = lax.dot_general(n_mat, q, nt, preferred_element_type=F32)
            den = jnp.sum(st, axis=0, keepdims=True) + inter * (nq[0:1, :] + nq[1:2, :])
            hh = (num / jnp.maximum(jnp.abs(den), jnp.exp(-m_t))).T

            kw = k.astype(F32) * w_all[:, h:h + 1]
            c_s[h][...] = decay * ct + jnp.dot(vt, kw.astype(BF16), preferred_element_type=F32)
            n_new.append(decay * n + jnp.sum(kw, axis=0, keepdims=True))

            hm = jax.nn.sigmoid(o.astype(F32)) * hh
            y = hm * lax.rsqrt(jnp.mean(hm * hm, axis=-1, keepdims=True) + EPS) * nw_ref[...]
            out_ref[rows, h * HEAD_DIM:(h + 1) * HEAD_DIM] = y.astype(BF16)
        return tuple(n_new), m_new

    init = (tuple(jnp.zeros((1, HEAD_DIM), F32) for _ in range(N_HEADS)), jnp.zeros((1, LANES), F32))
    lax.fori_loop(0, seq // chunk, step, init)


def _mlstm(proj, gates, gate_b, m_norm_w, batch, seq):
    T = batch * seq
    chunk = MLSTM_CHUNK
    assert chunk == LANES and seq % chunk == 0
    cst = lambda b: (0, 0)
    return pl.pallas_call(
        functools.partial(_mlstm_kernel, seq=seq, chunk=chunk),
        grid=(batch,),
        in_specs=[
            pl.BlockSpec((seq, 4 * MIX_HALF), lambda b: (b, 0)),
            pl.BlockSpec((seq, LANES), lambda b: (b, 0)),
            pl.BlockSpec((1, LANES), cst),
            pl.BlockSpec((1, HEAD_DIM), cst),
        ],
        out_specs=pl.BlockSpec((seq, MIX_HALF), lambda b: (b, 0)),
        out_shape=jax.ShapeDtypeStruct((T, MIX_HALF), BF16),
        scratch_shapes=[
            pltpu.VMEM((seq // chunk * 2 * N_HEADS, LANES), F32),
            pltpu.VMEM((seq // chunk * 2 * N_HEADS, LANES), F32),
            pltpu.VMEM((seq, LANES), F32),
        ] + [pltpu.VMEM((HEAD_DIM, HEAD_DIM), F32) for _ in range(N_HEADS)],
        compiler_params=_cparams(("arbitrary",)),
        name="mlstm",
    )(proj, gates, gate_b, m_norm_w)


def _half_norm(x, w):
    lo = lax.broadcasted_iota(jnp.int32, x.shape, 1) < QK_DIM
    x2 = x * x
    s_lo = jnp.sum(jnp.where(lo, x2, 0.0), axis=-1, keepdims=True)
    s_hi = jnp.sum(jnp.where(lo, 0.0, x2), axis=-1, keepdims=True)
    ms = jnp.where(lo, s_lo, s_hi) * (1.0 / QK_DIM)
    return x * lax.rsqrt(ms + EPS) * w


def _attn_kernel(q_ref, k_ref, v_ref, qw_ref, kw_ref, lq1_ref, lk1_ref, lq2_ref, lk2_ref, anw_ref,
                 out_ref, kn_s, vt_s, *, seq, tq, lambda_init):
    nq = seq // tq

    def prep(i, _):
        r0 = pl.multiple_of(i * tq, tq)
        kn_s[pl.ds(r0, tq), :] = _half_norm(k_ref[pl.ds(r0, tq), :].astype(F32), kw_ref[...]).astype(BF16)
        vt_s[i] = v_ref[pl.ds(r0, tq), :].astype(F32).T.astype(BF16)
        return 0

    lax.fori_loop(0, nq, prep, 0)

    lam = (jnp.exp(jnp.sum(lq1_ref[...] * lk1_ref[...], axis=-1, keepdims=True))
           - jnp.exp(jnp.sum(lq2_ref[...] * lk2_ref[...], axis=-1, keepdims=True)) + lambda_init)
    first_map = lax.broadcasted_iota(jnp.int32, (HEAD_DIM, tq), 0) < QK_DIM
    q_in_tile = lax.rem(lax.broadcasted_iota(jnp.int32, (tq, 2 * tq), 1), tq)
    diag_ok = lax.broadcasted_iota(jnp.int32, (tq, 2 * tq), 0) // ATTN_CHUNK <= q_in_tile // ATTN_CHUNK

    for qi in range(nq):
        r0 = qi * tq
        qt = _half_norm(q_ref[r0:r0 + tq, :].astype(F32), qw_ref[...]).T
        qc = jnp.concatenate([jnp.where(first_map, qt, 0.0), jnp.where(first_map, 0.0, qt)],
                             axis=1).astype(BF16)
        s_d = jnp.where(diag_ok, jnp.dot(kn_s[r0:r0 + tq, :], qc, preferred_element_type=F32), -jnp.inf)
        m = jnp.max(s_d, axis=0, keepdims=True)
        if qi > 0:
            s_f = jnp.dot(kn_s[0:r0, :], qc, preferred_element_type=F32)
            m = jnp.maximum(m, jnp.max(s_f, axis=0, keepdims=True))
        p_d = jnp.exp2(s_d - m)
        l = jnp.sum(p_d, axis=0, keepdims=True)
        acc = jnp.dot(vt_s[qi], p_d.astype(BF16), preferred_element_type=F32)
        if qi > 0:
            p_f = jnp.exp2(s_f - m)
            l = l + jnp.sum(p_f, axis=0, keepdims=True)
            for j in range(qi):
                acc = acc + jnp.dot(vt_s[j], p_f[j * tq:(j + 1) * tq].astype(BF16),
                                    preferred_element_type=F32)
        on = acc * (1.0 / l)
        o = (on[:, 0:tq] - lam * on[:, tq:]).T
        y = o * lax.rsqrt(jnp.mean(o * o, axis=-1, keepdims=True) + EPS) * anw_ref[...] * (1.0 - lambda_init)
        out_ref[r0:r0 + tq, :] = y.astype(BF16)


def _diff_attn(proj, qw, kw, lq1, lk1, lq2, lk2, a_norm_w, batch, seq, lambda_init):
    T = batch * seq
    tq = min(TQ_ATTN, seq)
    cst = lambda b, h: (0, 0)
    return pl.pallas_call(
        functools.partial(_attn_kernel, seq=seq, tq=tq, lambda_init=lambda_init),
        grid=(batch, N_HEADS),
        in_specs=[
            pl.BlockSpec((seq, HEAD_DIM), lambda b, h: (b, 4 * N_HEADS + h)),
            pl.BlockSpec((seq, HEAD_DIM), lambda b, h: (b, 5 * N_HEADS + h)),
            pl.BlockSpec((seq, HEAD_DIM), lambda b, h: (b, 6 * N_HEADS + h)),
            pl.BlockSpec((1, HEAD_DIM), cst),
            pl.BlockSpec((1, HEAD_DIM), cst),
            pl.BlockSpec((1, QK_DIM), cst),
            pl.BlockSpec((1, QK_DIM), cst),
            pl.BlockSpec((1, QK_DIM), cst),
            pl.BlockSpec((1, QK_DIM), cst),
            pl.BlockSpec((1, HEAD_DIM), cst),
        ],
        out_specs=pl.BlockSpec((seq, HEAD_DIM), lambda b, h: (b, h)),
        out_shape=jax.ShapeDtypeStruct((T, MIX_HALF), BF16),
        scratch_shapes=[
            pltpu.VMEM((seq, HEAD_DIM), BF16),
            pltpu.VMEM((seq // tq, HEAD_DIM, tq), BF16),
        ],
        compiler_params=_cparams(("arbitrary", "arbitrary")),
        name="diffattn",
    )(proj, proj, proj, qw, kw, lq1, lk1, lq2, lk2, a_norm_w)


def _outproj_router_kernel(hm_ref, ha_ref, x_ref, wo_ref, nw_ref, wr_hi_ref, wr_lo_ref, br_ref,
                           x1_ref, h2_ref, route_ref, counts_ref, carry):
    i = pl.program_id(0)
    tm = x_ref.shape[0]

    @pl.when(i == 0)
    def _():
        carry[...] = jnp.zeros_like(carry)

    mix = (jnp.dot(hm_ref[...], wo_ref[0:MIX_HALF, :], preferred_element_type=F32)
           + jnp.dot(ha_ref[...], wo_ref[MIX_HALF:, :], preferred_element_type=F32))
    x1 = x_ref[...] + mix
    x1_ref[...] = x1
    h2 = x1 * lax.rsqrt(jnp.mean(x1 * x1, axis=-1, keepdims=True) + EPS) * nw_ref[...]
    h2_ref[...] = h2

    h_hi = h2.astype(BF16)
    h_lo = (h2 - h_hi.astype(F32)).astype(BF16)
    logits = (jnp.dot(h_hi, wr_hi_ref[...], preferred_element_type=F32)
              + jnp.dot(h_lo, wr_hi_ref[...], preferred_element_type=F32)
              + jnp.dot(h_hi, wr_lo_ref[...], preferred_element_type=F32)) + br_ref[...]

    lane_i = lax.broadcasted_iota(jnp.int32, (tm, LANES), 1)
    lane = lane_i.astype(F32)
    big = float(LANES)
    gmask = (lane_i >= N_EXPERTS) & (lane_i < N_EXPERTS + N_GROUPS)
    gl = jnp.where(gmask, logits, -jnp.inf)
    gmax = jnp.max(gl, axis=-1, keepdims=True)
    g_sel = jnp.min(jnp.where(gl == gmax, lane, big), axis=-1, keepdims=True) - float(N_EXPERTS)
    g_w = 1.0 / jnp.sum(jnp.where(gmask, jnp.exp(logits - gmax), 0.0), axis=-1, keepdims=True)
    e_lo = g_sel * float(EXPERTS_PER_GROUP)
    emask = (lane >= e_lo) & (lane < e_lo + float(EXPERTS_PER_GROUP))
    el = jnp.where(emask, logits, -jnp.inf)
    v1 = jnp.max(el, axis=-1, keepdims=True)
    e1 = jnp.min(jnp.where(el == v1, lane, big), axis=-1, keepdims=True)
    el2 = jnp.where(lane == e1, -jnp.inf, el)
    v2 = jnp.max(el2, axis=-1, keepdims=True)
    e2 = jnp.min(jnp.where(el2 == v2, lane, big), axis=-1, keepdims=True)
    t2 = jnp.exp(v2 - v1)
    p1 = 1.0 / (1.0 + t2)
    p2 = t2 / (1.0 + t2)

    oh1 = (lane == e1).astype(F32)
    oh2 = (lane == e2).astype(F32)
    both = oh1 + oh2
    strict = (lax.broadcasted_iota(jnp.int32, (tm, tm), 0)
              > lax.broadcasted_iota(jnp.int32, (tm, tm), 1)).astype(BF16)
    prefix = jnp.dot(strict, both.astype(BF16), preferred_element_type=F32) + carry[...]
    r1 = jnp.sum(oh1 * prefix, axis=-1, keepdims=True)
    r2 = jnp.sum(oh2 * prefix, axis=-1, keepdims=True)
    carry[...] = carry[...] + jnp.sum(both, axis=0, keepdims=True)
    counts_ref[...] = carry[...]

    route = jnp.where(lane_i == 0, e1, 0.0)
    route = jnp.where(lane_i == 1, e2, route)
    route = jnp.where(lane_i == 2, r1, route)
    route = jnp.where(lane_i == 3, r2, route)
    route = jnp.where(lane_i == 4, g_w * p1, route)
    route = jnp.where(lane_i == 5, g_w * p2, route)
    route_ref[...] = route


def _outproj_router(hm, ha, x2d, wo, nw, wr_hi, wr_lo, br):
    T = x2d.shape[0]
    tm = min(TM_OUT, T)
    row = lambda i: (i, 0)
    cst = lambda i: (0, 0)
    return pl.pallas_call(
        _outproj_router_kernel,
        grid=(T // tm,),
        in_specs=[
            pl.BlockSpec((tm, MIX_HALF), row),
            pl.BlockSpec((tm, MIX_HALF), row),
            pl.BlockSpec((tm, D_MODEL), row),
            pl.BlockSpec((D_MODEL, D_MODEL), cst),
            pl.BlockSpec((1, D_MODEL), cst),
            pl.BlockSpec((D_MODEL, LANES), cst),
            pl.BlockSpec((D_MODEL, LANES), cst),
            pl.BlockSpec((1, LANES), cst),
        ],
        out_specs=[
            pl.BlockSpec((tm, D_MODEL), row),
            pl.BlockSpec((tm, D_MODEL), row),
            pl.BlockSpec((tm, LANES), row),
            pl.BlockSpec((1, LANES), cst),
        ],
        out_shape=[
            jax.ShapeDtypeStruct((T, D_MODEL), F32),
            jax.ShapeDtypeStruct((T, D_MODEL), F32),
            jax.ShapeDtypeStruct((T, LANES), F32),
            jax.ShapeDtypeStruct((1, LANES), F32),
        ],
        scratch_shapes=[pltpu.VMEM((1, LANES), F32)],
        compiler_params=_cparams(("arbitrary",)),
        name="outproj_router",
    )(hm, ha, x2d, wo, nw, wr_hi, wr_lo, br)


ROW_TILE = D_MODEL // LANES


def _to_tiles(dst_ref, x, n):
    for s in range(ROW_TILE):
        dst_ref[pl.ds(s, n, stride=ROW_TILE), :] = x[:, s * LANES:(s + 1) * LANES]


def _tile_piece(src_ref, s, n, base=0):
    return src_ref[pl.ds(base * ROW_TILE + s, n, stride=ROW_TILE), :]


def _row_dma(src, src_row, dst, dst_row, sem):
    def tile(row):
        start = row * ROW_TILE
        return pl.ds(start if isinstance(row, int) else pl.multiple_of(start, ROW_TILE), ROW_TILE)
    return pltpu.make_async_copy(src.at[tile(src_row)], dst.at[tile(dst_row)], sem)


def _dest_copy(dest_hbm, dsm, dsem, step):
    slot = lax.rem(step, 2)
    return pltpu.make_async_copy(dest_hbm.at[step], dsm.at[slot], dsem.at[slot])


def _scatter_kernel(pend_ref, padded_ref, dest_hbm, h2_ref, xbuf, dsm, dsem, stage, ssem, zbuf, zsem):
    i = pl.program_id(0)
    n = pl.num_programs(0)
    tm = h2_ref.shape[0] // 2
    blk = MOE_BLOCK

    @pl.when(i == 0)
    def _():
        zbuf[...] = jnp.zeros_like(zbuf)

        def zero_copy(e):
            row0 = pl.multiple_of((pend_ref[e] - blk) * ROW_TILE, blk * ROW_TILE)
            return pltpu.make_async_copy(zbuf, xbuf.at[pl.ds(row0, blk * ROW_TILE)], zsem)

        for e in range(N_EXPERTS):
            pl.when(padded_ref[e] > 0)(lambda e=e: zero_copy(e).start())
        for e in range(N_EXPERTS):
            pl.when(padded_ref[e] > 0)(lambda e=e: zero_copy(e).wait())

        def zero_tail(bi, _):
            row0 = pl.multiple_of(bi * (blk * ROW_TILE), blk * ROW_TILE)
            cp = pltpu.make_async_copy(zbuf, xbuf.at[pl.ds(row0, blk * ROW_TILE)], zsem)
            cp.start()
            cp.wait()
            return 0

        lax.fori_loop(pend_ref[N_EXPERTS - 1] // blk, xbuf.shape[0] // (blk * ROW_TILE), zero_tail, 0)
        _dest_copy(dest_hbm, dsm, dsem, i).start()

    _dest_copy(dest_hbm, dsm, dsem, i).wait()

    @pl.when(i + 1 < n)
    def _():
        _dest_copy(dest_hbm, dsm, dsem, i + 1).start()

    slot = lax.rem(i, 2)

    def drain(half):
        for _ in range(TOP_K):
            pltpu.make_async_copy(stage.at[half], xbuf.at[pl.ds(0, tm * ROW_TILE)], ssem.at[half]).wait()

    for half in range(2):
        pl.when(i > 0)(functools.partial(drain, half))
        _to_tiles(stage.at[half], h2_ref[half * tm:(half + 1) * tm, :], tm)
        for r in range(tm):
            for kk in range(TOP_K):
                d = dsm[slot, 0, (half * tm + r) * TOP_K + kk]
                _row_dma(stage.at[half], r, xbuf, d, ssem.at[half]).start(priority=kk)

    @pl.when(i == n - 1)
    def _():
        for half in range(2):
            drain(half)


def _scatter_rows(pad_end, padded, dest3, h2, n_pos):
    T = h2.shape[0]
    n_steps = dest3.shape[0]
    tm2 = T // n_steps
    return pl.pallas_call(
        _scatter_kernel,
        grid_spec=pltpu.PrefetchScalarGridSpec(
            num_scalar_prefetch=2,
            grid=(n_steps,),
            in_specs=[
                pl.BlockSpec(memory_space=pl.ANY),
                pl.BlockSpec((tm2, D_MODEL), lambda i, pe, pd: (i, 0)),
            ],
            out_specs=pl.BlockSpec(memory_space=pl.ANY),
            scratch_shapes=[
                pltpu.SMEM((2, 1, dest3.shape[2]), jnp.int32),
                pltpu.SemaphoreType.DMA((2,)),
                pltpu.VMEM((2, (tm2 // 2) * ROW_TILE, LANES), F32),
                pltpu.SemaphoreType.DMA((2,)),
                pltpu.VMEM((MOE_BLOCK * ROW_TILE, LANES), F32),
                pltpu.SemaphoreType.DMA,
            ],
        ),
        out_shape=jax.ShapeDtypeStruct((n_pos * ROW_TILE, LANES), F32),
        compiler_params=_cparams(("arbitrary",)),
        name="scatter_rows",
    )(pad_end, padded, dest3, h2)


def _expert_kernel(be_ref, nv_ref, x_ref, wg_ref, wu_ref, wd_ref, y_ref):
    del be_ref
    b = pl.program_id(0)
    blk = MOE_BLOCK

    @pl.when(b < nv_ref[0])
    def _():
        x = jnp.concatenate([_tile_piece(x_ref, s, blk) for s in range(ROW_TILE)], axis=1).astype(BF16)
        g = jnp.dot(x, wg_ref[0], preferred_element_type=F32)
        u = jnp.dot(x, wu_ref[0], preferred_element_type=F32)
        hid = (g * jax.nn.sigmoid(g) * u).astype(BF16)
        _to_tiles(y_ref, jnp.dot(hid, wd_ref[0], preferred_element_type=F32), blk)

    @pl.when(b >= nv_ref[0])
    def _():
        y_ref[...] = jnp.zeros_like(y_ref)


def _experts(block_expert, n_valid, xbuf, wg, wu, wd):
    blk = MOE_BLOCK
    nb = xbuf.shape[0] // (blk * ROW_TILE)
    xmap = lambda b, be, nv: (jnp.minimum(b, nv[0] - 1), 0)
    wmap = lambda b, be, nv: (be[b], 0, 0)
    return pl.pallas_call(
        _expert_kernel,
        grid_spec=pltpu.PrefetchScalarGridSpec(
            num_scalar_prefetch=2,
            grid=(nb,),
            in_specs=[
                pl.BlockSpec((blk * ROW_TILE, LANES), xmap),
                pl.BlockSpec((1, D_MODEL, D_EXPERT), wmap),
                pl.BlockSpec((1, D_MODEL, D_EXPERT), wmap),
                pl.BlockSpec((1, D_EXPERT, D_MODEL), wmap),
            ],
            out_specs=pl.BlockSpec((blk * ROW_TILE, LANES), lambda b, be, nv: (b, 0)),
        ),
        out_shape=jax.ShapeDtypeStruct(xbuf.shape, F32),
        compiler_params=_cparams(("arbitrary",)),
        name="experts",
    )(block_expert, n_valid, xbuf, wg, wu, wd)


def _combine_kernel(dest_hbm, x1_ref, route_ref, ybuf, out_ref, dsm, dsem, buf, gsem):
    i = pl.program_id(0)
    n = pl.num_programs(0)
    tm = x1_ref.shape[0] // 2

    def issue_gather(step, half):
        slot = lax.rem(step, 2)
        for r in range(tm):
            for kk in range(TOP_K):
                d = dsm[slot, 0, (half * tm + r) * TOP_K + kk]
                _row_dma(ybuf, d, buf.at[half], kk * tm + r, gsem.at[half]).start(priority=kk)

    def combine_tile(half):
        for _ in range(TOP_K):
            pltpu.make_async_copy(ybuf.at[pl.ds(0, tm * ROW_TILE)],
                                  buf.at[half, pl.ds(0, tm * ROW_TILE)], gsem.at[half]).wait()
        rows = slice(half * tm, (half + 1) * tm)
        w1 = route_ref[rows, 4:5]
        w2 = route_ref[rows, 5:6]
        for s in range(ROW_TILE):
            cols = slice(s * LANES, (s + 1) * LANES)
            out_ref[rows, cols] = x1_ref[rows, cols] + (_tile_piece(buf.at[half], s, tm) * w1
                                                        + _tile_piece(buf.at[half], s, tm, base=tm) * w2)

    @pl.when(i == 0)
    def _():
        _dest_copy(dest_hbm, dsm, dsem, i).start()
        _dest_copy(dest_hbm, dsm, dsem, i).wait()
        issue_gather(i, 0)

    @pl.when(i + 1 < n)
    def _():
        _dest_copy(dest_hbm, dsm, dsem, i + 1).start()

    issue_gather(i, 1)
    combine_tile(0)

    @pl.when(i + 1 < n)
    def _():
        _dest_copy(dest_hbm, dsm, dsem, i + 1).wait()
        issue_gather(i + 1, 0)

    combine_tile(1)


def _combine(dest3, x1, route, ybuf):
    T = x1.shape[0]
    n_steps = dest3.shape[0]
    tm2 = T // n_steps
    return pl.pallas_call(
        _combine_kernel,
        grid=(n_steps,),
        in_specs=[
            pl.BlockSpec(memory_space=pl.ANY),
            pl.BlockSpec((tm2, D_MODEL), lambda i: (i, 0)),
            pl.BlockSpec((tm2, LANES), lambda i: (i, 0)),
            pl.BlockSpec(memory_space=pl.ANY),
        ],
        out_specs=pl.BlockSpec((tm2, D_MODEL), lambda i: (i, 0)),
        out_shape=jax.ShapeDtypeStruct((T, D_MODEL), F32),
        scratch_shapes=[
            pltpu.SMEM((2, 1, dest3.shape[2]), jnp.int32),
            pltpu.SemaphoreType.DMA((2,)),
            pltpu.VMEM((2, TOP_K * (tm2 // 2) * ROW_TILE, LANES), F32),
            pltpu.SemaphoreType.DMA((2,)),
        ],
        compiler_params=_cparams(("arbitrary",)),
        name="combine",
    )(dest3, x1, route, ybuf)


def _layer(x2d, batch, seq, lambda_init, norm1_w, w_in, b_igate, b_fgate, conv_w, conv_b, m_norm_w,
           q_norm_w, k_norm_w, lq1, lk1, lq2, lk2, a_norm_w, w_out, norm2_w,
           w_group, b_group, w_expert, b_expert, w_gate, w_up, w_down):
    T = x2d.shape[0]
    row = lambda v: v.reshape(1, -1).astype(F32)
    g0 = 2 * MIX_HALF + 2 * MIX_HALF
    g1 = g0 + 2 * N_HEADS
    w_proj = jnp.concatenate(
        [w_in[:, :g0], w_in[:, g1:], w_in[:, g0:g1], jnp.zeros((D_MODEL, LANES - 2 * N_HEADS), F32)],
        axis=1).astype(BF16)
    gate_b = jnp.concatenate([b_igate, b_fgate, jnp.zeros((LANES - 2 * N_HEADS,), F32)]).reshape(1, LANES)

    proj, gates = _inproj(x2d, row(norm1_w), w_proj, conv_w, row(conv_b), seq)
    hm = _mlstm(proj, gates, gate_b, row(m_norm_w), batch, seq)
    qw = row(jnp.concatenate([q_norm_w, q_norm_w])) * (QK_DIM ** -0.5 * math.log2(math.e))
    kw = row(jnp.concatenate([k_norm_w, k_norm_w]))
    ha = _diff_attn(proj, qw, kw, row(lq1), row(lk1), row(lq2), row(lk2), row(a_norm_w),
                    batch, seq, lambda_init)

    w_router = jnp.concatenate(
        [w_expert, w_group, jnp.zeros((D_MODEL, LANES - N_EXPERTS - N_GROUPS), F32)], axis=1)
    wr_hi = w_router.astype(BF16)
    wr_lo = (w_router - wr_hi.astype(F32)).astype(BF16)
    b_router = jnp.concatenate(
        [b_expert, b_group, jnp.zeros((LANES - N_EXPERTS - N_GROUPS,), F32)]).reshape(1, LANES)
    x1, h2, route, counts = _outproj_router(hm, ha, x2d, w_out.astype(BF16), row(norm2_w),
                                            wr_hi, wr_lo, b_router)

    blk = MOE_BLOCK
    counts = counts[0, :N_EXPERTS].astype(jnp.int32)
    padded = ((counts + blk - 1) // blk) * blk
    pad_end = jnp.cumsum(padded)
    pad_start = pad_end - padded
    nb = (T * TOP_K) // blk + N_EXPERTS
    n_valid = (pad_end[-1] // blk).astype(jnp.int32)
    bstart = jnp.minimum(jnp.arange(nb, dtype=jnp.int32), n_valid - 1) * blk
    block_expert = jnp.minimum(
        jnp.sum(pad_end[None, :] <= bstart[:, None], axis=1), N_EXPERTS - 1).astype(jnp.int32)
    eid = route[:, 0:TOP_K].astype(jnp.int32)
    rank = route[:, TOP_K:2 * TOP_K].astype(jnp.int32)
    dest = pad_start[eid] + rank
    tm2 = min(2 * TM_ROWS, T)
    dest3 = dest.reshape(T // tm2, 1, tm2 * TOP_K)

    xbuf = _scatter_rows(pad_end.astype(jnp.int32), padded - counts, dest3, h2, nb * blk)
    ybuf = _experts(block_expert, n_valid.reshape(1), xbuf,
                    w_gate.astype(BF16), w_up.astype(BF16), w_down.astype(BF16))
    return _combine(dest3, x1, route, ybuf)


def kernel(x, norm1_w, w_in, b_igate, b_fgate, conv_w, conv_b, m_norm_w, q_norm_w, k_norm_w,
           lambda_q1, lambda_k1, lambda_q2, lambda_k2, a_norm_w, w_out, norm2_w,
           w_group, b_group, w_expert, b_expert, w_gate, w_up, w_down):
    batch, seq, d = x.shape
    x2d = x.reshape(batch * seq, d)
    for l in range(norm1_w.shape[0]):
        lambda_init = 0.8 - 0.6 * math.exp(-0.3 * l)
        x2d = _layer(x2d, batch, seq, lambda_init, norm1_w[l], w_in[l], b_igate[l], b_fgate[l],
                     conv_w[l], conv_b[l], m_norm_w[l], q_norm_w[l], k_norm_w[l],
                     lambda_q1[l], lambda_k1[l], lambda_q2[l], lambda_k2[l], a_norm_w[l],
                     w_out[l], norm2_w[l], w_group[l], b_group[l], w_expert[l], b_expert[l],
                     w_gate[l], w_up[l], w_down[l])
    return x2d.reshape(batch, seq, d)
```

```python
import functools
import math

import jax
import jax.numpy as jnp
from jax import lax
from jax.experimental import pallas as pl
from jax.experimental.pallas import tpu as pltpu

F32 = jnp.float32
BF16 = jnp.bfloat16
EPS = 1e-6

D_MODEL = 1024
HEAD_DIM = 128
N_HEADS = 4
MIX_HALF = N_HEADS * HEAD_DIM
QK_DIM = 64
CONV_WIDTH = 4
ATTN_CHUNK = 64
SUM_ROWS = 16
N_GROUPS = 4
EXPERTS_PER_GROUP = 8
N_EXPERTS = N_GROUPS * EXPERTS_PER_GROUP
TOP_K = 2
D_EXPERT = D_MODEL // 2

LANES = 128
PROJ_COLS = 7 * MIX_HALF
VMEM_LIMIT = 48 * 1024 * 1024

TM_PROJ = 512
MLSTM_CHUNK = 128
TQ_ATTN = 256
TK_ATTN = 256
TM_OUT = 512
ROUTE_ROWS = 8
TM_ROWS = 256
MOE_BLOCK = 512


def _cparams(sem):
    return pltpu.CompilerParams(dimension_semantics=sem, vmem_limit_bytes=VMEM_LIMIT)


CONV_HALO = 8


def _inproj_kernel(x_ref, nw_ref, w_ref, cw_ref, cb_ref, proj_ref, gate_ref, tail, *, tiles_per_seq):
    tm = x_ref.shape[0]

    @pl.when(lax.rem(pl.program_id(0), tiles_per_seq) == 0)
    def _():
        tail[...] = jnp.zeros_like(tail)

    x = x_ref[...]
    ms = jnp.mean(x * x, axis=-1, keepdims=True)
    h = (x * lax.rsqrt(ms + EPS) * nw_ref[...]).astype(BF16)
    nb = MIX_HALF
    for j in range(PROJ_COLS // nb):
        cols = slice(j * nb, (j + 1) * nb)
        pre = jnp.dot(h, w_ref[:, cols], preferred_element_type=F32)
        if j < 2:
            win = jnp.concatenate([tail[:, cols], pre], axis=0)
            acc = cb_ref[:, cols]
            for t in range(CONV_WIDTH):
                sh = CONV_WIDTH - 1 - t
                xs = win if sh == 0 else pltpu.roll(win, sh, axis=0)
                acc = acc + xs[CONV_HALO:, :] * cw_ref[t:t + 1, cols]
            tail[:, cols] = pre[tm - CONV_HALO:, :]
            pre = acc * jax.nn.sigmoid(acc)
            if j == 1:
                pre = pre * (HEAD_DIM ** -0.5)
        proj_ref[:, cols] = pre.astype(BF16)
    gate_ref[...] = jnp.dot(h, w_ref[:, PROJ_COLS:], preferred_element_type=F32)


def _inproj(x2d, nw, w, conv_w, conv_b, seq):
    T = x2d.shape[0]
    tm = min(TM_PROJ, seq)
    return pl.pallas_call(
        functools.partial(_inproj_kernel, tiles_per_seq=seq // tm),
        grid=(T // tm,),
        in_specs=[
            pl.BlockSpec((tm, D_MODEL), lambda i: (i, 0)),
            pl.BlockSpec((1, D_MODEL), lambda i: (0, 0)),
            pl.BlockSpec((D_MODEL, PROJ_COLS + LANES), lambda i: (0, 0)),
            pl.BlockSpec((CONV_WIDTH, 2 * MIX_HALF), lambda i: (0, 0)),
            pl.BlockSpec((1, 2 * MIX_HALF), lambda i: (0, 0)),
        ],
        out_specs=[
            pl.BlockSpec((tm, PROJ_COLS), lambda i: (i, 0)),
            pl.BlockSpec((tm, LANES), lambda i: (i, 0)),
        ],
        out_shape=[
            jax.ShapeDtypeStruct((T, PROJ_COLS), BF16),
            jax.ShapeDtypeStruct((T, LANES), F32),
        ],
        scratch_shapes=[pltpu.VMEM((CONV_HALO, 2 * MIX_HALF), F32)],
        compiler_params=_cparams(("arbitrary",)),
        name="inproj",
    )(x2d, nw, w, conv_w, conv_b)


def _log_sigmoid(x):
    return jnp.minimum(x, 0.0) - jnp.log(1.0 + jnp.exp(-jnp.abs(x)))


def _cumsum_lanes(x):
    n = x.shape[1]
    lane = lax.broadcasted_iota(jnp.int32, x.shape, 1)
    s = 1
    while s < n:
        x = x + jnp.where(lane >= s, pltpu.roll(x, s, axis=1), 0.0)
        s *= 2
    return x


def _mlstm_kernel(p_ref, g_ref, gb_ref, nw_ref, out_ref, gt_s, bt_s, bc_s, *c_s, seq, chunk):
    for c_h in c_s:
        c_h[...] = jnp.zeros_like(c_h)
    src_le_out = (lax.broadcasted_iota(jnp.int32, (chunk, chunk), 0)
                  <= lax.broadcasted_iota(jnp.int32, (chunk, chunk), 1))
    gate_rows = 2 * N_HEADS
    nt = (((1,), (1,)), ((), ()))
    split_row = lax.broadcasted_iota(jnp.int32, (16, HEAD_DIM), 0)

    n_chunks = seq // chunk
    for c in range(n_chunks):
        g = g_ref[c * chunk:(c + 1) * chunk, :] + gb_ref[...]
        gt_s[c * gate_rows:(c + 1) * gate_rows, :] = g.T[0:gate_rows, :]
    bt_s[...] = _cumsum_lanes(_log_sigmoid(gt_s[...]))
    for c in range(n_chunks):
        b_t = bt_s[c * gate_rows:(c + 1) * gate_rows, :]
        bc_s[c * chunk:(c + 1) * chunk, :] = pltpu.roll(
            jnp.concatenate([b_t, jnp.zeros((LANES - gate_rows, chunk), F32)], axis=0).T,
            LANES - N_HEADS, axis=1)

    def step(c, carry):
        n_all, m_row = carry
        r0 = pl.multiple_of(c * chunk, chunk)
        rows = pl.ds(r0, chunk)
        g = g_ref[rows, :] + gb_ref[...]
        b_t = bt_s[pl.ds(pl.multiple_of(c * gate_rows, gate_rows), gate_rows), :]
        b_c = bc_s[rows, :]
        a_all = g - b_c
        b_end = b_c[chunk - 1:chunk, :]
        log_w = b_end + a_all
        m_new = jnp.maximum(b_end + m_row, jnp.max(log_w, axis=0, keepdims=True))
        w_all = jnp.exp(log_w - m_new)
        decay_row = jnp.exp(b_end + m_row - m_new)

        n_new = []
        for h in range(N_HEADS):
            n = n_all[h]
            ct = c_s[h][...]
            q = p_ref[rows, h * HEAD_DIM:(h + 1) * HEAD_DIM]
            k = p_ref[rows, MIX_HALF + h * HEAD_DIM:MIX_HALF + (h + 1) * HEAD_DIM]
            v = p_ref[rows, 2 * MIX_HALF + h * HEAD_DIM:2 * MIX_HALF + (h + 1) * HEAD_DIM]
            o = p_ref[rows, 3 * MIX_HALF + h * HEAD_DIM:3 * MIX_HALF + (h + 1) * HEAD_DIM]
            b_row = b_t[N_HEADS + h:N_HEADS + h + 1, :]
            m = m_row[:, h:h + 1]
            decay = decay_row[:, h:h + 1]

            log_d = jnp.where(src_le_out, b_row + a_all[:, h:h + 1], -jnp.inf)
            log_inter = b_row + m
            m_t = jnp.maximum(log_inter, jnp.max(log_d, axis=0, keepdims=True))
            inter = jnp.exp(log_inter - m_t)
            st = lax.dot_general(k, q, nt, preferred_element_type=F32) * jnp.exp(log_d - m_t)
            vt = v.astype(F32).T.astype(BF16)
            num = (jnp.dot(vt, st.astype(BF16), preferred_element_type=F32)
                   + inter * lax.dot_general(ct.astype(BF16), q, nt, preferred_element_type=F32))
            n_hi = n.astype(BF16).astype(F32)
            n_mat = jnp.where(split_row == 0, n_hi, jnp.where(split_row == 1, n - n_hi, 0.0)).astype(BF16)
            nq = lax.dot_general(n_mat, q, nt, preferred_element_type=F32)
            den = jnp.sum(st, axis=0, keepdims=True) + inter * (nq[0:1, :] + nq[1:2, :])
            hh = (num / jnp.maximum(jnp.abs(den), jnp.exp(-m_t))).T

            kw = k.astype(F32) * w_all[:, h:h + 1]
            c_s[h][...] = decay * ct + jnp.dot(vt, kw.astype(BF16), preferred_element_type=F32)
            n_new.append(decay * n + jnp.sum(kw, axis=0, keepdims=True))

            hm = jax.nn.sigmoid(o.astype(F32)) * hh
            y = hm * lax.rsqrt(jnp.mean(hm * hm, axis=-1, keepdims=True) + EPS) * nw_ref[...]
            out_ref[rows, h * HEAD_DIM:(h + 1) * HEAD_DIM] = y.astype(BF16)
        return tuple(n_new), m_new

    init = (tuple(jnp.zeros((1, HEAD_DIM), F32) for _ in range(N_HEADS)), jnp.zeros((1, LANES), F32))
    lax.fori_loop(0, seq // chunk, step, init)


def _mlstm(proj, gates, gate_b, m_norm_w, batch, seq):
    T = batch * seq
    chunk = MLSTM_CHUNK
    assert chunk == LANES and seq % chunk == 0
    cst = lambda b: (0, 0)
    return pl.pallas_call(
        functools.partial(_mlstm_kernel, seq=seq, chunk=chunk),
        grid=(batch,),
        in_specs=[
            pl.BlockSpec((seq, 4 * MIX_HALF), lambda b: (b, 0)),
            pl.BlockSpec((seq, LANES), lambda b: (b, 0)),
            pl.BlockSpec((1, LANES), cst),
            pl.BlockSpec((1, HEAD_DIM), cst),
        ],
        out_specs=pl.BlockSpec((seq, MIX_HALF), lambda b: (b, 0)),
        out_shape=jax.ShapeDtypeStruct((T, MIX_HALF), BF16),
        scratch_shapes=[
            pltpu.VMEM((seq // chunk * 2 * N_HEADS, LANES), F32),
            pltpu.VMEM((seq // chunk * 2 * N_HEADS, LANES), F32),
            pltpu.VMEM((seq, LANES), F32),
        ] + [pltpu.VMEM((HEAD_DIM, HEAD_DIM), F32) for _ in range(N_HEADS)],
        compiler_params=_cparams(("arbitrary",)),
        name="mlstm",
    )(proj, gates, gate_b, m_norm_w)


def _half_norm(x, w):
    lo = lax.broadcasted_iota(jnp.int32, x.shape, 1) < QK_DIM
    x2 = x * x
    s_lo = jnp.sum(jnp.where(lo, x2, 0.0), axis=-1, keepdims=True)
    s_hi = jnp.sum(jnp.where(lo, 0.0, x2), axis=-1, keepdims=True)
    ms = jnp.where(lo, s_lo, s_hi) * (1.0 / QK_DIM)
    return x * lax.rsqrt(ms + EPS) * w


def _attn_kernel(q_ref, k_ref, v_ref, qw_ref, kw_ref, lq1_ref, lk1_ref, lq2_ref, lk2_ref, anw_ref,
                 out_ref, kn_s, vt_s, *, seq, tq, lambda_init):
    nq = seq // tq

    def prep(i, _):
        r0 = pl.multiple_of(i * tq, tq)
        kn_s[pl.ds(r0, tq), :] = _half_norm(k_ref[pl.ds(r0, tq), :].astype(F32), kw_ref[...]).astype(BF16)
        vt_s[i, 0:HEAD_DIM, :] = v_ref[pl.ds(r0, tq), :].astype(F32).T.astype(BF16)
        vt_s[i, HEAD_DIM:, :] = jnp.ones((SUM_ROWS, tq), BF16)
        return 0

    lax.fori_loop(0, nq, prep, 0)

    lam = (jnp.exp(jnp.sum(lq1_ref[...] * lk1_ref[...], axis=-1, keepdims=True))
           - jnp.exp(jnp.sum(lq2_ref[...] * lk2_ref[...], axis=-1, keepdims=True)) + lambda_init)
    first_map = lax.broadcasted_iota(jnp.int32, (HEAD_DIM, tq), 0) < QK_DIM
    q_in_tile = lax.rem(lax.broadcasted_iota(jnp.int32, (tq, 2 * tq), 1), tq)
    diag_ok = lax.broadcasted_iota(jnp.int32, (tq, 2 * tq), 0) // ATTN_CHUNK <= q_in_tile // ATTN_CHUNK

    def scores(qi):
        r0 = qi * tq
        qt = _half_norm(q_ref[r0:r0 + tq, :].astype(F32), qw_ref[...]).T
        qc = jnp.concatenate([jnp.where(first_map, qt, 0.0), jnp.where(first_map, 0.0, qt)],
                             axis=1).astype(BF16)
        s_d = jnp.where(diag_ok, jnp.dot(kn_s[r0:r0 + tq, :], qc, preferred_element_type=F32), -jnp.inf)
        m = jnp.max(s_d, axis=0, keepdims=True)
        s_f = None
        if qi > 0:
            s_f = jnp.dot(kn_s[0:r0, :], qc, preferred_element_type=F32)
            m = jnp.maximum(m, jnp.max(s_f, axis=0, keepdims=True))
        return s_d, s_f, m

    def weights(sc):
        s_d, s_f, m = sc
        p_d = jnp.exp2(s_d - m).astype(BF16)
        p_f = None if s_f is None else jnp.exp2(s_f - m).astype(BF16)
        return p_d, p_f

    def output(qi, pw):
        p_d, p_f = pw
        r0 = qi * tq
        acc = jnp.dot(vt_s[qi], p_d, preferred_element_type=F32)
        for j in range(qi):
            acc = acc + jnp.dot(vt_s[j], p_f[j * tq:(j + 1) * tq], preferred_element_type=F32)
        on = acc[0:HEAD_DIM] * (1.0 / acc[HEAD_DIM:HEAD_DIM + 1])
        o = (on[:, 0:tq] - lam * on[:, tq:]).T
        y = o * lax.rsqrt(jnp.mean(o * o, axis=-1, keepdims=True) + EPS) * anw_ref[...] * (1.0 - lambda_init)
        out_ref[r0:r0 + tq, :] = y.astype(BF16)

    sc = {0: scores(0)}
    pw = {}
    for step in range(nq + 2):
        if step + 1 < nq:
            sc[step + 1] = scores(step + 1)
        if 1 <= step <= nq:
            output(step - 1, pw.pop(step - 1))
        if step < nq:
            pw[step] = weights(sc.pop(step))


def _diff_attn(proj, qw, kw, lq1, lk1, lq2, lk2, a_norm_w, batch, seq, lambda_init):
    T = batch * seq
    tq = min(TQ_ATTN, seq)
    cst = lambda b, h: (0, 0)
    return pl.pallas_call(
        functools.partial(_attn_kernel, seq=seq, tq=tq, lambda_init=lambda_init),
        grid=(batch, N_HEADS),
        in_specs=[
            pl.BlockSpec((seq, HEAD_DIM), lambda b, h: (b, 4 * N_HEADS + h)),
            pl.BlockSpec((seq, HEAD_DIM), lambda b, h: (b, 5 * N_HEADS + h)),
            pl.BlockSpec((seq, HEAD_DIM), lambda b, h: (b, 6 * N_HEADS + h)),
            pl.BlockSpec((1, HEAD_DIM), cst),
            pl.BlockSpec((1, HEAD_DIM), cst),
            pl.BlockSpec((1, QK_DIM), cst),
            pl.BlockSpec((1, QK_DIM), cst),
            pl.BlockSpec((1, QK_DIM), cst),
            pl.BlockSpec((1, QK_DIM), cst),
            pl.BlockSpec((1, HEAD_DIM), cst),
        ],
        out_specs=pl.BlockSpec((seq, HEAD_DIM), lambda b, h: (b, h)),
        out_shape=jax.ShapeDtypeStruct((T, MIX_HALF), BF16),
        scratch_shapes=[
            pltpu.VMEM((seq, HEAD_DIM), BF16),
            pltpu.VMEM((seq // tq, HEAD_DIM + SUM_ROWS, tq), BF16),
        ],
        compiler_params=_cparams(("arbitrary", "arbitrary")),
        name="diffattn",
    )(proj, proj, proj, qw, kw, lq1, lk1, lq2, lk2, a_norm_w)


def _outproj_router_kernel(hm_ref, ha_ref, x_ref, wo_ref, nw_ref, wr_hi_ref, wr_lo_ref, br_ref,
                           x1_ref, h2_ref, route_ref, route_t_ref, counts_ref, carry):
    i = pl.program_id(0)
    tm = x_ref.shape[0]

    @pl.when(i == 0)
    def _():
        carry[...] = jnp.zeros_like(carry)

    mix = (jnp.dot(hm_ref[...], wo_ref[0:MIX_HALF, :], preferred_element_type=F32)
           + jnp.dot(ha_ref[...], wo_ref[MIX_HALF:, :], preferred_element_type=F32))
    x1 = x_ref[...] + mix
    x1_ref[...] = x1
    h2 = x1 * lax.rsqrt(jnp.mean(x1 * x1, axis=-1, keepdims=True) + EPS) * nw_ref[...]
    h2_ref[...] = h2

    h_hi = h2.astype(BF16)
    h_lo = (h2 - h_hi.astype(F32)).astype(BF16)
    logits = (jnp.dot(h_hi, wr_hi_ref[...], preferred_element_type=F32)
              + jnp.dot(h_lo, wr_hi_ref[...], preferred_element_type=F32)
              + jnp.dot(h_hi, wr_lo_ref[...], preferred_element_type=F32)) + br_ref[...]

    lane_i = lax.broadcasted_iota(jnp.int32, (tm, LANES), 1)
    lane = lane_i.astype(F32)
    big = float(LANES)
    gmask = (lane_i >= N_EXPERTS) & (lane_i < N_EXPERTS + N_GROUPS)
    gl = jnp.where(gmask, logits, -jnp.inf)
    gmax = jnp.max(gl, axis=-1, keepdims=True)
    g_sel = jnp.min(jnp.where(gl == gmax, lane, big), axis=-1, keepdims=True) - float(N_EXPERTS)
    g_w = 1.0 / jnp.sum(jnp.where(gmask, jnp.exp(logits - gmax), 0.0), axis=-1, keepdims=True)
    e_lo = g_sel * float(EXPERTS_PER_GROUP)
    emask = (lane >= e_lo) & (lane < e_lo + float(EXPERTS_PER_GROUP))
    el = jnp.where(emask, logits, -jnp.inf)
    v1 = jnp.max(el, axis=-1, keepdims=True)
    e1 = jnp.min(jnp.where(el == v1, lane, big), axis=-1, keepdims=True)
    el2 = jnp.where(lane == e1, -jnp.inf, el)
    v2 = jnp.max(el2, axis=-1, keepdims=True)
    e2 = jnp.min(jnp.where(el2 == v2, lane, big), axis=-1, keepdims=True)
    t2 = jnp.exp(v2 - v1)
    p1 = 1.0 / (1.0 + t2)
    p2 = t2 / (1.0 + t2)

    oh1 = (lane == e1).astype(F32)
    oh2 = (lane == e2).astype(F32)
    both = oh1 + oh2
    strict = (lax.broadcasted_iota(jnp.int32, (tm, tm), 0)
              > lax.broadcasted_iota(jnp.int32, (tm, tm), 1)).astype(BF16)
    prefix = jnp.dot(strict, both.astype(BF16), preferred_element_type=F32) + carry[...]
    r1 = jnp.sum(oh1 * prefix, axis=-1, keepdims=True)
    r2 = jnp.sum(oh2 * prefix, axis=-1, keepdims=True)
    carry[...] = carry[...] + jnp.sum(both, axis=0, keepdims=True)
    counts_ref[...] = carry[...]

    route = jnp.where(lane_i == 0, e1, 0.0)
    route = jnp.where(lane_i == 1, e2, route)
    route = jnp.where(lane_i == 2, r1, route)
    route = jnp.where(lane_i == 3, r2, route)
    route = jnp.where(lane_i == 4, g_w * p1, route)
    route = jnp.where(lane_i == 5, g_w * p2, route)
    route_ref[...] = route
    route_t_ref[...] = route.T[0:ROUTE_ROWS, :]


def _outproj_router(hm, ha, x2d, wo, nw, wr_hi, wr_lo, br):
    T = x2d.shape[0]
    tm = min(TM_OUT, T)
    row = lambda i: (i, 0)
    cst = lambda i: (0, 0)
    return pl.pallas_call(
        _outproj_router_kernel,
        grid=(T // tm,),
        in_specs=[
            pl.BlockSpec((tm, MIX_HALF), row),
            pl.BlockSpec((tm, MIX_HALF), row),
            pl.BlockSpec((tm, D_MODEL), row),
            pl.BlockSpec((D_MODEL, D_MODEL), cst),
            pl.BlockSpec((1, D_MODEL), cst),
            pl.BlockSpec((D_MODEL, LANES), cst),
            pl.BlockSpec((D_MODEL, LANES), cst),
            pl.BlockSpec((1, LANES), cst),
        ],
        out_specs=[
            pl.BlockSpec((tm, D_MODEL), row),
            pl.BlockSpec((tm, D_MODEL), row),
            pl.BlockSpec((tm, LANES), row),
            pl.BlockSpec((ROUTE_ROWS, tm), lambda i: (0, i)),
            pl.BlockSpec((1, LANES), cst),
        ],
        out_shape=[
            jax.ShapeDtypeStruct((T, D_MODEL), F32),
            jax.ShapeDtypeStruct((T, D_MODEL), F32),
            jax.ShapeDtypeStruct((T, LANES), F32),
            jax.ShapeDtypeStruct((ROUTE_ROWS, T), F32),
            jax.ShapeDtypeStruct((1, LANES), F32),
        ],
        scratch_shapes=[pltpu.VMEM((1, LANES), F32)],
        compiler_params=_cparams(("arbitrary",)),
        name="outproj_router",
    )(hm, ha, x2d, wo, nw, wr_hi, wr_lo, br)


ROW_TILE = D_MODEL // LANES


def _to_tiles(dst_ref, x, n):
    for s in range(ROW_TILE):
        dst_ref[pl.ds(s, n, stride=ROW_TILE), :] = x[:, s * LANES:(s + 1) * LANES]


def _tile_piece(src_ref, s, n, base=0):
    return src_ref[pl.ds(base * ROW_TILE + s, n, stride=ROW_TILE), :]


def _row_dma(src, src_row, dst, dst_row, sem):
    def tile(row):
        start = row * ROW_TILE
        return pl.ds(start if isinstance(row, int) else pl.multiple_of(start, ROW_TILE), ROW_TILE)
    return pltpu.make_async_copy(src.at[tile(src_row)], dst.at[tile(dst_row)], sem)


def _dest_copy(dest_hbm, dsm, dsem, step):
    slot = lax.rem(step, 2)
    return pltpu.make_async_copy(dest_hbm.at[step], dsm.at[slot], dsem.at[slot])


def _scatter_kernel(pend_ref, padded_ref, dest_hbm, h2_ref, xbuf, dsm, dsem, stage, ssem, zbuf, zsem):
    i = pl.program_id(0)
    n = pl.num_programs(0)
    tm = h2_ref.shape[0] // 2
    blk = MOE_BLOCK

    @pl.when(i == 0)
    def _():
        zbuf[...] = jnp.zeros_like(zbuf)

        def zero_copy(e):
            row0 = pl.multiple_of((pend_ref[e] - blk) * ROW_TILE, blk * ROW_TILE)
            return pltpu.make_async_copy(zbuf, xbuf.at[pl.ds(row0, blk * ROW_TILE)], zsem)

        for e in range(N_EXPERTS):
            pl.when(padded_ref[e] > 0)(lambda e=e: zero_copy(e).start())
        for e in range(N_EXPERTS):
            pl.when(padded_ref[e] > 0)(lambda e=e: zero_copy(e).wait())

        def zero_tail(bi, _):
            row0 = pl.multiple_of(bi * (blk * ROW_TILE), blk * ROW_TILE)
            cp = pltpu.make_async_copy(zbuf, xbuf.at[pl.ds(row0, blk * ROW_TILE)], zsem)
            cp.start()
            cp.wait()
            return 0

        lax.fori_loop(pend_ref[N_EXPERTS - 1] // blk, xbuf.shape[0] // (blk * ROW_TILE), zero_tail, 0)
        _dest_copy(dest_hbm, dsm, dsem, i).start()

    _dest_copy(dest_hbm, dsm, dsem, i).wait()

    @pl.when(i + 1 < n)
    def _():
        _dest_copy(dest_hbm, dsm, dsem, i + 1).start()

    slot = lax.rem(i, 2)

    def drain(half):
        for _ in range(TOP_K):
            pltpu.make_async_copy(stage.at[half], xbuf.at[pl.ds(0, tm * ROW_TILE)], ssem.at[half]).wait()

    for half in range(2):
        pl.when(i > 0)(functools.partial(drain, half))
        _to_tiles(stage.at[half], h2_ref[half * tm:(half + 1) * tm, :], tm)
        for r in range(tm):
            for kk in range(TOP_K):
                d = dsm[slot, 0, kk * 2 * tm + half * tm + r]
                _row_dma(stage.at[half], r, xbuf, d, ssem.at[half]).start(priority=kk)

    @pl.when(i == n - 1)
    def _():
        for half in range(2):
            drain(half)


def _scatter_rows(pad_end, padded, dest3, h2, n_pos):
    T = h2.shape[0]
    n_steps = dest3.shape[0]
    tm2 = T // n_steps
    return pl.pallas_call(
        _scatter_kernel,
        grid_spec=pltpu.PrefetchScalarGridSpec(
            num_scalar_prefetch=2,
            grid=(n_steps,),
            in_specs=[
                pl.BlockSpec(memory_space=pl.ANY),
                pl.BlockSpec((tm2, D_MODEL), lambda i, pe, pd: (i, 0)),
            ],
            out_specs=pl.BlockSpec(memory_space=pl.ANY),
            scratch_shapes=[
                pltpu.SMEM((2, 1, dest3.shape[2]), jnp.int32),
                pltpu.SemaphoreType.DMA((2,)),
                pltpu.VMEM((2, (tm2 // 2) * ROW_TILE, LANES), F32),
                pltpu.SemaphoreType.DMA((2,)),
                pltpu.VMEM((MOE_BLOCK * ROW_TILE, LANES), F32),
                pltpu.SemaphoreType.DMA,
            ],
        ),
        out_shape=jax.ShapeDtypeStruct((n_pos * ROW_TILE, LANES), F32),
        compiler_params=_cparams(("arbitrary",)),
        name="scatter_rows",
    )(pad_end, padded, dest3, h2)


def _expert_kernel(be_ref, nv_ref, x_ref, wg_ref, wu_ref, wd_ref, y_ref, wg_s, wu_s, wd_s):
    b = pl.program_id(0)
    blk = MOE_BLOCK

    @pl.when((b == 0) | (be_ref[b] != be_ref[jnp.maximum(b - 1, 0)]))
    def _():
        wg_s[...] = wg_ref[0].astype(BF16)
        wu_s[...] = wu_ref[0].astype(BF16)
        wd_s[...] = wd_ref[0].astype(BF16)

    @pl.when(b < nv_ref[0])
    def _():
        x = jnp.concatenate([_tile_piece(x_ref, s, blk) for s in range(ROW_TILE)], axis=1).astype(BF16)
        g = jnp.dot(x, wg_s[...], preferred_element_type=F32)
        u = jnp.dot(x, wu_s[...], preferred_element_type=F32)
        hid = (g * jax.nn.sigmoid(g) * u).astype(BF16)
        _to_tiles(y_ref, jnp.dot(hid, wd_s[...], preferred_element_type=F32), blk)

    @pl.when(b >= nv_ref[0])
    def _():
        y_ref[...] = jnp.zeros_like(y_ref)


def _experts(block_expert, n_valid, xbuf, wg, wu, wd):
    blk = MOE_BLOCK
    nb = xbuf.shape[0] // (blk * ROW_TILE)
    xmap = lambda b, be, nv: (jnp.minimum(b, nv[0] - 1), 0)
    wmap = lambda b, be, nv: (be[b], 0, 0)
    return pl.pallas_call(
        _expert_kernel,
        grid_spec=pltpu.PrefetchScalarGridSpec(
            num_scalar_prefetch=2,
            grid=(nb,),
            in_specs=[
                pl.BlockSpec((blk * ROW_TILE, LANES), xmap),
                pl.BlockSpec((1, D_MODEL, D_EXPERT), wmap),
                pl.BlockSpec((1, D_MODEL, D_EXPERT), wmap),
                pl.BlockSpec((1, D_EXPERT, D_MODEL), wmap),
            ],
            out_specs=pl.BlockSpec((blk * ROW_TILE, LANES), lambda b, be, nv: (b, 0)),
            scratch_shapes=[
                pltpu.VMEM((D_MODEL, D_EXPERT), BF16),
                pltpu.VMEM((D_MODEL, D_EXPERT), BF16),
                pltpu.VMEM((D_EXPERT, D_MODEL), BF16),
            ],
        ),
        out_shape=jax.ShapeDtypeStruct(xbuf.shape, F32),
        compiler_params=_cparams(("arbitrary",)),
        name="experts",
    )(block_expert, n_valid, xbuf, wg, wu, wd)


def _combine_kernel(dest_hbm, x1_ref, route_ref, ybuf, out_ref, dsm, dsem, buf, gsem):
    i = pl.program_id(0)
    n = pl.num_programs(0)
    tm = x1_ref.shape[0] // 2

    def issue_gather(step, half):
        slot = lax.rem(step, 2)
        for r in range(tm):
            for kk in range(TOP_K):
                d = dsm[slot, 0, kk * 2 * tm + half * tm + r]
                _row_dma(ybuf, d, buf.at[half], kk * tm + r, gsem.at[half]).start(priority=kk)

    def combine_tile(half):
        for _ in range(TOP_K):
            pltpu.make_async_copy(ybuf.at[pl.ds(0, tm * ROW_TILE)],
                                  buf.at[half, pl.ds(0, tm * ROW_TILE)], gsem.at[half]).wait()
        rows = slice(half * tm, (half + 1) * tm)
        w1 = route_ref[rows, 4:5]
        w2 = route_ref[rows, 5:6]
        for s in range(ROW_TILE):
            cols = slice(s * LANES, (s + 1) * LANES)
            out_ref[rows, cols] = x1_ref[rows, cols] + (_tile_piece(buf.at[half], s, tm) * w1
                                                        + _tile_piece(buf.at[half], s, tm, base=tm) * w2)

    @pl.when(i == 0)
    def _():
        _dest_copy(dest_hbm, dsm, dsem, i).start()
        _dest_copy(dest_hbm, dsm, dsem, i).wait()
        issue_gather(i, 0)

    @pl.when(i + 1 < n)
    def _():
        _dest_copy(dest_hbm, dsm, dsem, i + 1).start()

    issue_gather(i, 1)
    combine_tile(0)

    @pl.when(i + 1 < n)
    def _():
        _dest_copy(dest_hbm, dsm, dsem, i + 1).wait()
        issue_gather(i + 1, 0)

    combine_tile(1)


def _combine(dest3, x1, route, ybuf):
    T = x1.shape[0]
    n_steps = dest3.shape[0]
    tm2 = T // n_steps
    return pl.pallas_call(
        _combine_kernel,
        grid=(n_steps,),
        in_specs=[
            pl.BlockSpec(memory_space=pl.ANY),
            pl.BlockSpec((tm2, D_MODEL), lambda i: (i, 0)),
            pl.BlockSpec((tm2, LANES), lambda i: (i, 0)),
            pl.BlockSpec(memory_space=pl.ANY),
        ],
        out_specs=pl.BlockSpec((tm2, D_MODEL), lambda i: (i, 0)),
        out_shape=jax.ShapeDtypeStruct((T, D_MODEL), F32),
        scratch_shapes=[
            pltpu.SMEM((2, 1, dest3.shape[2]), jnp.int32),
            pltpu.SemaphoreType.DMA((2,)),
            pltpu.VMEM((2, TOP_K * (tm2 // 2) * ROW_TILE, LANES), F32),
            pltpu.SemaphoreType.DMA((2,)),
        ],
        compiler_params=_cparams(("arbitrary",)),
        name="combine",
    )(dest3, x1, route, ybuf)


def _layer(x2d, batch, seq, lambda_init, norm1_w, w_in, b_igate, b_fgate, conv_w, conv_b, m_norm_w,
           q_norm_w, k_norm_w, lq1, lk1, lq2, lk2, a_norm_w, w_out, norm2_w,
           w_group, b_group, w_expert, b_expert, w_gate, w_up, w_down):
    T = x2d.shape[0]
    row = lambda v: v.reshape(1, -1).astype(F32)
    g0 = 2 * MIX_HALF + 2 * MIX_HALF
    g1 = g0 + 2 * N_HEADS
    w_proj = jnp.concatenate(
        [w_in[:, :g0], w_in[:, g1:], w_in[:, g0:g1], jnp.zeros((D_MODEL, LANES - 2 * N_HEADS), F32)],
        axis=1).astype(BF16)
    gate_b = jnp.concatenate([b_igate, b_fgate, jnp.zeros((LANES - 2 * N_HEADS,), F32)]).reshape(1, LANES)

    proj, gates = _inproj(x2d, row(norm1_w), w_proj, conv_w, row(conv_b), seq)
    hm = _mlstm(proj, gates, gate_b, row(m_norm_w), batch, seq)
    qw = row(jnp.concatenate([q_norm_w, q_norm_w])) * (QK_DIM ** -0.5 * math.log2(math.e))
    kw = row(jnp.concatenate([k_norm_w, k_norm_w]))
    ha = _diff_attn(proj, qw, kw, row(lq1), row(lk1), row(lq2), row(lk2), row(a_norm_w),
                    batch, seq, lambda_init)

    w_router = jnp.concatenate(
        [w_expert, w_group, jnp.zeros((D_MODEL, LANES - N_EXPERTS - N_GROUPS), F32)], axis=1)
    wr_hi = w_router.astype(BF16)
    wr_lo = (w_router - wr_hi.astype(F32)).astype(BF16)
    b_router = jnp.concatenate(
        [b_expert, b_group, jnp.zeros((LANES - N_EXPERTS - N_GROUPS,), F32)]).reshape(1, LANES)
    x1, h2, route, route_t, counts = _outproj_router(hm, ha, x2d, w_out.astype(BF16), row(norm2_w),
                                                     wr_hi, wr_lo, b_router)

    blk = MOE_BLOCK
    counts = counts[0, :N_EXPERTS].astype(jnp.int32)
    padded = ((counts + blk - 1) // blk) * blk
    pad_end = jnp.cumsum(padded)
    pad_start = pad_end - padded
    nb = (T * TOP_K) // blk + N_EXPERTS
    n_valid = (pad_end[-1] // blk).astype(jnp.int32)
    bstart = jnp.minimum(jnp.arange(nb, dtype=jnp.int32), n_valid - 1) * blk
    block_expert = jnp.minimum(
        jnp.sum(pad_end[None, :] <= bstart[:, None], axis=1), N_EXPERTS - 1).astype(jnp.int32)
    eid = route_t[0:TOP_K].astype(jnp.int32)
    rank = route_t[TOP_K:2 * TOP_K].astype(jnp.int32)
    dest = pad_start[eid] + rank
    tm2 = min(2 * TM_ROWS, T)
    dest3 = dest.reshape(TOP_K, T // tm2, tm2).transpose(1, 0, 2).reshape(T // tm2, 1, TOP_K * tm2)

    xbuf = _scatter_rows(pad_end.astype(jnp.int32), padded - counts, dest3, h2, nb * blk)
    ybuf = _experts(block_expert, n_valid.reshape(1), xbuf, w_gate, w_up, w_down)
    return _combine(dest3, x1, route, ybuf)


def kernel(x, norm1_w, w_in, b_igate, b_fgate, conv_w, conv_b, m_norm_w, q_norm_w, k_norm_w,
           lambda_q1, lambda_k1, lambda_q2, lambda_k2, a_norm_w, w_out, norm2_w,
           w_group, b_group, w_expert, b_expert, w_gate, w_up, w_down):
    batch, seq, d = x.shape
    x2d = x.reshape(batch * seq, d)
    for l in range(norm1_w.shape[0]):
        lambda_init = 0.8 - 0.6 * math.exp(-0.3 * l)
        x2d = _layer(x2d, batch, seq, lambda_init, norm1_w[l], w_in[l], b_igate[l], b_fgate[l],
                     conv_w[l], conv_b[l], m_norm_w[l], q_norm_w[l], k_norm_w[l],
                     lambda_q1[l], lambda_k1[l], lambda_q2[l], lambda_k2[l], a_norm_w[l],
                     w_out[l], norm2_w[l], w_group[l], b_group[l], w_expert[l], b_expert[l],
                     w_gate[l], w_up[l], w_down[l])
    return x2d.reshape(batch, seq, d)
```

```python
import functools
import math

import jax
import jax.numpy as jnp
from jax import lax
from jax.experimental import pallas as pl
from jax.experimental.pallas import tpu as pltpu

F32 = jnp.float32
BF16 = jnp.bfloat16
EPS = 1e-6

D_MODEL = 1024
HEAD_DIM = 128
N_HEADS = 4
MIX_HALF = N_HEADS * HEAD_DIM
QK_DIM = 64
CONV_WIDTH = 4
ATTN_CHUNK = 64
SUM_ROWS = 16
N_GROUPS = 4
EXPERTS_PER_GROUP = 8
N_EXPERTS = N_GROUPS * EXPERTS_PER_GROUP
TOP_K = 2
D_EXPERT = D_MODEL // 2

LANES = 128
PROJ_COLS = 7 * MIX_HALF
VMEM_LIMIT = 48 * 1024 * 1024

TM_PROJ = 512
MLSTM_CHUNK = 128
TQ_ATTN = 256
TK_ATTN = 256
TM_OUT = 512
ROUTE_ROWS = 8
TM_ROWS = 256
MOE_BLOCK = 512


def _cparams(sem):
    return pltpu.CompilerParams(dimension_semantics=sem, vmem_limit_bytes=VMEM_LIMIT)


CONV_HALO = 8


def _inproj_kernel(x_ref, nw_ref, w_ref, cw_ref, cb_ref, proj_ref, gate_ref, tail, *, tiles_per_seq):
    tm = x_ref.shape[0]

    @pl.when(lax.rem(pl.program_id(0), tiles_per_seq) == 0)
    def _():
        tail[...] = jnp.zeros_like(tail)

    x = x_ref[...]
    ms = jnp.mean(x * x, axis=-1, keepdims=True)
    h = (x * lax.rsqrt(ms + EPS) * nw_ref[...]).astype(BF16)
    nb = MIX_HALF
    for j in range(PROJ_COLS // nb):
        cols = slice(j * nb, (j + 1) * nb)
        pre = jnp.dot(h, w_ref[:, cols], preferred_element_type=F32)
        if j < 2:
            win = jnp.concatenate([tail[:, cols], pre], axis=0)
            acc = cb_ref[:, cols]
            for t in range(CONV_WIDTH):
                sh = CONV_WIDTH - 1 - t
                xs = win if sh == 0 else pltpu.roll(win, sh, axis=0)
                acc = acc + xs[CONV_HALO:, :] * cw_ref[t:t + 1, cols]
            tail[:, cols] = pre[tm - CONV_HALO:, :]
            pre = acc * jax.nn.sigmoid(acc)
            if j == 1:
                pre = pre * (HEAD_DIM ** -0.5)
        proj_ref[:, cols] = pre.astype(BF16)
    gate_ref[...] = jnp.dot(h, w_ref[:, PROJ_COLS:], preferred_element_type=F32)


def _inproj(x2d, nw, w, conv_w, conv_b, seq):
    T = x2d.shape[0]
    tm = min(TM_PROJ, seq)
    return pl.pallas_call(
        functools.partial(_inproj_kernel, tiles_per_seq=seq // tm),
        grid=(T // tm,),
        in_specs=[
            pl.BlockSpec((tm, D_MODEL), lambda i: (i, 0)),
            pl.BlockSpec((1, D_MODEL), lambda i: (0, 0)),
            pl.BlockSpec((D_MODEL, PROJ_COLS + LANES), lambda i: (0, 0)),
            pl.BlockSpec((CONV_WIDTH, 2 * MIX_HALF), lambda i: (0, 0)),
            pl.BlockSpec((1, 2 * MIX_HALF), lambda i: (0, 0)),
        ],
        out_specs=[
            pl.BlockSpec((tm, PROJ_COLS), lambda i: (i, 0)),
            pl.BlockSpec((tm, LANES), lambda i: (i, 0)),
        ],
        out_shape=[
            jax.ShapeDtypeStruct((T, PROJ_COLS), BF16),
            jax.ShapeDtypeStruct((T, LANES), F32),
        ],
        scratch_shapes=[pltpu.VMEM((CONV_HALO, 2 * MIX_HALF), F32)],
        compiler_params=_cparams(("arbitrary",)),
        name="inproj",
    )(x2d, nw, w, conv_w, conv_b)


def _log_sigmoid(x):
    return jnp.minimum(x, 0.0) - jnp.log(1.0 + jnp.exp(-jnp.abs(x)))


def _cumsum_lanes(x):
    n = x.shape[1]
    lane = lax.broadcasted_iota(jnp.int32, x.shape, 1)
    s = 1
    while s < n:
        x = x + jnp.where(lane >= s, pltpu.roll(x, s, axis=1), 0.0)
        s *= 2
    return x


def _mlstm_kernel(p_ref, g_ref, gb_ref, nw_ref, out_ref, gt_s, bt_s, bc_s, *c_s, seq, chunk):
    for c_h in c_s:
        c_h[...] = jnp.zeros_like(c_h)
    src_le_out = (lax.broadcasted_iota(jnp.int32, (chunk, chunk), 0)
                  <= lax.broadcasted_iota(jnp.int32, (chunk, chunk), 1))
    gate_rows = 2 * N_HEADS
    nt = (((1,), (1,)), ((), ()))
    split_row = lax.broadcasted_iota(jnp.int32, (16, HEAD_DIM), 0)

    n_chunks = seq // chunk
    for c in range(n_chunks):
        g = g_ref[c * chunk:(c + 1) * chunk, :] + gb_ref[...]
        gt_s[c * gate_rows:(c + 1) * gate_rows, :] = g.T[0:gate_rows, :]
    bt_s[...] = _cumsum_lanes(_log_sigmoid(gt_s[...]))
    for c in range(n_chunks):
        b_t = bt_s[c * gate_rows:(c + 1) * gate_rows, :]
        bc_s[c * chunk:(c + 1) * chunk, :] = pltpu.roll(
            jnp.concatenate([b_t, jnp.zeros((LANES - gate_rows, chunk), F32)], axis=0).T,
            LANES - N_HEADS, axis=1)

    def step(c, carry):
        n_all, m_row = carry
        r0 = pl.multiple_of(c * chunk, chunk)
        rows = pl.ds(r0, chunk)
        g = g_ref[rows, :] + gb_ref[...]
        b_t = bt_s[pl.ds(pl.multiple_of(c * gate_rows, gate_rows), gate_rows), :]
        b_c = bc_s[rows, :]
        a_all = g - b_c
        b_end = b_c[chunk - 1:chunk, :]
        log_w = b_end + a_all
        m_new = jnp.maximum(b_end + m_row, jnp.max(log_w, axis=0, keepdims=True))
        w_all = jnp.exp(log_w - m_new)
        decay_row = jnp.exp(b_end + m_row - m_new)

        heads = range(N_HEADS)
        col = lambda blk, h: slice(blk * MIX_HALF + h * HEAD_DIM, blk * MIX_HALF + (h + 1) * HEAD_DIM)
        q = [p_ref[rows, col(0, h)] for h in heads]
        k = [p_ref[rows, col(1, h)] for h in heads]
        ct = [c_s[h][...] for h in heads]
        vt = [p_ref[rows, col(2, h)].astype(F32).T.astype(BF16) for h in heads]
        b_row = [b_t[N_HEADS + h:N_HEADS + h + 1, :] for h in heads]
        log_d = [jnp.where(src_le_out, b_row[h] + a_all[:, h:h + 1], -jnp.inf) for h in heads]
        log_inter = [b_row[h] + m_row[:, h:h + 1] for h in heads]
        m_t = [jnp.maximum(log_inter[h], jnp.max(log_d[h], axis=0, keepdims=True)) for h in heads]
        inter = [jnp.exp(log_inter[h] - m_t[h]) for h in heads]
        st = [lax.dot_general(k[h], q[h], nt, preferred_element_type=F32) * jnp.exp(log_d[h] - m_t[h])
              for h in heads]
        cq = [lax.dot_general(ct[h].astype(BF16), q[h], nt, preferred_element_type=F32) for h in heads]
        nq = []
        for h in heads:
            n_hi = n_all[h].astype(BF16).astype(F32)
            n_mat = jnp.where(split_row == 0, n_hi,
                              jnp.where(split_row == 1, n_all[h] - n_hi, 0.0)).astype(BF16)
            nq.append(lax.dot_general(n_mat, q[h], nt, preferred_element_type=F32))
        num = [jnp.dot(vt[h], st[h].astype(BF16), preferred_element_type=F32) + inter[h] * cq[h]
               for h in heads]
        den = [jnp.sum(st[h], axis=0, keepdims=True) + inter[h] * (nq[h][0:1, :] + nq[h][1:2, :])
               for h in heads]
        hh = [(num[h] / jnp.maximum(jnp.abs(den[h]), jnp.exp(-m_t[h]))).T for h in heads]

        kw = [k[h].astype(F32) * w_all[:, h:h + 1] for h in heads]
        n_new = []
        for h in heads:
            decay = decay_row[:, h:h + 1]
            c_s[h][...] = decay * ct[h] + jnp.dot(vt[h], kw[h].astype(BF16), preferred_element_type=F32)
            n_new.append(decay * n_all[h] + jnp.sum(kw[h], axis=0, keepdims=True))
        for h in heads:
            hm = jax.nn.sigmoid(p_ref[rows, col(3, h)].astype(F32)) * hh[h]
            y = hm * lax.rsqrt(jnp.mean(hm * hm, axis=-1, keepdims=True) + EPS) * nw_ref[...]
            out_ref[rows, col(0, h)] = y.astype(BF16)
        return tuple(n_new), m_new

    init = (tuple(jnp.zeros((1, HEAD_DIM), F32) for _ in range(N_HEADS)), jnp.zeros((1, LANES), F32))
    lax.fori_loop(0, seq // chunk, step, init)


def _mlstm(proj, gates, gate_b, m_norm_w, batch, seq):
    T = batch * seq
    chunk = MLSTM_CHUNK
    assert chunk == LANES and seq % chunk == 0
    cst = lambda b: (0, 0)
    return pl.pallas_call(
        functools.partial(_mlstm_kernel, seq=seq, chunk=chunk),
        grid=(batch,),
        in_specs=[
            pl.BlockSpec((seq, 4 * MIX_HALF), lambda b: (b, 0)),
            pl.BlockSpec((seq, LANES), lambda b: (b, 0)),
            pl.BlockSpec((1, LANES), cst),
            pl.BlockSpec((1, HEAD_DIM), cst),
        ],
        out_specs=pl.BlockSpec((seq, MIX_HALF), lambda b: (b, 0)),
        out_shape=jax.ShapeDtypeStruct((T, MIX_HALF), BF16),
        scratch_shapes=[
            pltpu.VMEM((seq // chunk * 2 * N_HEADS, LANES), F32),
            pltpu.VMEM((seq // chunk * 2 * N_HEADS, LANES), F32),
            pltpu.VMEM((seq, LANES), F32),
        ] + [pltpu.VMEM((HEAD_DIM, HEAD_DIM), F32) for _ in range(N_HEADS)],
        compiler_params=_cparams(("arbitrary",)),
        name="mlstm",
    )(proj, gates, gate_b, m_norm_w)


def _half_norm(x, w):
    lo = lax.broadcasted_iota(jnp.int32, x.shape, 1) < QK_DIM
    x2 = x * x
    s_lo = jnp.sum(jnp.where(lo, x2, 0.0), axis=-1, keepdims=True)
    s_hi = jnp.sum(jnp.where(lo, 0.0, x2), axis=-1, keepdims=True)
    ms = jnp.where(lo, s_lo, s_hi) * (1.0 / QK_DIM)
    return x * lax.rsqrt(ms + EPS) * w


def _attn_kernel(q_ref, k_ref, v_ref, qw_ref, kw_ref, lq1_ref, lk1_ref, lq2_ref, lk2_ref, anw_ref,
                 out_ref, kn_s, vt_s, *, seq, tq, lambda_init):
    nq = seq // tq

    def prep(i, _):
        r0 = pl.multiple_of(i * tq, tq)
        kn_s[pl.ds(r0, tq), :] = _half_norm(k_ref[pl.ds(r0, tq), :].astype(F32), kw_ref[...]).astype(BF16)
        vt_s[i, 0:HEAD_DIM, :] = v_ref[pl.ds(r0, tq), :].astype(F32).T.astype(BF16)
        vt_s[i, HEAD_DIM:, :] = jnp.ones((SUM_ROWS, tq), BF16)
        return 0

    lax.fori_loop(0, nq, prep, 0)

    lam = (jnp.exp(jnp.sum(lq1_ref[...] * lk1_ref[...], axis=-1, keepdims=True))
           - jnp.exp(jnp.sum(lq2_ref[...] * lk2_ref[...], axis=-1, keepdims=True)) + lambda_init)
    first_map = lax.broadcasted_iota(jnp.int32, (HEAD_DIM, tq), 0) < QK_DIM
    q_in_tile = lax.rem(lax.broadcasted_iota(jnp.int32, (tq, 2 * tq), 1), tq)
    diag_ok = lax.broadcasted_iota(jnp.int32, (tq, 2 * tq), 0) // ATTN_CHUNK <= q_in_tile // ATTN_CHUNK

    def scores(qi):
        r0 = qi * tq
        qt = _half_norm(q_ref[r0:r0 + tq, :].astype(F32), qw_ref[...]).T
        qc = jnp.concatenate([jnp.where(first_map, qt, 0.0), jnp.where(first_map, 0.0, qt)],
                             axis=1).astype(BF16)
        s_d = jnp.where(diag_ok, jnp.dot(kn_s[r0:r0 + tq, :], qc, preferred_element_type=F32), -jnp.inf)
        m = jnp.max(s_d, axis=0, keepdims=True)
        s_f = None
        if qi > 0:
            s_f = jnp.dot(kn_s[0:r0, :], qc, preferred_element_type=F32)
            m = jnp.maximum(m, jnp.max(s_f, axis=0, keepdims=True))
        return s_d, s_f, m

    def weights(sc):
        s_d, s_f, m = sc
        p_d = jnp.exp2(s_d - m).astype(BF16)
        p_f = None if s_f is None else jnp.exp2(s_f - m).astype(BF16)
        return p_d, p_f

    def output(qi, pw):
        p_d, p_f = pw
        r0 = qi * tq
        acc = jnp.dot(vt_s[qi], p_d, preferred_element_type=F32)
        for j in range(qi):
            acc = acc + jnp.dot(vt_s[j], p_f[j * tq:(j + 1) * tq], preferred_element_type=F32)
        on = acc[0:HEAD_DIM] * (1.0 / acc[HEAD_DIM:HEAD_DIM + 1])
        o = (on[:, 0:tq] - lam * on[:, tq:]).T
        y = o * lax.rsqrt(jnp.mean(o * o, axis=-1, keepdims=True) + EPS) * anw_ref[...] * (1.0 - lambda_init)
        out_ref[r0:r0 + tq, :] = y.astype(BF16)

    sc = {0: scores(0)}
    pw = {}
    for step in range(nq + 2):
        if step + 1 < nq:
            sc[step + 1] = scores(step + 1)
        if 1 <= step <= nq:
            output(step - 1, pw.pop(step - 1))
        if step < nq:
            pw[step] = weights(sc.pop(step))


def _diff_attn(proj, qw, kw, lq1, lk1, lq2, lk2, a_norm_w, batch, seq, lambda_init):
    T = batch * seq
    tq = min(TQ_ATTN, seq)
    cst = lambda b, h: (0, 0)
    return pl.pallas_call(
        functools.partial(_attn_kernel, seq=seq, tq=tq, lambda_init=lambda_init),
        grid=(batch, N_HEADS),
        in_specs=[
            pl.BlockSpec((seq, HEAD_DIM), lambda b, h: (b, 4 * N_HEADS + h)),
            pl.BlockSpec((seq, HEAD_DIM), lambda b, h: (b, 5 * N_HEADS + h)),
            pl.BlockSpec((seq, HEAD_DIM), lambda b, h: (b, 6 * N_HEADS + h)),
            pl.BlockSpec((1, HEAD_DIM), cst),
            pl.BlockSpec((1, HEAD_DIM), cst),
            pl.BlockSpec((1, QK_DIM), cst),
            pl.BlockSpec((1, QK_DIM), cst),
            pl.BlockSpec((1, QK_DIM), cst),
            pl.BlockSpec((1, QK_DIM), cst),
            pl.BlockSpec((1, HEAD_DIM), cst),
        ],
        out_specs=pl.BlockSpec((seq, HEAD_DIM), lambda b, h: (b, h)),
        out_shape=jax.ShapeDtypeStruct((T, MIX_HALF), BF16),
        scratch_shapes=[
            pltpu.VMEM((seq, HEAD_DIM), BF16),
            pltpu.VMEM((seq // tq, HEAD_DIM + SUM_ROWS, tq), BF16),
        ],
        compiler_params=_cparams(("arbitrary", "arbitrary")),
        name="diffattn",
    )(proj, proj, proj, qw, kw, lq1, lk1, lq2, lk2, a_norm_w)


def _outproj_router_kernel(hm_ref, ha_ref, x_ref, wo_ref, nw_ref, wr_hi_ref, wr_lo_ref, br_ref,
                           x1_ref, h2_ref, route_ref, route_t_ref, counts_ref, carry):
    i = pl.program_id(0)
    tm = x_ref.shape[0]

    @pl.when(i == 0)
    def _():
        carry[...] = jnp.zeros_like(carry)

    mix = (jnp.dot(hm_ref[...], wo_ref[0:MIX_HALF, :], preferred_element_type=F32)
           + jnp.dot(ha_ref[...], wo_ref[MIX_HALF:, :], preferred_element_type=F32))
    x1 = x_ref[...] + mix
    x1_ref[...] = x1
    h2 = x1 * lax.rsqrt(jnp.mean(x1 * x1, axis=-1, keepdims=True) + EPS) * nw_ref[...]
    h2_ref[...] = h2

    h_hi = h2.astype(BF16)
    h_lo = (h2 - h_hi.astype(F32)).astype(BF16)
    logits = (jnp.dot(h_hi, wr_hi_ref[...], preferred_element_type=F32)
              + jnp.dot(h_lo, wr_hi_ref[...], preferred_element_type=F32)
              + jnp.dot(h_hi, wr_lo_ref[...], preferred_element_type=F32)) + br_ref[...]

    lane_i = lax.broadcasted_iota(jnp.int32, (tm, LANES), 1)
    lane = lane_i.astype(F32)
    big = float(LANES)
    gmask = (lane_i >= N_EXPERTS) & (lane_i < N_EXPERTS + N_GROUPS)
    gl = jnp.where(gmask, logits, -jnp.inf)
    gmax = jnp.max(gl, axis=-1, keepdims=True)
    g_sel = jnp.min(jnp.where(gl == gmax, lane, big), axis=-1, keepdims=True) - float(N_EXPERTS)
    g_w = 1.0 / jnp.sum(jnp.where(gmask, jnp.exp(logits - gmax), 0.0), axis=-1, keepdims=True)
    e_lo = g_sel * float(EXPERTS_PER_GROUP)
    emask = (lane >= e_lo) & (lane < e_lo + float(EXPERTS_PER_GROUP))
    el = jnp.where(emask, logits, -jnp.inf)
    v1 = jnp.max(el, axis=-1, keepdims=True)
    e1 = jnp.min(jnp.where(el == v1, lane, big), axis=-1, keepdims=True)
    el2 = jnp.where(lane == e1, -jnp.inf, el)
    v2 = jnp.max(el2, axis=-1, keepdims=True)
    e2 = jnp.min(jnp.where(el2 == v2, lane, big), axis=-1, keepdims=True)
    t2 = jnp.exp(v2 - v1)
    p1 = 1.0 / (1.0 + t2)
    p2 = t2 / (1.0 + t2)

    oh1 = (lane == e1).astype(F32)
    oh2 = (lane == e2).astype(F32)
    both = oh1 + oh2
    strict = (lax.broadcasted_iota(jnp.int32, (tm, tm), 0)
              > lax.broadcasted_iota(jnp.int32, (tm, tm), 1)).astype(BF16)
    prefix = jnp.dot(strict, both.astype(BF16), preferred_element_type=F32) + carry[...]
    r1 = jnp.sum(oh1 * prefix, axis=-1, keepdims=True)
    r2 = jnp.sum(oh2 * prefix, axis=-1, keepdims=True)
    carry[...] = carry[...] + jnp.sum(both, axis=0, keepdims=True)
    counts_ref[...] = carry[...]

    route = jnp.where(lane_i == 0, e1, 0.0)
    route = jnp.where(lane_i == 1, e2, route)
    route = jnp.where(lane_i == 2, r1, route)
    route = jnp.where(lane_i == 3, r2, route)
    route = jnp.where(lane_i == 4, g_w * p1, route)
    route = jnp.where(lane_i == 5, g_w * p2, route)
    route_ref[...] = route
    route_t_ref[...] = route.T[0:ROUTE_ROWS, :]


def _outproj_router(hm, ha, x2d, wo, nw, wr_hi, wr_lo, br):
    T = x2d.shape[0]
    tm = min(TM_OUT, T)
    row = lambda i: (i, 0)
    cst = lambda i: (0, 0)
    return pl.pallas_call(
        _outproj_router_kernel,
        grid=(T // tm,),
        in_specs=[
            pl.BlockSpec((tm, MIX_HALF), row),
            pl.BlockSpec((tm, MIX_HALF), row),
            pl.BlockSpec((tm, D_MODEL), row),
            pl.BlockSpec((D_MODEL, D_MODEL), cst),
            pl.BlockSpec((1, D_MODEL), cst),
            pl.BlockSpec((D_MODEL, LANES), cst),
            pl.BlockSpec((D_MODEL, LANES), cst),
            pl.BlockSpec((1, LANES), cst),
        ],
        out_specs=[
            pl.BlockSpec((tm, D_MODEL), row),
            pl.BlockSpec((tm, D_MODEL), row),
            pl.BlockSpec((tm, LANES), row),
            pl.BlockSpec((ROUTE_ROWS, tm), lambda i: (0, i)),
            pl.BlockSpec((1, LANES), cst),
        ],
        out_shape=[
            jax.ShapeDtypeStruct((T, D_MODEL), F32),
            jax.ShapeDtypeStruct((T, D_MODEL), F32),
            jax.ShapeDtypeStruct((T, LANES), F32),
            jax.ShapeDtypeStruct((ROUTE_ROWS, T), F32),
            jax.ShapeDtypeStruct((1, LANES), F32),
        ],
        scratch_shapes=[pltpu.VMEM((1, LANES), F32)],
        compiler_params=_cparams(("arbitrary",)),
        name="outproj_router",
    )(hm, ha, x2d, wo, nw, wr_hi, wr_lo, br)


ROW_TILE = D_MODEL // LANES


def _to_tiles(dst_ref, x, n):
    for s in range(ROW_TILE):
        dst_ref[pl.ds(s, n, stride=ROW_TILE), :] = x[:, s * LANES:(s + 1) * LANES]


def _tile_piece(src_ref, s, n, base=0):
    return src_ref[pl.ds(base * ROW_TILE + s, n, stride=ROW_TILE), :]


def _row_dma(src, src_row, dst, dst_row, sem):
    def tile(row):
        start = row * ROW_TILE
        return pl.ds(start if isinstance(row, int) else pl.multiple_of(start, ROW_TILE), ROW_TILE)
    return pltpu.make_async_copy(src.at[tile(src_row)], dst.at[tile(dst_row)], sem)


def _dest_copy(dest_hbm, dsm, dsem, step):
    slot = lax.rem(step, 2)
    return pltpu.make_async_copy(dest_hbm.at[step], dsm.at[slot], dsem.at[slot])


def _scatter_kernel(pend_ref, padded_ref, dest_hbm, h2_ref, xbuf, dsm, dsem, stage, ssem, zbuf, zsem):
    i = pl.program_id(0)
    n = pl.num_programs(0)
    tm = h2_ref.shape[0] // 2
    blk = MOE_BLOCK

    @pl.when(i == 0)
    def _():
        zbuf[...] = jnp.zeros_like(zbuf)

        def zero_copy(e):
            row0 = pl.multiple_of((pend_ref[e] - blk) * ROW_TILE, blk * ROW_TILE)
            return pltpu.make_async_copy(zbuf, xbuf.at[pl.ds(row0, blk * ROW_TILE)], zsem)

        for e in range(N_EXPERTS):
            pl.when(padded_ref[e] > 0)(lambda e=e: zero_copy(e).start())
        for e in range(N_EXPERTS):
            pl.when(padded_ref[e] > 0)(lambda e=e: zero_copy(e).wait())

        def zero_tail(bi, _):
            row0 = pl.multiple_of(bi * (blk * ROW_TILE), blk * ROW_TILE)
            cp = pltpu.make_async_copy(zbuf, xbuf.at[pl.ds(row0, blk * ROW_TILE)], zsem)
            cp.start()
            cp.wait()
            return 0

        lax.fori_loop(pend_ref[N_EXPERTS - 1] // blk, xbuf.shape[0] // (blk * ROW_TILE), zero_tail, 0)
        _dest_copy(dest_hbm, dsm, dsem, i).start()

    _dest_copy(dest_hbm, dsm, dsem, i).wait()

    @pl.when(i + 1 < n)
    def _():
        _dest_copy(dest_hbm, dsm, dsem, i + 1).start()

    slot = lax.rem(i, 2)

    def drain(half):
        for _ in range(TOP_K):
            pltpu.make_async_copy(stage.at[half], xbuf.at[pl.ds(0, tm * ROW_TILE)], ssem.at[half]).wait()

    for half in range(2):
        pl.when(i > 0)(functools.partial(drain, half))
        _to_tiles(stage.at[half], h2_ref[half * tm:(half + 1) * tm, :], tm)
        for r in range(tm):
            for kk in range(TOP_K):
                d = dsm[slot, 0, kk * 2 * tm + half * tm + r]
                _row_dma(stage.at[half], r, xbuf, d, ssem.at[half]).start(priority=kk)

    @pl.when(i == n - 1)
    def _():
        for half in range(2):
            drain(half)


def _scatter_rows(pad_end, padded, dest3, h2, n_pos):
    T = h2.shape[0]
    n_steps = dest3.shape[0]
    tm2 = T // n_steps
    return pl.pallas_call(
        _scatter_kernel,
        grid_spec=pltpu.PrefetchScalarGridSpec(
            num_scalar_prefetch=2,
            grid=(n_steps,),
            in_specs=[
                pl.BlockSpec(memory_space=pl.ANY),
                pl.BlockSpec((tm2, D_MODEL), lambda i, pe, pd: (i, 0)),
            ],
            out_specs=pl.BlockSpec(memory_space=pl.ANY),
            scratch_shapes=[
                pltpu.SMEM((2, 1, dest3.shape[2]), jnp.int32),
                pltpu.SemaphoreType.DMA((2,)),
                pltpu.VMEM((2, (tm2 // 2) * ROW_TILE, LANES), F32),
                pltpu.SemaphoreType.DMA((2,)),
                pltpu.VMEM((MOE_BLOCK * ROW_TILE, LANES), F32),
                pltpu.SemaphoreType.DMA,
            ],
        ),
        out_shape=jax.ShapeDtypeStruct((n_pos * ROW_TILE, LANES), F32),
        compiler_params=_cparams(("arbitrary",)),
        name="scatter_rows",
    )(pad_end, padded, dest3, h2)


def _expert_kernel(be_ref, nv_ref, x_ref, wg_ref, wu_ref, wd_ref, y_ref, wg_s, wu_s, wd_s):
    b = pl.program_id(0)
    blk = MOE_BLOCK

    @pl.when((b == 0) | (be_ref[b] != be_ref[jnp.maximum(b - 1, 0)]))
    def _():
        wg_s[...] = wg_ref[0].astype(BF16)
        wu_s[...] = wu_ref[0].astype(BF16)
        wd_s[...] = wd_ref[0].astype(BF16)

    @pl.when(b < nv_ref[0])
    def _():
        x = jnp.concatenate([_tile_piece(x_ref, s, blk) for s in range(ROW_TILE)], axis=1).astype(BF16)
        g = jnp.dot(x, wg_s[...], preferred_element_type=F32)
        u = jnp.dot(x, wu_s[...], preferred_element_type=F32)
        hid = (g * jax.nn.sigmoid(g) * u).astype(BF16)
        _to_tiles(y_ref, jnp.dot(hid, wd_s[...], preferred_element_type=F32), blk)

    @pl.when(b >= nv_ref[0])
    def _():
        y_ref[...] = jnp.zeros_like(y_ref)


def _experts(block_expert, n_valid, xbuf, wg, wu, wd):
    blk = MOE_BLOCK
    nb = xbuf.shape[0] // (blk * ROW_TILE)
    xmap = lambda b, be, nv: (jnp.minimum(b, nv[0] - 1), 0)
    wmap = lambda b, be, nv: (be[b], 0, 0)
    return pl.pallas_call(
        _expert_kernel,
        grid_spec=pltpu.PrefetchScalarGridSpec(
            num_scalar_prefetch=2,
            grid=(nb,),
            in_specs=[
                pl.BlockSpec((blk * ROW_TILE, LANES), xmap),
                pl.BlockSpec((1, D_MODEL, D_EXPERT), wmap),
                pl.BlockSpec((1, D_MODEL, D_EXPERT), wmap),
                pl.BlockSpec((1, D_EXPERT, D_MODEL), wmap),
            ],
            out_specs=pl.BlockSpec((blk * ROW_TILE, LANES), lambda b, be, nv: (b, 0)),
            scratch_shapes=[
                pltpu.VMEM((D_MODEL, D_EXPERT), BF16),
                pltpu.VMEM((D_MODEL, D_EXPERT), BF16),
                pltpu.VMEM((D_EXPERT, D_MODEL), BF16),
            ],
        ),
        out_shape=jax.ShapeDtypeStruct(xbuf.shape, F32),
        compiler_params=_cparams(("arbitrary",)),
        name="experts",
    )(block_expert, n_valid, xbuf, wg, wu, wd)


def _combine_kernel(dest_hbm, x1_ref, route_ref, ybuf, out_ref, dsm, dsem, buf, gsem):
    i = pl.program_id(0)
    n = pl.num_programs(0)
    tm = x1_ref.shape[0] // 2

    def issue_gather(step, half):
        slot = lax.rem(step, 2)
        for r in range(tm):
            for kk in range(TOP_K):
                d = dsm[slot, 0, kk * 2 * tm + half * tm + r]
                _row_dma(ybuf, d, buf.at[half], kk * tm + r, gsem.at[half]).start(priority=kk)

    def combine_tile(half):
        for _ in range(TOP_K):
            pltpu.make_async_copy(ybuf.at[pl.ds(0, tm * ROW_TILE)],
                                  buf.at[half, pl.ds(0, tm * ROW_TILE)], gsem.at[half]).wait()
        rows = slice(half * tm, (half + 1) * tm)
        w1 = route_ref[rows, 4:5]
        w2 = route_ref[rows, 5:6]
        for s in range(ROW_TILE):
            cols = slice(s * LANES, (s + 1) * LANES)
            out_ref[rows, cols] = x1_ref[rows, cols] + (_tile_piece(buf.at[half], s, tm) * w1
                                                        + _tile_piece(buf.at[half], s, tm, base=tm) * w2)

    @pl.when(i == 0)
    def _():
        _dest_copy(dest_hbm, dsm, dsem, i).start()
        _dest_copy(dest_hbm, dsm, dsem, i).wait()
        issue_gather(i, 0)

    @pl.when(i + 1 < n)
    def _():
        _dest_copy(dest_hbm, dsm, dsem, i + 1).start()

    issue_gather(i, 1)
    combine_tile(0)

    @pl.when(i + 1 < n)
    def _():
        _dest_copy(dest_hbm, dsm, dsem, i + 1).wait()
        issue_gather(i + 1, 0)

    combine_tile(1)


def _combine(dest3, x1, route, ybuf):
    T = x1.shape[0]
    n_steps = dest3.shape[0]
    tm2 = T // n_steps
    return pl.pallas_call(
        _combine_kernel,
        grid=(n_steps,),
        in_specs=[
            pl.BlockSpec(memory_space=pl.ANY),
            pl.BlockSpec((tm2, D_MODEL), lambda i: (i, 0)),
            pl.BlockSpec((tm2, LANES), lambda i: (i, 0)),
            pl.BlockSpec(memory_space=pl.ANY),
        ],
        out_specs=pl.BlockSpec((tm2, D_MODEL), lambda i: (i, 0)),
        out_shape=jax.ShapeDtypeStruct((T, D_MODEL), F32),
        scratch_shapes=[
            pltpu.SMEM((2, 1, dest3.shape[2]), jnp.int32),
            pltpu.SemaphoreType.DMA((2,)),
            pltpu.VMEM((2, TOP_K * (tm2 // 2) * ROW_TILE, LANES), F32),
            pltpu.SemaphoreType.DMA((2,)),
        ],
        compiler_params=_cparams(("arbitrary",)),
        name="combine",
    )(dest3, x1, route, ybuf)


def _layer(x2d, batch, seq, lambda_init, norm1_w, w_in, b_igate, b_fgate, conv_w, conv_b, m_norm_w,
           q_norm_w, k_norm_w, lq1, lk1, lq2, lk2, a_norm_w, w_out, norm2_w,
           w_group, b_group, w_expert, b_expert, w_gate, w_up, w_down):
    T = x2d.shape[0]
    row = lambda v: v.reshape(1, -1).astype(F32)
    g0 = 2 * MIX_HALF + 2 * MIX_HALF
    g1 = g0 + 2 * N_HEADS
    w_proj = jnp.concatenate(
        [w_in[:, :g0], w_in[:, g1:], w_in[:, g0:g1], jnp.zeros((D_MODEL, LANES - 2 * N_HEADS), F32)],
        axis=1).astype(BF16)
    gate_b = jnp.concatenate([b_igate, b_fgate, jnp.zeros((LANES - 2 * N_HEADS,), F32)]).reshape(1, LANES)

    proj, gates = _inproj(x2d, row(norm1_w), w_proj, conv_w, row(conv_b), seq)
    hm = _mlstm(proj, gates, gate_b, row(m_norm_w), batch, seq)
    qw = row(jnp.concatenate([q_norm_w, q_norm_w])) * (QK_DIM ** -0.5 * math.log2(math.e))
    kw = row(jnp.concatenate([k_norm_w, k_norm_w]))
    ha = _diff_attn(proj, qw, kw, row(lq1), row(lk1), row(lq2), row(lk2), row(a_norm_w),
                    batch, seq, lambda_init)

    w_router = jnp.concatenate(
        [w_expert, w_group, jnp.zeros((D_MODEL, LANES - N_EXPERTS - N_GROUPS), F32)], axis=1)
    wr_hi = w_router.astype(BF16)
    wr_lo = (w_router - wr_hi.astype(F32)).astype(BF16)
    b_router = jnp.concatenate(
        [b_expert, b_group, jnp.zeros((LANES - N_EXPERTS - N_GROUPS,), F32)]).reshape(1, LANES)
    x1, h2, route, route_t, counts = _outproj_router(hm, ha, x2d, w_out.astype(BF16), row(norm2_w),
                                                     wr_hi, wr_lo, b_router)

    blk = MOE_BLOCK
    counts = counts[0, :N_EXPERTS].astype(jnp.int32)
    padded = ((counts + blk - 1) // blk) * blk
    pad_end = jnp.cumsum(padded)
    pad_start = pad_end - padded
    nb = (T * TOP_K) // blk + N_EXPERTS
    n_valid = (pad_end[-1] // blk).astype(jnp.int32)
    bstart = jnp.minimum(jnp.arange(nb, dtype=jnp.int32), n_valid - 1) * blk
    block_expert = jnp.minimum(
        jnp.sum(pad_end[None, :] <= bstart[:, None], axis=1), N_EXPERTS - 1).astype(jnp.int32)
    eid = route_t[0:TOP_K].astype(jnp.int32)
    rank = route_t[TOP_K:2 * TOP_K].astype(jnp.int32)
    experts = jnp.arange(N_EXPERTS, dtype=jnp.int32)[:, None, None]
    dest = rank + jnp.sum(jnp.where(eid[None] == experts, pad_start[:, None, None], 0), axis=0)
    tm2 = min(2 * TM_ROWS, T)
    dest3 = dest.reshape(TOP_K, T // tm2, tm2).transpose(1, 0, 2).reshape(T // tm2, 1, TOP_K * tm2)

    xbuf = _scatter_rows(pad_end.astype(jnp.int32), padded - counts, dest3, h2, nb * blk)
    ybuf = _experts(block_expert, n_valid.reshape(1), xbuf, w_gate, w_up, w_down)
    return _combine(dest3, x1, route, ybuf)


def kernel(x, norm1_w, w_in, b_igate, b_fgate, conv_w, conv_b, m_norm_w, q_norm_w, k_norm_w,
           lambda_q1, lambda_k1, lambda_q2, lambda_k2, a_norm_w, w_out, norm2_w,
           w_group, b_group, w_expert, b_expert, w_gate, w_up, w_down):
    batch, seq, d = x.shape
    x2d = x.reshape(batch * seq, d)
    for l in range(norm1_w.shape[0]):
        lambda_init = 0.8 - 0.6 * math.exp(-0.3 * l)
        x2d = _layer(x2d, batch, seq, lambda_init, norm1_w[l], w_in[l], b_igate[l], b_fgate[l],
                     conv_w[l], conv_b[l], m_norm_w[l], q_norm_w[l], k_norm_w[l],
                     lambda_q1[l], lambda_k1[l], lambda_q2[l], lambda_k2[l], a_norm_w[l],
                     w_out[l], norm2_w[l], w_group[l], b_group[l], w_expert[l], b_expert[l],
                     w_gate[l], w_up[l], w_down[l])
    return x2d.reshape(batch, seq, d)
```

```python
import functools
import math

import jax
import jax.numpy as jnp
from jax import lax
from jax.experimental import pallas as pl
from jax.experimental.pallas import tpu as pltpu

F32 = jnp.float32
BF16 = jnp.bfloat16
EPS = 1e-6

D_MODEL = 1024
HEAD_DIM = 128
N_HEADS = 4
MIX_HALF = N_HEADS * HEAD_DIM
QK_DIM = 64
CONV_WIDTH = 4
ATTN_CHUNK = 64
SUM_ROWS = 16
N_GROUPS = 4
EXPERTS_PER_GROUP = 8
N_EXPERTS = N_GROUPS * EXPERTS_PER_GROUP
TOP_K = 2
D_EXPERT = D_MODEL // 2

LANES = 128
PROJ_COLS = 7 * MIX_HALF
VMEM_LIMIT = 48 * 1024 * 1024

TM_PROJ = 512
MLSTM_CHUNK = 128
TQ_ATTN = 256
TK_ATTN = 256
TM_OUT = 1024
ROUTE_ROWS = 8
OUT_PARTS = 8
EXPERT_PARTS = 2
TM_ROWS = 256
MOE_BLOCK = 512


def _cparams(sem):
    return pltpu.CompilerParams(dimension_semantics=sem, vmem_limit_bytes=VMEM_LIMIT)


CONV_HALO = 8


def _inproj_kernel(x_ref, nw_ref, w_ref, cw_ref, cb_ref, proj_ref, gate_ref, tail, *, tiles_per_seq):
    tm = x_ref.shape[0]

    @pl.when(lax.rem(pl.program_id(0), tiles_per_seq) == 0)
    def _():
        tail[...] = jnp.zeros_like(tail)

    x = x_ref[...]
    ms = jnp.mean(x * x, axis=-1, keepdims=True)
    h = (x * lax.rsqrt(ms + EPS) * nw_ref[...]).astype(BF16)
    nb = MIX_HALF
    for j in range(PROJ_COLS // nb):
        cols = slice(j * nb, (j + 1) * nb)
        pre = jnp.dot(h, w_ref[:, cols], preferred_element_type=F32)
        if j < 2:
            win = jnp.concatenate([tail[:, cols], pre], axis=0)
            acc = cb_ref[:, cols]
            for t in range(CONV_WIDTH):
                sh = CONV_WIDTH - 1 - t
                xs = win if sh == 0 else pltpu.roll(win, sh, axis=0)
                acc = acc + xs[CONV_HALO:, :] * cw_ref[t:t + 1, cols]
            tail[:, cols] = pre[tm - CONV_HALO:, :]
            pre = acc * jax.nn.sigmoid(acc)
            if j == 1:
                pre = pre * (HEAD_DIM ** -0.5)
        proj_ref[:, cols] = pre.astype(BF16)
    gate_ref[...] = jnp.dot(h, w_ref[:, PROJ_COLS:], preferred_element_type=F32)


def _inproj(x2d, nw, w, conv_w, conv_b, seq):
    T = x2d.shape[0]
    tm = min(TM_PROJ, seq)
    return pl.pallas_call(
        functools.partial(_inproj_kernel, tiles_per_seq=seq // tm),
        grid=(T // tm,),
        in_specs=[
            pl.BlockSpec((tm, D_MODEL), lambda i: (i, 0)),
            pl.BlockSpec((1, D_MODEL), lambda i: (0, 0)),
            pl.BlockSpec((D_MODEL, PROJ_COLS + LANES), lambda i: (0, 0)),
            pl.BlockSpec((CONV_WIDTH, 2 * MIX_HALF), lambda i: (0, 0)),
            pl.BlockSpec((1, 2 * MIX_HALF), lambda i: (0, 0)),
        ],
        out_specs=[
            pl.BlockSpec((tm, PROJ_COLS), lambda i: (i, 0)),
            pl.BlockSpec((tm, LANES), lambda i: (i, 0)),
        ],
        out_shape=[
            jax.ShapeDtypeStruct((T, PROJ_COLS), BF16),
            jax.ShapeDtypeStruct((T, LANES), F32),
        ],
        scratch_shapes=[pltpu.VMEM((CONV_HALO, 2 * MIX_HALF), F32)],
        compiler_params=_cparams(("arbitrary",)),
        name="inproj",
    )(x2d, nw, w, conv_w, conv_b)


def _log_sigmoid(x):
    return jnp.minimum(x, 0.0) - jnp.log(1.0 + jnp.exp(-jnp.abs(x)))


def _cumsum_lanes(x):
    n = x.shape[1]
    lane = lax.broadcasted_iota(jnp.int32, x.shape, 1)
    s = 1
    while s < n:
        x = x + jnp.where(lane >= s, pltpu.roll(x, s, axis=1), 0.0)
        s *= 2
    return x


def _mlstm_kernel(p_ref, g_ref, gb_ref, nw_ref, out_ref, gt_s, bt_s, bc_s, *c_s, seq, chunk):
    for c_h in c_s:
        c_h[...] = jnp.zeros_like(c_h)
    src_le_out = (lax.broadcasted_iota(jnp.int32, (chunk, chunk), 0)
                  <= lax.broadcasted_iota(jnp.int32, (chunk, chunk), 1))
    gate_rows = 2 * N_HEADS
    nt = (((1,), (1,)), ((), ()))
    split_row = lax.broadcasted_iota(jnp.int32, (16, HEAD_DIM), 0)

    n_chunks = seq // chunk
    for c in range(n_chunks):
        g = g_ref[c * chunk:(c + 1) * chunk, :] + gb_ref[...]
        gt_s[c * gate_rows:(c + 1) * gate_rows, :] = g.T[0:gate_rows, :]
    bt_s[...] = _cumsum_lanes(_log_sigmoid(gt_s[...]))
    for c in range(n_chunks):
        b_t = bt_s[c * gate_rows:(c + 1) * gate_rows, :]
        bc_s[c * chunk:(c + 1) * chunk, :] = pltpu.roll(
            jnp.concatenate([b_t, jnp.zeros((LANES - gate_rows, chunk), F32)], axis=0).T,
            LANES - N_HEADS, axis=1)

    def step(c, carry):
        n_all, m_row = carry
        r0 = pl.multiple_of(c * chunk, chunk)
        rows = pl.ds(r0, chunk)
        g = g_ref[rows, :] + gb_ref[...]
        b_t = bt_s[pl.ds(pl.multiple_of(c * gate_rows, gate_rows), gate_rows), :]
        b_c = bc_s[rows, :]
        a_all = g - b_c
        b_end = b_c[chunk - 1:chunk, :]
        log_w = b_end + a_all
        m_new = jnp.maximum(b_end + m_row, jnp.max(log_w, axis=0, keepdims=True))
        w_all = jnp.exp(log_w - m_new)
        decay_row = jnp.exp(b_end + m_row - m_new)

        heads = range(N_HEADS)
        col = lambda blk, h: slice(blk * MIX_HALF + h * HEAD_DIM, blk * MIX_HALF + (h + 1) * HEAD_DIM)
        q = [p_ref[rows, col(0, h)] for h in heads]
        k = [p_ref[rows, col(1, h)] for h in heads]
        ct = [c_s[h][...] for h in heads]
        vt = [p_ref[rows, col(2, h)].astype(F32).T.astype(BF16) for h in heads]
        b_row = [b_t[N_HEADS + h:N_HEADS + h + 1, :] for h in heads]
        log_d = [jnp.where(src_le_out, b_row[h] + a_all[:, h:h + 1], -jnp.inf) for h in heads]
        log_inter = [b_row[h] + m_row[:, h:h + 1] for h in heads]
        m_t = [jnp.maximum(log_inter[h], jnp.max(log_d[h], axis=0, keepdims=True)) for h in heads]
        inter = [jnp.exp(log_inter[h] - m_t[h]) for h in heads]
        st = [lax.dot_general(k[h], q[h], nt, preferred_element_type=F32) * jnp.exp(log_d[h] - m_t[h])
              for h in heads]
        cq = [lax.dot_general(ct[h].astype(BF16), q[h], nt, preferred_element_type=F32) for h in heads]
        nq = []
        for h in heads:
            n_hi = n_all[h].astype(BF16).astype(F32)
            n_mat = jnp.where(split_row == 0, n_hi,
                              jnp.where(split_row == 1, n_all[h] - n_hi, 0.0)).astype(BF16)
            nq.append(lax.dot_general(n_mat, q[h], nt, preferred_element_type=F32))
        num = [jnp.dot(vt[h], st[h].astype(BF16), preferred_element_type=F32) + inter[h] * cq[h]
               for h in heads]
        den = [jnp.sum(st[h], axis=0, keepdims=True) + inter[h] * (nq[h][0:1, :] + nq[h][1:2, :])
               for h in heads]
        hh = [(num[h] / jnp.maximum(jnp.abs(den[h]), jnp.exp(-m_t[h]))).T for h in heads]

        kw = [k[h].astype(F32) * w_all[:, h:h + 1] for h in heads]
        n_new = []
        for h in heads:
            decay = decay_row[:, h:h + 1]
            c_s[h][...] = decay * ct[h] + jnp.dot(vt[h], kw[h].astype(BF16), preferred_element_type=F32)
            n_new.append(decay * n_all[h] + jnp.sum(kw[h], axis=0, keepdims=True))
        for h in heads:
            hm = jax.nn.sigmoid(p_ref[rows, col(3, h)].astype(F32)) * hh[h]
            y = hm * lax.rsqrt(jnp.mean(hm * hm, axis=-1, keepdims=True) + EPS) * nw_ref[...]
            out_ref[rows, col(0, h)] = y.astype(BF16)
        return tuple(n_new), m_new

    init = (tuple(jnp.zeros((1, HEAD_DIM), F32) for _ in range(N_HEADS)), jnp.zeros((1, LANES), F32))
    lax.fori_loop(0, seq // chunk, step, init)


def _mlstm(proj, gates, gate_b, m_norm_w, batch, seq):
    T = batch * seq
    chunk = MLSTM_CHUNK
    assert chunk == LANES and seq % chunk == 0
    cst = lambda b: (0, 0)
    return pl.pallas_call(
        functools.partial(_mlstm_kernel, seq=seq, chunk=chunk),
        grid=(batch,),
        in_specs=[
            pl.BlockSpec((seq, 4 * MIX_HALF), lambda b: (b, 0)),
            pl.BlockSpec((seq, LANES), lambda b: (b, 0)),
            pl.BlockSpec((1, LANES), cst),
            pl.BlockSpec((1, HEAD_DIM), cst),
        ],
        out_specs=pl.BlockSpec((seq, MIX_HALF), lambda b: (b, 0)),
        out_shape=jax.ShapeDtypeStruct((T, MIX_HALF), BF16),
        scratch_shapes=[
            pltpu.VMEM((seq // chunk * 2 * N_HEADS, LANES), F32),
            pltpu.VMEM((seq // chunk * 2 * N_HEADS, LANES), F32),
            pltpu.VMEM((seq, LANES), F32),
        ] + [pltpu.VMEM((HEAD_DIM, HEAD_DIM), F32) for _ in range(N_HEADS)],
        compiler_params=_cparams(("arbitrary",)),
        name="mlstm",
    )(proj, gates, gate_b, m_norm_w)


def _half_norm(x, w):
    lo = lax.broadcasted_iota(jnp.int32, x.shape, 1) < QK_DIM
    x2 = x * x
    s_lo = jnp.sum(jnp.where(lo, x2, 0.0), axis=-1, keepdims=True)
    s_hi = jnp.sum(jnp.where(lo, 0.0, x2), axis=-1, keepdims=True)
    ms = jnp.where(lo, s_lo, s_hi) * (1.0 / QK_DIM)
    return x * lax.rsqrt(ms + EPS) * w


def _attn_kernel(q_ref, k_ref, v_ref, qw_ref, kw_ref, lq1_ref, lk1_ref, lq2_ref, lk2_ref, anw_ref,
                 out_ref, kn_s, vt_s, *, seq, tq, heads, lambda_init):
    nq = seq // tq
    hcol = lambda hd: slice(hd * HEAD_DIM, (hd + 1) * HEAD_DIM)

    def prep(i, _):
        r0 = pl.multiple_of(i * tq, tq)
        for hd in range(heads):
            kn_s[hd, pl.ds(r0, tq), :] = _half_norm(k_ref[pl.ds(r0, tq), hcol(hd)].astype(F32),
                                                    kw_ref[...]).astype(BF16)
            vt_s[hd, i, 0:HEAD_DIM, :] = v_ref[pl.ds(r0, tq), hcol(hd)].astype(F32).T.astype(BF16)
            vt_s[hd, i, HEAD_DIM:, :] = jnp.ones((SUM_ROWS, tq), BF16)
        return 0

    lax.fori_loop(0, nq, prep, 0)

    lam = (jnp.exp(jnp.sum(lq1_ref[...] * lk1_ref[...], axis=-1, keepdims=True))
           - jnp.exp(jnp.sum(lq2_ref[...] * lk2_ref[...], axis=-1, keepdims=True)) + lambda_init)
    first_map = lax.broadcasted_iota(jnp.int32, (HEAD_DIM, tq), 0) < QK_DIM
    q_in_tile = lax.rem(lax.broadcasted_iota(jnp.int32, (tq, 2 * tq), 1), tq)
    diag_ok = lax.broadcasted_iota(jnp.int32, (tq, 2 * tq), 0) // ATTN_CHUNK <= q_in_tile // ATTN_CHUNK

    def scores(hd, qi):
        r0 = qi * tq
        qt = _half_norm(q_ref[r0:r0 + tq, hcol(hd)].astype(F32), qw_ref[...]).T
        qc = jnp.concatenate([jnp.where(first_map, qt, 0.0), jnp.where(first_map, 0.0, qt)],
                             axis=1).astype(BF16)
        s_d = jnp.where(diag_ok, jnp.dot(kn_s[hd, r0:r0 + tq, :], qc, preferred_element_type=F32), -jnp.inf)
        m = jnp.max(s_d, axis=0, keepdims=True)
        s_f = None
        if qi > 0:
            s_f = jnp.dot(kn_s[hd, 0:r0, :], qc, preferred_element_type=F32)
            m = jnp.maximum(m, jnp.max(s_f, axis=0, keepdims=True))
        return s_d, s_f, m

    def weights(sc):
        s_d, s_f, m = sc
        p_d = jnp.exp2(s_d - m).astype(BF16)
        p_f = None if s_f is None else jnp.exp2(s_f - m).astype(BF16)
        return p_d, p_f

    def output(hd, qi, pw):
        p_d, p_f = pw
        r0 = qi * tq
        acc = jnp.dot(vt_s[hd, qi], p_d, preferred_element_type=F32)
        for j in range(qi):
            acc = acc + jnp.dot(vt_s[hd, j], p_f[j * tq:(j + 1) * tq], preferred_element_type=F32)
        on = acc[0:HEAD_DIM] * (1.0 / acc[HEAD_DIM:HEAD_DIM + 1])
        o = (on[:, 0:tq] - lam * on[:, tq:]).T
        y = o * lax.rsqrt(jnp.mean(o * o, axis=-1, keepdims=True) + EPS) * anw_ref[...] * (1.0 - lambda_init)
        out_ref[r0:r0 + tq, hcol(hd)] = y.astype(BF16)

    hs = range(heads)
    sc = {(hd, 0): scores(hd, 0) for hd in hs}
    pw = {}
    for step in range(nq + 2):
        for hd in hs:
            if step + 1 < nq:
                sc[hd, step + 1] = scores(hd, step + 1)
        for hd in hs:
            if 1 <= step <= nq:
                output(hd, step - 1, pw.pop((hd, step - 1)))
        for hd in hs:
            if step < nq:
                pw[hd, step] = weights(sc.pop((hd, step)))


ATTN_HEADS_PER_STEP = 2


def _diff_attn(proj, qw, kw, lq1, lk1, lq2, lk2, a_norm_w, batch, seq, lambda_init):
    T = batch * seq
    tq = min(TQ_ATTN, seq)
    hps = ATTN_HEADS_PER_STEP
    width = hps * HEAD_DIM
    groups = N_HEADS // hps
    cst = lambda b, g: (0, 0)
    return pl.pallas_call(
        functools.partial(_attn_kernel, seq=seq, tq=tq, heads=hps, lambda_init=lambda_init),
        grid=(batch, groups),
        in_specs=[
            pl.BlockSpec((seq, width), lambda b, g: (b, 4 * groups + g)),
            pl.BlockSpec((seq, width), lambda b, g: (b, 5 * groups + g)),
            pl.BlockSpec((seq, width), lambda b, g: (b, 6 * groups + g)),
            pl.BlockSpec((1, HEAD_DIM), cst),
            pl.BlockSpec((1, HEAD_DIM), cst),
            pl.BlockSpec((1, QK_DIM), cst),
            pl.BlockSpec((1, QK_DIM), cst),
            pl.BlockSpec((1, QK_DIM), cst),
            pl.BlockSpec((1, QK_DIM), cst),
            pl.BlockSpec((1, HEAD_DIM), cst),
        ],
        out_specs=pl.BlockSpec((seq, width), lambda b, g: (b, g)),
        out_shape=jax.ShapeDtypeStruct((T, MIX_HALF), BF16),
        scratch_shapes=[
            pltpu.VMEM((hps, seq, HEAD_DIM), BF16),
            pltpu.VMEM((hps, seq // tq, HEAD_DIM + SUM_ROWS, tq), BF16),
        ],
        compiler_params=_cparams(("arbitrary", "arbitrary")),
        name="diffattn",
    )(proj, proj, proj, qw, kw, lq1, lk1, lq2, lk2, a_norm_w)


def _outproj_router_kernel(hm_ref, ha_ref, x_ref, wo_ref, nw_ref, wr_hi_ref, wr_lo_ref, br_ref,
                           x1_ref, h2_ref, route_ref, route_t_ref, counts_ref, carry):
    i = pl.program_id(0)
    tm = x_ref.shape[0]
    tp = tm // OUT_PARTS
    parts = range(OUT_PARTS)
    rows = [slice(p * tp, (p + 1) * tp) for p in parts]

    @pl.when(i == 0)
    def _():
        carry[...] = jnp.zeros_like(carry)

    mix = [jnp.dot(hm_ref[rows[p], :], wo_ref[0:MIX_HALF, :], preferred_element_type=F32)
           + jnp.dot(ha_ref[rows[p], :], wo_ref[MIX_HALF:, :], preferred_element_type=F32) for p in parts]
    x1 = [x_ref[rows[p], :] + mix[p] for p in parts]
    h2 = [x1[p] * lax.rsqrt(jnp.mean(x1[p] * x1[p], axis=-1, keepdims=True) + EPS) * nw_ref[...]
          for p in parts]
    for p in parts:
        x1_ref[rows[p], :] = x1[p]
        h2_ref[rows[p], :] = h2[p]

    h_hi = [h2[p].astype(BF16) for p in parts]
    h_lo = [(h2[p] - h_hi[p].astype(F32)).astype(BF16) for p in parts]
    logits = [(jnp.dot(h_hi[p], wr_hi_ref[...], preferred_element_type=F32)
               + jnp.dot(h_lo[p], wr_hi_ref[...], preferred_element_type=F32)
               + jnp.dot(h_hi[p], wr_lo_ref[...], preferred_element_type=F32)) + br_ref[...] for p in parts]

    lane_i = lax.broadcasted_iota(jnp.int32, (tp, LANES), 1)
    lane = lane_i.astype(F32)
    big = float(LANES)
    gmask = (lane_i >= N_EXPERTS) & (lane_i < N_EXPERTS + N_GROUPS)
    gl = [jnp.where(gmask, logits[p], -jnp.inf) for p in parts]
    gmax = [jnp.max(gl[p], axis=-1, keepdims=True) for p in parts]
    g_sel = [jnp.min(jnp.where(gl[p] == gmax[p], lane, big), axis=-1, keepdims=True) - float(N_EXPERTS)
             for p in parts]
    g_w = [1.0 / jnp.sum(jnp.where(gmask, jnp.exp(logits[p] - gmax[p]), 0.0), axis=-1, keepdims=True)
           for p in parts]
    e_lo = [g_sel[p] * float(EXPERTS_PER_GROUP) for p in parts]
    el = [jnp.where((lane >= e_lo[p]) & (lane < e_lo[p] + float(EXPERTS_PER_GROUP)), logits[p], -jnp.inf)
          for p in parts]
    v1 = [jnp.max(el[p], axis=-1, keepdims=True) for p in parts]
    e1 = [jnp.min(jnp.where(el[p] == v1[p], lane, big), axis=-1, keepdims=True) for p in parts]
    el2 = [jnp.where(lane == e1[p], -jnp.inf, el[p]) for p in parts]
    v2 = [jnp.max(el2[p], axis=-1, keepdims=True) for p in parts]
    e2 = [jnp.min(jnp.where(el2[p] == v2[p], lane, big), axis=-1, keepdims=True) for p in parts]
    t2 = [jnp.exp(v2[p] - v1[p]) for p in parts]
    w1 = [g_w[p] * (1.0 / (1.0 + t2[p])) for p in parts]
    w2 = [g_w[p] * (t2[p] / (1.0 + t2[p])) for p in parts]

    oh1 = [(lane == e1[p]).astype(F32) for p in parts]
    oh2 = [(lane == e2[p]).astype(F32) for p in parts]
    both = [oh1[p] + oh2[p] for p in parts]
    strict = (lax.broadcasted_iota(jnp.int32, (tp, tp), 0)
              > lax.broadcasted_iota(jnp.int32, (tp, tp), 1)).astype(BF16)
    within = [jnp.dot(strict, both[p].astype(BF16), preferred_element_type=F32) for p in parts]
    before = carry[...]
    for p in parts:
        prefix = within[p] + before
        r1 = jnp.sum(oh1[p] * prefix, axis=-1, keepdims=True)
        r2 = jnp.sum(oh2[p] * prefix, axis=-1, keepdims=True)
        before = before + jnp.sum(both[p], axis=0, keepdims=True)
        route = jnp.where(lane_i == 0, e1[p], 0.0)
        route = jnp.where(lane_i == 1, e2[p], route)
        route = jnp.where(lane_i == 2, r1, route)
        route = jnp.where(lane_i == 3, r2, route)
        route = jnp.where(lane_i == 4, w1[p], route)
        route = jnp.where(lane_i == 5, w2[p], route)
        route_ref[rows[p], :] = route
        route_t_ref[:, rows[p]] = route.T[0:ROUTE_ROWS, :]
    carry[...] = before
    counts_ref[...] = before


def _outproj_router(hm, ha, x2d, wo, nw, wr_hi, wr_lo, br):
    T = x2d.shape[0]
    tm = min(TM_OUT, T)
    row = lambda i: (i, 0)
    cst = lambda i: (0, 0)
    return pl.pallas_call(
        _outproj_router_kernel,
        grid=(T // tm,),
        in_specs=[
            pl.BlockSpec((tm, MIX_HALF), row),
            pl.BlockSpec((tm, MIX_HALF), row),
            pl.BlockSpec((tm, D_MODEL), row),
            pl.BlockSpec((D_MODEL, D_MODEL), cst),
            pl.BlockSpec((1, D_MODEL), cst),
            pl.BlockSpec((D_MODEL, LANES), cst),
            pl.BlockSpec((D_MODEL, LANES), cst),
            pl.BlockSpec((1, LANES), cst),
        ],
        out_specs=[
            pl.BlockSpec((tm, D_MODEL), row),
            pl.BlockSpec((tm, D_MODEL), row),
            pl.BlockSpec((tm, LANES), row),
            pl.BlockSpec((ROUTE_ROWS, tm), lambda i: (0, i)),
            pl.BlockSpec((1, LANES), cst),
        ],
        out_shape=[
            jax.ShapeDtypeStruct((T, D_MODEL), F32),
            jax.ShapeDtypeStruct((T, D_MODEL), F32),
            jax.ShapeDtypeStruct((T, LANES), F32),
            jax.ShapeDtypeStruct((ROUTE_ROWS, T), F32),
            jax.ShapeDtypeStruct((1, LANES), F32),
        ],
        scratch_shapes=[pltpu.VMEM((1, LANES), F32)],
        compiler_params=_cparams(("arbitrary",)),
        name="outproj_router",
    )(hm, ha, x2d, wo, nw, wr_hi, wr_lo, br)


ROW_TILE = D_MODEL // LANES


def _to_tiles(dst_ref, x, n):
    for s in range(ROW_TILE):
        dst_ref[pl.ds(s, n, stride=ROW_TILE), :] = x[:, s * LANES:(s + 1) * LANES]


def _tile_piece(src_ref, s, n, base=0):
    return src_ref[pl.ds(base * ROW_TILE + s, n, stride=ROW_TILE), :]


def _row_dma(src, src_row, dst, dst_row, sem):
    def tile(row):
        start = row * ROW_TILE
        return pl.ds(start if isinstance(row, int) else pl.multiple_of(start, ROW_TILE), ROW_TILE)
    return pltpu.make_async_copy(src.at[tile(src_row)], dst.at[tile(dst_row)], sem)


def _dest_copy(dest_hbm, dsm, dsem, step):
    slot = lax.rem(step, 2)
    return pltpu.make_async_copy(dest_hbm.at[step], dsm.at[slot], dsem.at[slot])


def _scatter_kernel(pend_ref, padded_ref, dest_hbm, h2_ref, xbuf, dsm, dsem, stage, ssem, zbuf, zsem):
    i = pl.program_id(0)
    n = pl.num_programs(0)
    tm = h2_ref.shape[0] // 2
    blk = MOE_BLOCK

    @pl.when(i == 0)
    def _():
        zbuf[...] = jnp.zeros_like(zbuf)

        def zero_copy(e):
            row0 = pl.multiple_of((pend_ref[e] - blk) * ROW_TILE, blk * ROW_TILE)
            return pltpu.make_async_copy(zbuf, xbuf.at[pl.ds(row0, blk * ROW_TILE)], zsem)

        for e in range(N_EXPERTS):
            pl.when(padded_ref[e] > 0)(lambda e=e: zero_copy(e).start())
        for e in range(N_EXPERTS):
            pl.when(padded_ref[e] > 0)(lambda e=e: zero_copy(e).wait())

        def zero_tail(bi, _):
            row0 = pl.multiple_of(bi * (blk * ROW_TILE), blk * ROW_TILE)
            cp = pltpu.make_async_copy(zbuf, xbuf.at[pl.ds(row0, blk * ROW_TILE)], zsem)
            cp.start()
            cp.wait()
            return 0

        lax.fori_loop(pend_ref[N_EXPERTS - 1] // blk, xbuf.shape[0] // (blk * ROW_TILE), zero_tail, 0)
        _dest_copy(dest_hbm, dsm, dsem, i).start()

    _dest_copy(dest_hbm, dsm, dsem, i).wait()

    @pl.when(i + 1 < n)
    def _():
        _dest_copy(dest_hbm, dsm, dsem, i + 1).start()

    slot = lax.rem(i, 2)

    def drain(half):
        for _ in range(TOP_K):
            pltpu.make_async_copy(stage.at[half], xbuf.at[pl.ds(0, tm * ROW_TILE)], ssem.at[half]).wait()

    for half in range(2):
        pl.when(i > 0)(functools.partial(drain, half))
        _to_tiles(stage.at[half], h2_ref[half * tm:(half + 1) * tm, :], tm)
        for r in range(tm):
            for kk in range(TOP_K):
                d = dsm[slot, 0, kk * 2 * tm + half * tm + r]
                _row_dma(stage.at[half], r, xbuf, d, ssem.at[half]).start(priority=kk)

    @pl.when(i == n - 1)
    def _():
        for half in range(2):
            drain(half)


def _scatter_rows(pad_end, padded, dest3, h2, n_pos):
    T = h2.shape[0]
    n_steps = dest3.shape[0]
    tm2 = T // n_steps
    return pl.pallas_call(
        _scatter_kernel,
        grid_spec=pltpu.PrefetchScalarGridSpec(
            num_scalar_prefetch=2,
            grid=(n_steps,),
            in_specs=[
                pl.BlockSpec(memory_space=pl.ANY),
                pl.BlockSpec((tm2, D_MODEL), lambda i, pe, pd: (i, 0)),
            ],
            out_specs=pl.BlockSpec(memory_space=pl.ANY),
            scratch_shapes=[
                pltpu.SMEM((2, 1, dest3.shape[2]), jnp.int32),
                pltpu.SemaphoreType.DMA((2,)),
                pltpu.VMEM((2, (tm2 // 2) * ROW_TILE, LANES), F32),
                pltpu.SemaphoreType.DMA((2,)),
                pltpu.VMEM((MOE_BLOCK * ROW_TILE, LANES), F32),
                pltpu.SemaphoreType.DMA,
            ],
        ),
        out_shape=jax.ShapeDtypeStruct((n_pos * ROW_TILE, LANES), F32),
        compiler_params=_cparams(("arbitrary",)),
        name="scatter_rows",
    )(pad_end, padded, dest3, h2)


def _expert_kernel(be_ref, nv_ref, x_ref, wg_ref, wu_ref, wd_ref, y_ref, wg_s, wu_s, wd_s):
    b = pl.program_id(0)
    blk = MOE_BLOCK

    @pl.when((b == 0) | (be_ref[b] != be_ref[jnp.maximum(b - 1, 0)]))
    def _():
        wg_s[...] = wg_ref[0].astype(BF16)
        wu_s[...] = wu_ref[0].astype(BF16)
        wd_s[...] = wd_ref[0].astype(BF16)

    @pl.when(b < nv_ref[0])
    def _():
        parts = range(EXPERT_PARTS)
        tp = blk // EXPERT_PARTS
        x = [jnp.concatenate([_tile_piece(x_ref, s, tp, base=p * tp) for s in range(ROW_TILE)],
                             axis=1).astype(BF16) for p in parts]
        g = [jnp.dot(x[p], wg_s[...], preferred_element_type=F32) for p in parts]
        u = [jnp.dot(x[p], wu_s[...], preferred_element_type=F32) for p in parts]
        hid = [(g[p] * jax.nn.sigmoid(g[p]) * u[p]).astype(BF16) for p in parts]
        y = [jnp.dot(hid[p], wd_s[...], preferred_element_type=F32) for p in parts]
        for p in parts:
            _to_tiles(y_ref.at[pl.ds(p * tp * ROW_TILE, tp * ROW_TILE)], y[p], tp)

    @pl.when(b >= nv_ref[0])
    def _():
        y_ref[...] = jnp.zeros_like(y_ref)


def _experts(block_expert, n_valid, xbuf, wg, wu, wd):
    blk = MOE_BLOCK
    nb = xbuf.shape[0] // (blk * ROW_TILE)
    xmap = lambda b, be, nv: (jnp.minimum(b, nv[0] - 1), 0)
    wmap = lambda b, be, nv: (be[b], 0, 0)
    return pl.pallas_call(
        _expert_kernel,
        grid_spec=pltpu.PrefetchScalarGridSpec(
            num_scalar_prefetch=2,
            grid=(nb,),
            in_specs=[
                pl.BlockSpec((blk * ROW_TILE, LANES), xmap),
                pl.BlockSpec((1, D_MODEL, D_EXPERT), wmap),
                pl.BlockSpec((1, D_MODEL, D_EXPERT), wmap),
                pl.BlockSpec((1, D_EXPERT, D_MODEL), wmap),
            ],
            out_specs=pl.BlockSpec((blk * ROW_TILE, LANES), lambda b, be, nv: (b, 0)),
            scratch_shapes=[
                pltpu.VMEM((D_MODEL, D_EXPERT), BF16),
                pltpu.VMEM((D_MODEL, D_EXPERT), BF16),
                pltpu.VMEM((D_EXPERT, D_MODEL), BF16),
            ],
        ),
        out_shape=jax.ShapeDtypeStruct(xbuf.shape, F32),
        compiler_params=_cparams(("arbitrary",)),
        name="experts",
    )(block_expert, n_valid, xbuf, wg, wu, wd)


def _combine_kernel(dest_hbm, x1_ref, route_ref, ybuf, out_ref, dsm, dsem, buf, gsem):
    i = pl.program_id(0)
    n = pl.num_programs(0)
    tm = x1_ref.shape[0] // 2

    def issue_gather(step, half):
        slot = lax.rem(step, 2)
        for r in range(tm):
            for kk in range(TOP_K):
                d = dsm[slot, 0, kk * 2 * tm + half * tm + r]
                _row_dma(ybuf, d, buf.at[half], kk * tm + r, gsem.at[half]).start(priority=kk)

    def combine_tile(half):
        for _ in range(TOP_K):
            pltpu.make_async_copy(ybuf.at[pl.ds(0, tm * ROW_TILE)],
                                  buf.at[half, pl.ds(0, tm * ROW_TILE)], gsem.at[half]).wait()
        rows = slice(half * tm, (half + 1) * tm)
        w1 = route_ref[rows, 4:5]
        w2 = route_ref[rows, 5:6]
        for s in range(ROW_TILE):
            cols = slice(s * LANES, (s + 1) * LANES)
            out_ref[rows, cols] = x1_ref[rows, cols] + (_tile_piece(buf.at[half], s, tm) * w1
                                                        + _tile_piece(buf.at[half], s, tm, base=tm) * w2)

    @pl.when(i == 0)
    def _():
        _dest_copy(dest_hbm, dsm, dsem, i).start()
        _dest_copy(dest_hbm, dsm, dsem, i).wait()
        issue_gather(i, 0)

    @pl.when(i + 1 < n)
    def _():
        _dest_copy(dest_hbm, dsm, dsem, i + 1).start()

    issue_gather(i, 1)
    combine_tile(0)

    @pl.when(i + 1 < n)
    def _():
        _dest_copy(dest_hbm, dsm, dsem, i + 1).wait()
        issue_gather(i + 1, 0)

    combine_tile(1)


def _combine(dest3, x1, route, ybuf):
    T = x1.shape[0]
    n_steps = dest3.shape[0]
    tm2 = T // n_steps
    return pl.pallas_call(
        _combine_kernel,
        grid=(n_steps,),
        in_specs=[
            pl.BlockSpec(memory_space=pl.ANY),
            pl.BlockSpec((tm2, D_MODEL), lambda i: (i, 0)),
            pl.BlockSpec((tm2, LANES), lambda i: (i, 0)),
            pl.BlockSpec(memory_space=pl.ANY),
        ],
        out_specs=pl.BlockSpec((tm2, D_MODEL), lambda i: (i, 0)),
        out_shape=jax.ShapeDtypeStruct((T, D_MODEL), F32),
        scratch_shapes=[
            pltpu.SMEM((2, 1, dest3.shape[2]), jnp.int32),
            pltpu.SemaphoreType.DMA((2,)),
            pltpu.VMEM((2, TOP_K * (tm2 // 2) * ROW_TILE, LANES), F32),
            pltpu.SemaphoreType.DMA((2,)),
        ],
        compiler_params=_cparams(("arbitrary",)),
        name="combine",
    )(dest3, x1, route, ybuf)


def _layer(x2d, batch, seq, lambda_init, norm1_w, w_in, b_igate, b_fgate, conv_w, conv_b, m_norm_w,
           q_norm_w, k_norm_w, lq1, lk1, lq2, lk2, a_norm_w, w_out, norm2_w,
           w_group, b_group, w_expert, b_expert, w_gate, w_up, w_down):
    T = x2d.shape[0]
    row = lambda v: v.reshape(1, -1).astype(F32)
    g0 = 2 * MIX_HALF + 2 * MIX_HALF
    g1 = g0 + 2 * N_HEADS
    w_proj = jnp.concatenate(
        [w_in[:, :g0], w_in[:, g1:], w_in[:, g0:g1], jnp.zeros((D_MODEL, LANES - 2 * N_HEADS), F32)],
        axis=1).astype(BF16)
    gate_b = jnp.concatenate([b_igate, b_fgate, jnp.zeros((LANES - 2 * N_HEADS,), F32)]).reshape(1, LANES)

    proj, gates = _inproj(x2d, row(norm1_w), w_proj, conv_w, row(conv_b), seq)
    hm = _mlstm(proj, gates, gate_b, row(m_norm_w), batch, seq)
    qw = row(jnp.concatenate([q_norm_w, q_norm_w])) * (QK_DIM ** -0.5 * math.log2(math.e))
    kw = row(jnp.concatenate([k_norm_w, k_norm_w]))
    ha = _diff_attn(proj, qw, kw, row(lq1), row(lk1), row(lq2), row(lk2), row(a_norm_w),
                    batch, seq, lambda_init)

    w_router = jnp.concatenate(
        [w_expert, w_group, jnp.zeros((D_MODEL, LANES - N_EXPERTS - N_GROUPS), F32)], axis=1)
    wr_hi = w_router.astype(BF16)
    wr_lo = (w_router - wr_hi.astype(F32)).astype(BF16)
    b_router = jnp.concatenate(
        [b_expert, b_group, jnp.zeros((LANES - N_EXPERTS - N_GROUPS,), F32)]).reshape(1, LANES)
    x1, h2, route, route_t, counts = _outproj_router(hm, ha, x2d, w_out.astype(BF16), row(norm2_w),
                                                     wr_hi, wr_lo, b_router)

    blk = MOE_BLOCK
    counts = counts[0, :N_EXPERTS].astype(jnp.int32)
    padded = ((counts + blk - 1) // blk) * blk
    pad_end = jnp.cumsum(padded)
    pad_start = pad_end - padded
    nb = (T * TOP_K) // blk + N_EXPERTS
    n_valid = (pad_end[-1] // blk).astype(jnp.int32)
    bstart = jnp.minimum(jnp.arange(nb, dtype=jnp.int32), n_valid - 1) * blk
    block_expert = jnp.minimum(
        jnp.sum(pad_end[None, :] <= bstart[:, None], axis=1), N_EXPERTS - 1).astype(jnp.int32)
    eid = route_t[0:TOP_K].astype(jnp.int32)
    rank = route_t[TOP_K:2 * TOP_K].astype(jnp.int32)
    experts = jnp.arange(N_EXPERTS, dtype=jnp.int32)[:, None, None]
    dest = rank + jnp.sum(jnp.where(eid[None] == experts, pad_start[:, None, None], 0), axis=0)
    tm2 = min(2 * TM_ROWS, T)
    dest3 = dest.reshape(TOP_K, T // tm2, tm2).transpose(1, 0, 2).reshape(T // tm2, 1, TOP_K * tm2)

    xbuf = _scatter_rows(pad_end.astype(jnp.int32), padded - counts, dest3, h2, nb * blk)
    ybuf = _experts(block_expert, n_valid.reshape(1), xbuf, w_gate, w_up, w_down)
    return _combine(dest3, x1, route, ybuf)


def kernel(x, norm1_w, w_in, b_igate, b_fgate, conv_w, conv_b, m_norm_w, q_norm_w, k_norm_w,
           lambda_q1, lambda_k1, lambda_q2, lambda_k2, a_norm_w, w_out, norm2_w,
           w_group, b_group, w_expert, b_expert, w_gate, w_up, w_down):
    batch, seq, d = x.shape
    x2d = x.reshape(batch * seq, d)
    for l in range(norm1_w.shape[0]):
        lambda_init = 0.8 - 0.6 * math.exp(-0.3 * l)
        x2d = _layer(x2d, batch, seq, lambda_init, norm1_w[l], w_in[l], b_igate[l], b_fgate[l],
                     conv_w[l], conv_b[l], m_norm_w[l], q_norm_w[l], k_norm_w[l],
                     lambda_q1[l], lambda_k1[l], lambda_q2[l], lambda_k2[l], a_norm_w[l],
                     w_out[l], norm2_w[l], w_group[l], b_group[l], w_expert[l], b_expert[l],
                     w_gate[l], w_up[l], w_down[l])
    return x2d.reshape(batch, seq, d)
```

```python
import functools
import math

import jax
import jax.numpy as jnp
from jax import lax
from jax.experimental import pallas as pl
from jax.experimental.pallas import tpu as pltpu

F32 = jnp.float32
BF16 = jnp.bfloat16
EPS = 1e-6

D_MODEL = 1024
HEAD_DIM = 128
N_HEADS = 4
MIX_HALF = N_HEADS * HEAD_DIM
QK_DIM = 64
CONV_WIDTH = 4
ATTN_CHUNK = 64
SUM_ROWS = 16
N_GROUPS = 4
EXPERTS_PER_GROUP = 8
N_EXPERTS = N_GROUPS * EXPERTS_PER_GROUP
TOP_K = 2
D_EXPERT = D_MODEL // 2

LANES = 128
PROJ_COLS = 7 * MIX_HALF
VMEM_LIMIT = 48 * 1024 * 1024

TM_PROJ = 512
MLSTM_CHUNK = 128
TQ_ATTN = 256
TM_OUT = 1024
ROUTE_ROWS = 8
OUT_PARTS = 8
EXPERT_PARTS = 2
TM_ROWS = 256
MOE_BLOCK = 512


def _cparams(sem):
    return pltpu.CompilerParams(dimension_semantics=sem, vmem_limit_bytes=VMEM_LIMIT)


CONV_HALO = 8


def _half_norm(x, w):
    lo = lax.broadcasted_iota(jnp.int32, x.shape, 1) < QK_DIM
    x2 = x * x
    s_lo = jnp.sum(jnp.where(lo, x2, 0.0), axis=-1, keepdims=True)
    s_hi = jnp.sum(jnp.where(lo, 0.0, x2), axis=-1, keepdims=True)
    ms = jnp.where(lo, s_lo, s_hi) * (1.0 / QK_DIM)
    return x * lax.rsqrt(ms + EPS) * w


def _inproj_kernel(x_ref, nw_ref, w_ref, cw_ref, cb_ref, qw_ref, kw_ref,
                   proj_ref, gate_ref, kn_ref, qt_ref, vt_ref, tail, *, tiles_per_seq):
    tm = x_ref.shape[0]

    @pl.when(lax.rem(pl.program_id(0), tiles_per_seq) == 0)
    def _():
        tail[...] = jnp.zeros_like(tail)

    x = x_ref[...]
    ms = jnp.mean(x * x, axis=-1, keepdims=True)
    h = (x * lax.rsqrt(ms + EPS) * nw_ref[...]).astype(BF16)
    nb = MIX_HALF
    head = lambda hd: slice(hd * HEAD_DIM, (hd + 1) * HEAD_DIM)
    vrows = HEAD_DIM + SUM_ROWS
    for j in range(PROJ_COLS // nb):
        cols = slice(j * nb, (j + 1) * nb)
        pre = jnp.dot(h, w_ref[:, cols], preferred_element_type=F32)
        if j < 2:
            win = jnp.concatenate([tail[:, cols], pre], axis=0)
            acc = cb_ref[:, cols]
            for t in range(CONV_WIDTH):
                sh = CONV_WIDTH - 1 - t
                xs = win if sh == 0 else pltpu.roll(win, sh, axis=0)
                acc = acc + xs[CONV_HALO:, :] * cw_ref[t:t + 1, cols]
            tail[:, cols] = pre[tm - CONV_HALO:, :]
            pre = acc * jax.nn.sigmoid(acc)
            if j == 1:
                pre = pre * (HEAD_DIM ** -0.5)
            proj_ref[:, cols] = pre.astype(BF16)
        elif j < 4:
            proj_ref[:, cols] = pre.astype(BF16)
        elif j == 4:
            for hd in range(N_HEADS):
                qt_ref[head(hd), :] = _half_norm(pre[:, head(hd)], qw_ref[...]).T.astype(BF16)
        elif j == 5:
            for hd in range(N_HEADS):
                kn_ref[:, head(hd)] = _half_norm(pre[:, head(hd)], kw_ref[...]).astype(BF16)
        else:
            for hd in range(N_HEADS):
                vt_ref[hd * vrows:hd * vrows + HEAD_DIM, :] = pre[:, head(hd)].T.astype(BF16)
                vt_ref[hd * vrows + HEAD_DIM:(hd + 1) * vrows, :] = jnp.ones((SUM_ROWS, tm), BF16)
    gate_ref[...] = jnp.dot(h, w_ref[:, PROJ_COLS:], preferred_element_type=F32)


def _inproj(x2d, nw, w, conv_w, conv_b, qw, kw, seq):
    T = x2d.shape[0]
    tm = min(TM_PROJ, seq)
    cst = lambda i: (0, 0)
    row = lambda i: (i, 0)
    col = lambda i: (0, i)
    vrows = N_HEADS * (HEAD_DIM + SUM_ROWS)
    return pl.pallas_call(
        functools.partial(_inproj_kernel, tiles_per_seq=seq // tm),
        grid=(T // tm,),
        in_specs=[
            pl.BlockSpec((tm, D_MODEL), row),
            pl.BlockSpec((1, D_MODEL), cst),
            pl.BlockSpec((D_MODEL, PROJ_COLS + LANES), cst),
            pl.BlockSpec((CONV_WIDTH, 2 * MIX_HALF), cst),
            pl.BlockSpec((1, 2 * MIX_HALF), cst),
            pl.BlockSpec((1, HEAD_DIM), cst),
            pl.BlockSpec((1, HEAD_DIM), cst),
        ],
        out_specs=[
            pl.BlockSpec((tm, 4 * MIX_HALF), row),
            pl.BlockSpec((tm, LANES), row),
            pl.BlockSpec((tm, MIX_HALF), row),
            pl.BlockSpec((MIX_HALF, tm), col),
            pl.BlockSpec((vrows, tm), col),
        ],
        out_shape=[
            jax.ShapeDtypeStruct((T, 4 * MIX_HALF), BF16),
            jax.ShapeDtypeStruct((T, LANES), F32),
            jax.ShapeDtypeStruct((T, MIX_HALF), BF16),
            jax.ShapeDtypeStruct((MIX_HALF, T), BF16),
            jax.ShapeDtypeStruct((vrows, T), BF16),
        ],
        scratch_shapes=[pltpu.VMEM((CONV_HALO, 2 * MIX_HALF), F32)],
        compiler_params=_cparams(("arbitrary",)),
        name="inproj",
    )(x2d, nw, w, conv_w, conv_b, qw, kw)


def _log_sigmoid(x):
    return jnp.minimum(x, 0.0) - jnp.log(1.0 + jnp.exp(-jnp.abs(x)))


def _cumsum_lanes(x):
    n = x.shape[1]
    lane = lax.broadcasted_iota(jnp.int32, x.shape, 1)
    s = 1
    while s < n:
        x = x + jnp.where(lane >= s, pltpu.roll(x, s, axis=1), 0.0)
        s *= 2
    return x


def _mlstm_kernel(p_ref, g_ref, gb_ref, nw_ref, out_ref, gt_s, bt_s, bc_s, *c_s, seq, chunk):
    for c_h in c_s:
        c_h[...] = jnp.zeros_like(c_h)
    src_le_out = (lax.broadcasted_iota(jnp.int32, (chunk, chunk), 0)
                  <= lax.broadcasted_iota(jnp.int32, (chunk, chunk), 1))
    gate_rows = 2 * N_HEADS
    nt = (((1,), (1,)), ((), ()))
    split_row = lax.broadcasted_iota(jnp.int32, (16, HEAD_DIM), 0)

    n_chunks = seq // chunk
    for c in range(n_chunks):
        g = g_ref[c * chunk:(c + 1) * chunk, :] + gb_ref[...]
        gt_s[c * gate_rows:(c + 1) * gate_rows, :] = g.T[0:gate_rows, :]
    bt_s[...] = _cumsum_lanes(_log_sigmoid(gt_s[...]))
    for c in range(n_chunks):
        b_t = bt_s[c * gate_rows:(c + 1) * gate_rows, :]
        bc_s[c * chunk:(c + 1) * chunk, :] = pltpu.roll(
            jnp.concatenate([b_t, jnp.zeros((LANES - gate_rows, chunk), F32)], axis=0).T,
            LANES - N_HEADS, axis=1)

    def step(c, carry):
        n_all, m_row = carry
        r0 = pl.multiple_of(c * chunk, chunk)
        rows = pl.ds(r0, chunk)
        g = g_ref[rows, :] + gb_ref[...]
        b_t = bt_s[pl.ds(pl.multiple_of(c * gate_rows, gate_rows), gate_rows), :]
        b_c = bc_s[rows, :]
        a_all = g - b_c
        b_end = b_c[chunk - 1:chunk, :]
        log_w = b_end + a_all
        m_new = jnp.maximum(b_end + m_row, jnp.max(log_w, axis=0, keepdims=True))
        w_all = jnp.exp(log_w - m_new)
        decay_row = jnp.exp(b_end + m_row - m_new)

        heads = range(N_HEADS)
        col = lambda blk, h: slice(blk * MIX_HALF + h * HEAD_DIM, blk * MIX_HALF + (h + 1) * HEAD_DIM)
        q = [p_ref[rows, col(0, h)] for h in heads]
        k = [p_ref[rows, col(1, h)] for h in heads]
        ct = [c_s[h][...] for h in heads]
        vt = [p_ref[rows, col(2, h)].astype(F32).T.astype(BF16) for h in heads]
        b_row = [b_t[N_HEADS + h:N_HEADS + h + 1, :] for h in heads]
        log_d = [jnp.where(src_le_out, b_row[h] + a_all[:, h:h + 1], -jnp.inf) for h in heads]
        log_inter = [b_row[h] + m_row[:, h:h + 1] for h in heads]
        m_t = [jnp.maximum(log_inter[h], jnp.max(log_d[h], axis=0, keepdims=True)) for h in heads]
        inter = [jnp.exp(log_inter[h] - m_t[h]) for h in heads]
        st = [lax.dot_general(k[h], q[h], nt, preferred_element_type=F32) * jnp.exp(log_d[h] - m_t[h])
              for h in heads]
        cq = [lax.dot_general(ct[h].astype(BF16), q[h], nt, preferred_element_type=F32) for h in heads]
        nq = []
        for h in heads:
            n_hi = n_all[h].astype(BF16).astype(F32)
            n_mat = jnp.where(split_row == 0, n_hi,
                              jnp.where(split_row == 1, n_all[h] - n_hi, 0.0)).astype(BF16)
            nq.append(lax.dot_general(n_mat, q[h], nt, preferred_element_type=F32))
        num = [jnp.dot(vt[h], st[h].astype(BF16), preferred_element_type=F32) + inter[h] * cq[h]
               for h in heads]
        den = [jnp.sum(st[h], axis=0, keepdims=True) + inter[h] * (nq[h][0:1, :] + nq[h][1:2, :])
               for h in heads]
        hh = [(num[h] / jnp.maximum(jnp.abs(den[h]), jnp.exp(-m_t[h]))).T for h in heads]

        kw = [k[h].astype(F32) * w_all[:, h:h + 1] for h in heads]
        n_new = []
        for h in heads:
            decay = decay_row[:, h:h + 1]
            c_s[h][...] = decay * ct[h] + jnp.dot(vt[h], kw[h].astype(BF16), preferred_element_type=F32)
            n_new.append(decay * n_all[h] + jnp.sum(kw[h], axis=0, keepdims=True))
        for h in heads:
            hm = jax.nn.sigmoid(p_ref[rows, col(3, h)].astype(F32)) * hh[h]
            y = hm * lax.rsqrt(jnp.mean(hm * hm, axis=-1, keepdims=True) + EPS) * nw_ref[...]
            out_ref[rows, col(0, h)] = y.astype(BF16)
        return tuple(n_new), m_new

    init = (tuple(jnp.zeros((1, HEAD_DIM), F32) for _ in range(N_HEADS)), jnp.zeros((1, LANES), F32))
    lax.fori_loop(0, seq // chunk, step, init)


def _mlstm(proj, gates, gate_b, m_norm_w, batch, seq):
    T = batch * seq
    chunk = MLSTM_CHUNK
    assert chunk == LANES and seq % chunk == 0
    cst = lambda b: (0, 0)
    return pl.pallas_call(
        functools.partial(_mlstm_kernel, seq=seq, chunk=chunk),
        grid=(batch,),
        in_specs=[
            pl.BlockSpec((seq, 4 * MIX_HALF), lambda b: (b, 0)),
            pl.BlockSpec((seq, LANES), lambda b: (b, 0)),
            pl.BlockSpec((1, LANES), cst),
            pl.BlockSpec((1, HEAD_DIM), cst),
        ],
        out_specs=pl.BlockSpec((seq, MIX_HALF), lambda b: (b, 0)),
        out_shape=jax.ShapeDtypeStruct((T, MIX_HALF), BF16),
        scratch_shapes=[
            pltpu.VMEM((seq // chunk * 2 * N_HEADS, LANES), F32),
            pltpu.VMEM((seq // chunk * 2 * N_HEADS, LANES), F32),
            pltpu.VMEM((seq, LANES), F32),
        ] + [pltpu.VMEM((HEAD_DIM, HEAD_DIM), F32) for _ in range(N_HEADS)],
        compiler_params=_cparams(("arbitrary",)),
        name="mlstm",
    )(proj, gates, gate_b, m_norm_w)


def _attn_kernel(qt_ref, kn_ref, vt_ref, lq1_ref, lk1_ref, lq2_ref, lk2_ref, anw_ref, out_ref,
                 *, seq, tq, heads, lambda_init):
    nq = seq // tq
    hcol = lambda hd: slice(hd * HEAD_DIM, (hd + 1) * HEAD_DIM)
    vrows = HEAD_DIM + SUM_ROWS

    lam = (jnp.exp(jnp.sum(lq1_ref[...] * lk1_ref[...], axis=-1, keepdims=True))
           - jnp.exp(jnp.sum(lq2_ref[...] * lk2_ref[...], axis=-1, keepdims=True)) + lambda_init)
    first_map = lax.broadcasted_iota(jnp.int32, (HEAD_DIM, tq), 0) < QK_DIM
    q_in_tile = lax.rem(lax.broadcasted_iota(jnp.int32, (tq, 2 * tq), 1), tq)
    diag_ok = lax.broadcasted_iota(jnp.int32, (tq, 2 * tq), 0) // ATTN_CHUNK <= q_in_tile // ATTN_CHUNK

    def v_tile(hd, j):
        return vt_ref[hd * vrows:(hd + 1) * vrows, j * tq:(j + 1) * tq]

    def scores(hd, qi):
        r0 = qi * tq
        qt = qt_ref[hcol(hd), r0:r0 + tq]
        zero = jnp.zeros_like(qt)
        qc = jnp.concatenate([jnp.where(first_map, qt, zero), jnp.where(first_map, zero, qt)], axis=1)
        s_d = jnp.where(diag_ok, jnp.dot(kn_ref[r0:r0 + tq, hcol(hd)], qc, preferred_element_type=F32),
                        -jnp.inf)
        m = jnp.max(s_d, axis=0, keepdims=True)
        s_f = None
        if qi > 0:
            s_f = jnp.dot(kn_ref[0:r0, hcol(hd)], qc, preferred_element_type=F32)
            m = jnp.maximum(m, jnp.max(s_f, axis=0, keepdims=True))
        return s_d, s_f, m

    def weights(sc):
        s_d, s_f, m = sc
        p_d = jnp.exp2(s_d - m).astype(BF16)
        p_f = None if s_f is None else jnp.exp2(s_f - m).astype(BF16)
        return p_d, p_f

    def output(hd, qi, pw):
        p_d, p_f = pw
        r0 = qi * tq
        acc = jnp.dot(v_tile(hd, qi), p_d, preferred_element_type=F32)
        for j in range(qi):
            acc = acc + jnp.dot(v_tile(hd, j), p_f[j * tq:(j + 1) * tq], preferred_element_type=F32)
        on = acc[0:HEAD_DIM] * (1.0 / acc[HEAD_DIM:HEAD_DIM + 1])
        o = (on[:, 0:tq] - lam * on[:, tq:]).T
        y = o * lax.rsqrt(jnp.mean(o * o, axis=-1, keepdims=True) + EPS) * anw_ref[...] * (1.0 - lambda_init)
        out_ref[r0:r0 + tq, hcol(hd)] = y.astype(BF16)

    hs = range(heads)
    sc = {(hd, 0): scores(hd, 0) for hd in hs}
    pw = {}
    for step in range(nq + 2):
        for hd in hs:
            if step + 1 < nq:
                sc[hd, step + 1] = scores(hd, step + 1)
        for hd in hs:
            if 1 <= step <= nq:
                output(hd, step - 1, pw.pop((hd, step - 1)))
        for hd in hs:
            if step < nq:
                pw[hd, step] = weights(sc.pop((hd, step)))


ATTN_HEADS_PER_STEP = 2


def _diff_attn(qt, kn, vt, lq1, lk1, lq2, lk2, a_norm_w, batch, seq, lambda_init):
    T = batch * seq
    tq = min(TQ_ATTN, seq)
    hps = ATTN_HEADS_PER_STEP
    width = hps * HEAD_DIM
    cst = lambda b, g: (0, 0)
    return pl.pallas_call(
        functools.partial(_attn_kernel, seq=seq, tq=tq, heads=hps, lambda_init=lambda_init),
        grid=(batch, N_HEADS // hps),
        in_specs=[
            pl.BlockSpec((width, seq), lambda b, g: (g, b)),
            pl.BlockSpec((seq, width), lambda b, g: (b, g)),
            pl.BlockSpec((hps * (HEAD_DIM + SUM_ROWS), seq), lambda b, g: (g, b)),
            pl.BlockSpec((1, QK_DIM), cst),
            pl.BlockSpec((1, QK_DIM), cst),
            pl.BlockSpec((1, QK_DIM), cst),
            pl.BlockSpec((1, QK_DIM), cst),
            pl.BlockSpec((1, HEAD_DIM), cst),
        ],
        out_specs=pl.BlockSpec((seq, width), lambda b, g: (b, g)),
        out_shape=jax.ShapeDtypeStruct((T, MIX_HALF), BF16),
        compiler_params=_cparams(("arbitrary", "arbitrary")),
        name="diffattn",
    )(qt, kn, vt, lq1, lk1, lq2, lk2, a_norm_w)


def _outproj_router_kernel(hm_ref, ha_ref, x_ref, wo_ref, nw_ref, wr_hi_ref, wr_lo_ref, br_ref,
                           x1_ref, h2_ref, route_ref, route_t_ref, counts_ref, carry):
    i = pl.program_id(0)
    tm = x_ref.shape[0]
    tp = tm // OUT_PARTS
    parts = range(OUT_PARTS)
    rows = [slice(p * tp, (p + 1) * tp) for p in parts]

    @pl.when(i == 0)
    def _():
        carry[...] = jnp.zeros_like(carry)

    mix = [jnp.dot(hm_ref[rows[p], :], wo_ref[0:MIX_HALF, :], preferred_element_type=F32)
           + jnp.dot(ha_ref[rows[p], :], wo_ref[MIX_HALF:, :], preferred_element_type=F32) for p in parts]
    x1 = [x_ref[rows[p], :] + mix[p] for p in parts]
    h2 = [x1[p] * lax.rsqrt(jnp.mean(x1[p] * x1[p], axis=-1, keepdims=True) + EPS) * nw_ref[...]
          for p in parts]
    for p in parts:
        x1_ref[rows[p], :] = x1[p]
        h2_ref[rows[p], :] = h2[p]

    h_hi = [h2[p].astype(BF16) for p in parts]
    h_lo = [(h2[p] - h_hi[p].astype(F32)).astype(BF16) for p in parts]
    logits = [(jnp.dot(h_hi[p], wr_hi_ref[...], preferred_element_type=F32)
               + jnp.dot(h_lo[p], wr_hi_ref[...], preferred_element_type=F32)
               + jnp.dot(h_hi[p], wr_lo_ref[...], preferred_element_type=F32)) + br_ref[...] for p in parts]

    lane_i = lax.broadcasted_iota(jnp.int32, (tp, LANES), 1)
    lane = lane_i.astype(F32)
    big = float(LANES)
    gmask = (lane_i >= N_EXPERTS) & (lane_i < N_EXPERTS + N_GROUPS)
    gl = [jnp.where(gmask, logits[p], -jnp.inf) for p in parts]
    gmax = [jnp.max(gl[p], axis=-1, keepdims=True) for p in parts]
    g_sel = [jnp.min(jnp.where(gl[p] == gmax[p], lane, big), axis=-1, keepdims=True) - float(N_EXPERTS)
             for p in parts]
    g_w = [1.0 / jnp.sum(jnp.where(gmask, jnp.exp(logits[p] - gmax[p]), 0.0), axis=-1, keepdims=True)
           for p in parts]
    e_lo = [g_sel[p] * float(EXPERTS_PER_GROUP) for p in parts]
    el = [jnp.where((lane >= e_lo[p]) & (lane < e_lo[p] + float(EXPERTS_PER_GROUP)), logits[p], -jnp.inf)
          for p in parts]
    v1 = [jnp.max(el[p], axis=-1, keepdims=True) for p in parts]
    e1 = [jnp.min(jnp.where(el[p] == v1[p], lane, big), axis=-1, keepdims=True) for p in parts]
    el2 = [jnp.where(lane == e1[p], -jnp.inf, el[p]) for p in parts]
    v2 = [jnp.max(el2[p], axis=-1, keepdims=True) for p in parts]
    e2 = [jnp.min(jnp.where(el2[p] == v2[p], lane, big), axis=-1, keepdims=True) for p in parts]
    t2 = [jnp.exp(v2[p] - v1[p]) for p in parts]
    w1 = [g_w[p] * (1.0 / (1.0 + t2[p])) for p in parts]
    w2 = [g_w[p] * (t2[p] / (1.0 + t2[p])) for p in parts]

    oh1 = [(lane == e1[p]).astype(F32) for p in parts]
    oh2 = [(lane == e2[p]).astype(F32) for p in parts]
    both = [oh1[p] + oh2[p] for p in parts]
    strict = (lax.broadcasted_iota(jnp.int32, (tp, tp), 0)
              > lax.broadcasted_iota(jnp.int32, (tp, tp), 1)).astype(BF16)
    within = [jnp.dot(strict, both[p].astype(BF16), preferred_element_type=F32) for p in parts]
    before = carry[...]
    for p in parts:
        prefix = within[p] + before
        r1 = jnp.sum(oh1[p] * prefix, axis=-1, keepdims=True)
        r2 = jnp.sum(oh2[p] * prefix, axis=-1, keepdims=True)
        before = before + jnp.sum(both[p], axis=0, keepdims=True)
        route = jnp.where(lane_i == 0, e1[p], 0.0)
        route = jnp.where(lane_i == 1, e2[p], route)
        route = jnp.where(lane_i == 2, r1, route)
        route = jnp.where(lane_i == 3, r2, route)
        route = jnp.where(lane_i == 4, w1[p], route)
        route = jnp.where(lane_i == 5, w2[p], route)
        route_ref[rows[p], :] = route
        route_t_ref[:, rows[p]] = route.T[0:ROUTE_ROWS, :]
    carry[...] = before
    counts_ref[...] = before


def _outproj_router(hm, ha, x2d, wo, nw, wr_hi, wr_lo, br):
    T = x2d.shape[0]
    tm = min(TM_OUT, T)
    row = lambda i: (i, 0)
    cst = lambda i: (0, 0)
    return pl.pallas_call(
        _outproj_router_kernel,
        grid=(T // tm,),
        in_specs=[
            pl.BlockSpec((tm, MIX_HALF), row),
            pl.BlockSpec((tm, MIX_HALF), row),
            pl.BlockSpec((tm, D_MODEL), row),
            pl.BlockSpec((D_MODEL, D_MODEL), cst),
            pl.BlockSpec((1, D_MODEL), cst),
            pl.BlockSpec((D_MODEL, LANES), cst),
            pl.BlockSpec((D_MODEL, LANES), cst),
            pl.BlockSpec((1, LANES), cst),
        ],
        out_specs=[
            pl.BlockSpec((tm, D_MODEL), row),
            pl.BlockSpec((tm, D_MODEL), row),
            pl.BlockSpec((tm, LANES), row),
            pl.BlockSpec((ROUTE_ROWS, tm), lambda i: (0, i)),
            pl.BlockSpec((1, LANES), cst),
        ],
        out_shape=[
            jax.ShapeDtypeStruct((T, D_MODEL), F32),
            jax.ShapeDtypeStruct((T, D_MODEL), F32),
            jax.ShapeDtypeStruct((T, LANES), F32),
            jax.ShapeDtypeStruct((ROUTE_ROWS, T), F32),
            jax.ShapeDtypeStruct((1, LANES), F32),
        ],
        scratch_shapes=[pltpu.VMEM((1, LANES), F32)],
        compiler_params=_cparams(("arbitrary",)),
        name="outproj_router",
    )(hm, ha, x2d, wo, nw, wr_hi, wr_lo, br)


ROW_TILE = D_MODEL // LANES


def _to_tiles(dst_ref, x, n):
    for s in range(ROW_TILE):
        dst_ref[pl.ds(s, n, stride=ROW_TILE), :] = x[:, s * LANES:(s + 1) * LANES]


def _tile_piece(src_ref, s, n, base=0):
    return src_ref[pl.ds(base * ROW_TILE + s, n, stride=ROW_TILE), :]


def _row_dma(src, src_row, dst, dst_row, sem):
    def tile(row):
        start = row * ROW_TILE
        return pl.ds(start if isinstance(row, int) else pl.multiple_of(start, ROW_TILE), ROW_TILE)
    return pltpu.make_async_copy(src.at[tile(src_row)], dst.at[tile(dst_row)], sem)


def _dest_copy(dest_hbm, dsm, dsem, step):
    slot = lax.rem(step, 2)
    return pltpu.make_async_copy(dest_hbm.at[step], dsm.at[slot], dsem.at[slot])


def _scatter_kernel(pend_ref, padded_ref, dest_hbm, h2_ref, xbuf, dsm, dsem, stage, ssem, zbuf, zsem):
    i = pl.program_id(0)
    n = pl.num_programs(0)
    tm = h2_ref.shape[0] // 2
    blk = MOE_BLOCK

    @pl.when(i == 0)
    def _():
        zbuf[...] = jnp.zeros_like(zbuf)

        def zero_copy(e):
            row0 = pl.multiple_of((pend_ref[e] - blk) * ROW_TILE, blk * ROW_TILE)
            return pltpu.make_async_copy(zbuf, xbuf.at[pl.ds(row0, blk * ROW_TILE)], zsem)

        for e in range(N_EXPERTS):
            pl.when(padded_ref[e] > 0)(lambda e=e: zero_copy(e).start())
        for e in range(N_EXPERTS):
            pl.when(padded_ref[e] > 0)(lambda e=e: zero_copy(e).wait())

        def zero_tail(bi, _):
            row0 = pl.multiple_of(bi * (blk * ROW_TILE), blk * ROW_TILE)
            cp = pltpu.make_async_copy(zbuf, xbuf.at[pl.ds(row0, blk * ROW_TILE)], zsem)
            cp.start()
            cp.wait()
            return 0

        lax.fori_loop(pend_ref[N_EXPERTS - 1] // blk, xbuf.shape[0] // (blk * ROW_TILE), zero_tail, 0)
        _dest_copy(dest_hbm, dsm, dsem, i).start()

    _dest_copy(dest_hbm, dsm, dsem, i).wait()

    @pl.when(i + 1 < n)
    def _():
        _dest_copy(dest_hbm, dsm, dsem, i + 1).start()

    slot = lax.rem(i, 2)

    def drain(half):
        for _ in range(TOP_K):
            pltpu.make_async_copy(stage.at[half], xbuf.at[pl.ds(0, tm * ROW_TILE)], ssem.at[half]).wait()

    for half in range(2):
        pl.when(i > 0)(functools.partial(drain, half))
        _to_tiles(stage.at[half], h2_ref[half * tm:(half + 1) * tm, :], tm)
        for r in range(tm):
            for kk in range(TOP_K):
                d = dsm[slot, 0, kk * 2 * tm + half * tm + r]
                _row_dma(stage.at[half], r, xbuf, d, ssem.at[half]).start(priority=kk)

    @pl.when(i == n - 1)
    def _():
        for half in range(2):
            drain(half)


def _scatter_rows(pad_end, padded, dest3, h2, n_pos):
    T = h2.shape[0]
    n_steps = dest3.shape[0]
    tm2 = T // n_steps
    return pl.pallas_call(
        _scatter_kernel,
        grid_spec=pltpu.PrefetchScalarGridSpec(
            num_scalar_prefetch=2,
            grid=(n_steps,),
            in_specs=[
                pl.BlockSpec(memory_space=pl.ANY),
                pl.BlockSpec((tm2, D_MODEL), lambda i, pe, pd: (i, 0)),
            ],
            out_specs=pl.BlockSpec(memory_space=pl.ANY),
            scratch_shapes=[
                pltpu.SMEM((2, 1, dest3.shape[2]), jnp.int32),
                pltpu.SemaphoreType.DMA((2,)),
                pltpu.VMEM((2, (tm2 // 2) * ROW_TILE, LANES), F32),
                pltpu.SemaphoreType.DMA((2,)),
                pltpu.VMEM((MOE_BLOCK * ROW_TILE, LANES), F32),
                pltpu.SemaphoreType.DMA,
            ],
        ),
        out_shape=jax.ShapeDtypeStruct((n_pos * ROW_TILE, LANES), F32),
        compiler_params=_cparams(("arbitrary",)),
        name="scatter_rows",
    )(pad_end, padded, dest3, h2)


def _expert_kernel(be_ref, nv_ref, x_ref, wg_ref, wu_ref, wd_ref, y_ref, wg_s, wu_s, wd_s):
    b = pl.program_id(0)
    blk = MOE_BLOCK

    @pl.when((b == 0) | (be_ref[b] != be_ref[jnp.maximum(b - 1, 0)]))
    def _():
        wg_s[...] = wg_ref[0].astype(BF16)
        wu_s[...] = wu_ref[0].astype(BF16)
        wd_s[...] = wd_ref[0].astype(BF16)

    @pl.when(b < nv_ref[0])
    def _():
        parts = range(EXPERT_PARTS)
        tp = blk // EXPERT_PARTS
        x = [jnp.concatenate([_tile_piece(x_ref, s, tp, base=p * tp) for s in range(ROW_TILE)],
                             axis=1).astype(BF16) for p in parts]
        g = [jnp.dot(x[p], wg_s[...], preferred_element_type=F32) for p in parts]
        u = [jnp.dot(x[p], wu_s[...], preferred_element_type=F32) for p in parts]
        hid = [(g[p] * jax.nn.sigmoid(g[p]) * u[p]).astype(BF16) for p in parts]
        y = [jnp.dot(hid[p], wd_s[...], preferred_element_type=F32) for p in parts]
        for p in parts:
            _to_tiles(y_ref.at[pl.ds(p * tp * ROW_TILE, tp * ROW_TILE)], y[p], tp)

    @pl.when(b >= nv_ref[0])
    def _():
        y_ref[...] = jnp.zeros_like(y_ref)


def _experts(block_expert, n_valid, xbuf, wg, wu, wd):
    blk = MOE_BLOCK
    nb = xbuf.shape[0] // (blk * ROW_TILE)
    xmap = lambda b, be, nv: (jnp.minimum(b, nv[0] - 1), 0)
    wmap = lambda b, be, nv: (be[b], 0, 0)
    return pl.pallas_call(
        _expert_kernel,
        grid_spec=pltpu.PrefetchScalarGridSpec(
            num_scalar_prefetch=2,
            grid=(nb,),
            in_specs=[
                pl.BlockSpec((blk * ROW_TILE, LANES), xmap),
                pl.BlockSpec((1, D_MODEL, D_EXPERT), wmap),
                pl.BlockSpec((1, D_MODEL, D_EXPERT), wmap),
                pl.BlockSpec((1, D_EXPERT, D_MODEL), wmap),
            ],
            out_specs=pl.BlockSpec((blk * ROW_TILE, LANES), lambda b, be, nv: (b, 0)),
            scratch_shapes=[
                pltpu.VMEM((D_MODEL, D_EXPERT), BF16),
                pltpu.VMEM((D_MODEL, D_EXPERT), BF16),
                pltpu.VMEM((D_EXPERT, D_MODEL), BF16),
            ],
        ),
        out_shape=jax.ShapeDtypeStruct(xbuf.shape, F32),
        compiler_params=_cparams(("arbitrary",)),
        name="experts",
    )(block_expert, n_valid, xbuf, wg, wu, wd)


def _combine_kernel(dest_hbm, x1_ref, route_ref, ybuf, out_ref, dsm, dsem, buf, gsem):
    i = pl.program_id(0)
    n = pl.num_programs(0)
    tm = x1_ref.shape[0] // 2

    def issue_gather(step, half):
        slot = lax.rem(step, 2)
        for r in range(tm):
            for kk in range(TOP_K):
                d = dsm[slot, 0, kk * 2 * tm + half * tm + r]
                _row_dma(ybuf, d, buf.at[half], kk * tm + r, gsem.at[half]).start(priority=kk)

    def combine_tile(half):
        for _ in range(TOP_K):
            pltpu.make_async_copy(ybuf.at[pl.ds(0, tm * ROW_TILE)],
                                  buf.at[half, pl.ds(0, tm * ROW_TILE)], gsem.at[half]).wait()
        rows = slice(half * tm, (half + 1) * tm)
        w1 = route_ref[rows, 4:5]
        w2 = route_ref[rows, 5:6]
        for s in range(ROW_TILE):
            cols = slice(s * LANES, (s + 1) * LANES)
            out_ref[rows, cols] = x1_ref[rows, cols] + (_tile_piece(buf.at[half], s, tm) * w1
                                                        + _tile_piece(buf.at[half], s, tm, base=tm) * w2)

    @pl.when(i == 0)
    def _():
        _dest_copy(dest_hbm, dsm, dsem, i).start()
        _dest_copy(dest_hbm, dsm, dsem, i).wait()
        issue_gather(i, 0)

    @pl.when(i + 1 < n)
    def _():
        _dest_copy(dest_hbm, dsm, dsem, i + 1).start()

    issue_gather(i, 1)
    combine_tile(0)

    @pl.when(i + 1 < n)
    def _():
        _dest_copy(dest_hbm, dsm, dsem, i + 1).wait()
        issue_gather(i + 1, 0)

    combine_tile(1)


def _combine(dest3, x1, route, ybuf):
    T = x1.shape[0]
    n_steps = dest3.shape[0]
    tm2 = T // n_steps
    return pl.pallas_call(
        _combine_kernel,
        grid=(n_steps,),
        in_specs=[
            pl.BlockSpec(memory_space=pl.ANY),
            pl.BlockSpec((tm2, D_MODEL), lambda i: (i, 0)),
            pl.BlockSpec((tm2, LANES), lambda i: (i, 0)),
            pl.BlockSpec(memory_space=pl.ANY),
        ],
        out_specs=pl.BlockSpec((tm2, D_MODEL), lambda i: (i, 0)),
        out_shape=jax.ShapeDtypeStruct((T, D_MODEL), F32),
        scratch_shapes=[
            pltpu.SMEM((2, 1, dest3.shape[2]), jnp.int32),
            pltpu.SemaphoreType.DMA((2,)),
            pltpu.VMEM((2, TOP_K * (tm2 // 2) * ROW_TILE, LANES), F32),
            pltpu.SemaphoreType.DMA((2,)),
        ],
        compiler_params=_cparams(("arbitrary",)),
        name="combine",
    )(dest3, x1, route, ybuf)


def _layer(x2d, batch, seq, lambda_init, norm1_w, w_in, b_igate, b_fgate, conv_w, conv_b, m_norm_w,
           q_norm_w, k_norm_w, lq1, lk1, lq2, lk2, a_norm_w, w_out, norm2_w,
           w_group, b_group, w_expert, b_expert, w_gate, w_up, w_down):
    T = x2d.shape[0]
    row = lambda v: v.reshape(1, -1).astype(F32)
    g0 = 2 * MIX_HALF + 2 * MIX_HALF
    g1 = g0 + 2 * N_HEADS
    w_proj = jnp.concatenate(
        [w_in[:, :g0], w_in[:, g1:], w_in[:, g0:g1], jnp.zeros((D_MODEL, LANES - 2 * N_HEADS), F32)],
        axis=1).astype(BF16)
    gate_b = jnp.concatenate([b_igate, b_fgate, jnp.zeros((LANES - 2 * N_HEADS,), F32)]).reshape(1, LANES)

    qw = row(jnp.concatenate([q_norm_w, q_norm_w])) * (QK_DIM ** -0.5 * math.log2(math.e))
    kw = row(jnp.concatenate([k_norm_w, k_norm_w]))
    proj, gates, kn, qt, vt = _inproj(x2d, row(norm1_w), w_proj, conv_w, row(conv_b), qw, kw, seq)
    hm = _mlstm(proj, gates, gate_b, row(m_norm_w), batch, seq)
    ha = _diff_attn(qt, kn, vt, row(lq1), row(lk1), row(lq2), row(lk2), row(a_norm_w),
                    batch, seq, lambda_init)

    w_router = jnp.concatenate(
        [w_expert, w_group, jnp.zeros((D_MODEL, LANES - N_EXPERTS - N_GROUPS), F32)], axis=1)
    wr_hi = w_router.astype(BF16)
    wr_lo = (w_router - wr_hi.astype(F32)).astype(BF16)
    b_router = jnp.concatenate(
        [b_expert, b_group, jnp.zeros((LANES - N_EXPERTS - N_GROUPS,), F32)]).reshape(1, LANES)
    x1, h2, route, route_t, counts = _outproj_router(hm, ha, x2d, w_out.astype(BF16), row(norm2_w),
                                                     wr_hi, wr_lo, b_router)

    blk = MOE_BLOCK
    counts = counts[0, :N_EXPERTS].astype(jnp.int32)
    padded = ((counts + blk - 1) // blk) * blk
    pad_end = jnp.cumsum(padded)
    pad_start = pad_end - padded
    nb = (T * TOP_K) // blk + N_EXPERTS
    n_valid = (pad_end[-1] // blk).astype(jnp.int32)
    bstart = jnp.minimum(jnp.arange(nb, dtype=jnp.int32), n_valid - 1) * blk
    block_expert = jnp.minimum(
        jnp.sum(pad_end[None, :] <= bstart[:, None], axis=1), N_EXPERTS - 1).astype(jnp.int32)
    eid = route_t[0:TOP_K].astype(jnp.int32)
    rank = route_t[TOP_K:2 * TOP_K].astype(jnp.int32)
    experts = jnp.arange(N_EXPERTS, dtype=jnp.int32)[:, None, None]
    dest = rank + jnp.sum(jnp.where(eid[None] == experts, pad_start[:, None, None], 0), axis=0)
    tm2 = min(2 * TM_ROWS, T)
    dest3 = dest.reshape(TOP_K, T // tm2, tm2).transpose(1, 0, 2).reshape(T // tm2, 1, TOP_K * tm2)

    xbuf = _scatter_rows(pad_end.astype(jnp.int32), padded - counts, dest3, h2, nb * blk)
    ybuf = _experts(block_expert, n_valid.reshape(1), xbuf, w_gate, w_up, w_down)
    return _combine(dest3, x1, route, ybuf)


def kernel(x, norm1_w, w_in, b_igate, b_fgate, conv_w, conv_b, m_norm_w, q_norm_w, k_norm_w,
           lambda_q1, lambda_k1, lambda_q2, lambda_k2, a_norm_w, w_out, norm2_w,
           w_group, b_group, w_expert, b_expert, w_gate, w_up, w_down):
    batch, seq, d = x.shape
    x2d = x.reshape(batch * seq, d)
    for l in range(norm1_w.shape[0]):
        lambda_init = 0.8 - 0.6 * math.exp(-0.3 * l)
        x2d = _layer(x2d, batch, seq, lambda_init, norm1_w[l], w_in[l], b_igate[l], b_fgate[l],
                     conv_w[l], conv_b[l], m_norm_w[l], q_norm_w[l], k_norm_w[l],
                     lambda_q1[l], lambda_k1[l], lambda_q2[l], lambda_k2[l], a_norm_w[l],
                     w_out[l], norm2_w[l], w_group[l], b_group[l], w_expert[l], b_expert[l],
                     w_gate[l], w_up[l], w_down[l])
    return x2d.reshape(batch, seq, d)
```

```python
import functools
import math

import jax
import jax.numpy as jnp
from jax import lax
from jax.experimental import pallas as pl
from jax.experimental.pallas import tpu as pltpu

F32 = jnp.float32
BF16 = jnp.bfloat16
EPS = 1e-6

D_MODEL = 1024
HEAD_DIM = 128
N_HEADS = 4
MIX_HALF = N_HEADS * HEAD_DIM
QK_DIM = 64
CONV_WIDTH = 4
ATTN_CHUNK = 64
SUM_ROWS = 16
N_GROUPS = 4
EXPERTS_PER_GROUP = 8
N_EXPERTS = N_GROUPS * EXPERTS_PER_GROUP
TOP_K = 2
D_EXPERT = D_MODEL // 2

LANES = 128
PROJ_COLS = 7 * MIX_HALF
VMEM_LIMIT = 48 * 1024 * 1024

TM_PROJ = 1024
MLSTM_CHUNK = 128
TQ_ATTN = 256
TM_OUT = 1024
ROUTE_ROWS = 8
OUT_PARTS = 8
EXPERT_PARTS = 2
TM_ROWS = 256
MOE_BLOCK = 512


def _cparams(sem):
    return pltpu.CompilerParams(dimension_semantics=sem, vmem_limit_bytes=VMEM_LIMIT)


CONV_HALO = 8


def _half_norm(x, w):
    lo = lax.broadcasted_iota(jnp.int32, x.shape, 1) < QK_DIM
    x2 = x * x
    s_lo = jnp.sum(jnp.where(lo, x2, 0.0), axis=-1, keepdims=True)
    s_hi = jnp.sum(jnp.where(lo, 0.0, x2), axis=-1, keepdims=True)
    ms = jnp.where(lo, s_lo, s_hi) * (1.0 / QK_DIM)
    return x * lax.rsqrt(ms + EPS) * w


def _inproj_kernel(x_ref, nw_ref, w_ref, cw_ref, cb_ref, qw_ref, kw_ref,
                   proj_ref, gate_ref, kn_ref, qt_ref, vt_ref, tail, *, tiles_per_seq):
    tm = x_ref.shape[0]

    @pl.when(lax.rem(pl.program_id(0), tiles_per_seq) == 0)
    def _():
        tail[...] = jnp.zeros_like(tail)

    x = x_ref[...]
    ms = jnp.mean(x * x, axis=-1, keepdims=True)
    h = (x * lax.rsqrt(ms + EPS) * nw_ref[...]).astype(BF16)
    nb = MIX_HALF
    head = lambda hd: slice(hd * HEAD_DIM, (hd + 1) * HEAD_DIM)
    vrows = HEAD_DIM + SUM_ROWS
    for j in range(PROJ_COLS // nb):
        cols = slice(j * nb, (j + 1) * nb)
        pre = jnp.dot(h, w_ref[:, cols], preferred_element_type=F32)
        if j < 2:
            win = jnp.concatenate([tail[:, cols], pre], axis=0)
            acc = cb_ref[:, cols]
            for t in range(CONV_WIDTH):
                sh = CONV_WIDTH - 1 - t
                xs = win if sh == 0 else pltpu.roll(win, sh, axis=0)
                acc = acc + xs[CONV_HALO:, :] * cw_ref[t:t + 1, cols]
            tail[:, cols] = pre[tm - CONV_HALO:, :]
            pre = acc * jax.nn.sigmoid(acc)
            if j == 1:
                pre = pre * (HEAD_DIM ** -0.5)
            proj_ref[:, cols] = pre.astype(BF16)
        elif j < 4:
            proj_ref[:, cols] = pre.astype(BF16)
        elif j == 4:
            for hd in range(N_HEADS):
                qt_ref[head(hd), :] = _half_norm(pre[:, head(hd)], qw_ref[...]).T.astype(BF16)
        elif j == 5:
            for hd in range(N_HEADS):
                kn_ref[:, head(hd)] = _half_norm(pre[:, head(hd)], kw_ref[...]).astype(BF16)
        else:
            for hd in range(N_HEADS):
                vt_ref[hd * vrows:hd * vrows + HEAD_DIM, :] = pre[:, head(hd)].T.astype(BF16)
                vt_ref[hd * vrows + HEAD_DIM:(hd + 1) * vrows, :] = jnp.ones((SUM_ROWS, tm), BF16)
    gate_ref[...] = jnp.dot(h, w_ref[:, PROJ_COLS:], preferred_element_type=F32)


def _inproj(x2d, nw, w, conv_w, conv_b, qw, kw, seq):
    T = x2d.shape[0]
    tm = min(TM_PROJ, seq)
    cst = lambda i: (0, 0)
    row = lambda i: (i, 0)
    col = lambda i: (0, i)
    vrows = N_HEADS * (HEAD_DIM + SUM_ROWS)
    return pl.pallas_call(
        functools.partial(_inproj_kernel, tiles_per_seq=seq // tm),
        grid=(T // tm,),
        in_specs=[
            pl.BlockSpec((tm, D_MODEL), row),
            pl.BlockSpec((1, D_MODEL), cst),
            pl.BlockSpec((D_MODEL, PROJ_COLS + LANES), cst),
            pl.BlockSpec((CONV_WIDTH, 2 * MIX_HALF), cst),
            pl.BlockSpec((1, 2 * MIX_HALF), cst),
            pl.BlockSpec((1, HEAD_DIM), cst),
            pl.BlockSpec((1, HEAD_DIM), cst),
        ],
        out_specs=[
            pl.BlockSpec((tm, 4 * MIX_HALF), row),
            pl.BlockSpec((tm, LANES), row),
            pl.BlockSpec((tm, MIX_HALF), row),
            pl.BlockSpec((MIX_HALF, tm), col),
            pl.BlockSpec((vrows, tm), col),
        ],
        out_shape=[
            jax.ShapeDtypeStruct((T, 4 * MIX_HALF), BF16),
            jax.ShapeDtypeStruct((T, LANES), F32),
            jax.ShapeDtypeStruct((T, MIX_HALF), BF16),
            jax.ShapeDtypeStruct((MIX_HALF, T), BF16),
            jax.ShapeDtypeStruct((vrows, T), BF16),
        ],
        scratch_shapes=[pltpu.VMEM((CONV_HALO, 2 * MIX_HALF), F32)],
        compiler_params=_cparams(("arbitrary",)),
        name="inproj",
    )(x2d, nw, w, conv_w, conv_b, qw, kw)


def _log_sigmoid(x):
    return jnp.minimum(x, 0.0) - jnp.log(1.0 + jnp.exp(-jnp.abs(x)))


def _cumsum_lanes(x):
    n = x.shape[1]
    lane = lax.broadcasted_iota(jnp.int32, x.shape, 1)
    s = 1
    while s < n:
        x = x + jnp.where(lane >= s, pltpu.roll(x, s, axis=1), 0.0)
        s *= 2
    return x


def _mlstm_kernel(p_ref, g_ref, gb_ref, nw_ref, out_ref, gt_s, bt_s, bc_s, *c_s, seq, chunk):
    for c_h in c_s:
        c_h[...] = jnp.zeros_like(c_h)
    src_le_out = (lax.broadcasted_iota(jnp.int32, (chunk, chunk), 0)
                  <= lax.broadcasted_iota(jnp.int32, (chunk, chunk), 1))
    gate_rows = 2 * N_HEADS
    nt = (((1,), (1,)), ((), ()))
    split_row = lax.broadcasted_iota(jnp.int32, (16, HEAD_DIM), 0)

    n_chunks = seq // chunk
    for c in range(n_chunks):
        g = g_ref[c * chunk:(c + 1) * chunk, :] + gb_ref[...]
        gt_s[c * gate_rows:(c + 1) * gate_rows, :] = g.T[0:gate_rows, :]
    bt_s[...] = _cumsum_lanes(_log_sigmoid(gt_s[...]))
    for c in range(n_chunks):
        b_t = bt_s[c * gate_rows:(c + 1) * gate_rows, :]
        bc_s[c * chunk:(c + 1) * chunk, :] = pltpu.roll(
            jnp.concatenate([b_t, jnp.zeros((LANES - gate_rows, chunk), F32)], axis=0).T,
            LANES - N_HEADS, axis=1)

    def step(c, carry):
        n_all, m_row = carry
        r0 = pl.multiple_of(c * chunk, chunk)
        rows = pl.ds(r0, chunk)
        g = g_ref[rows, :] + gb_ref[...]
        b_t = bt_s[pl.ds(pl.multiple_of(c * gate_rows, gate_rows), gate_rows), :]
        b_c = bc_s[rows, :]
        a_all = g - b_c
        b_end = b_c[chunk - 1:chunk, :]
        log_w = b_end + a_all
        m_new = jnp.maximum(b_end + m_row, jnp.max(log_w, axis=0, keepdims=True))
        w_all = jnp.exp(log_w - m_new)
        decay_row = jnp.exp(b_end + m_row - m_new)

        heads = range(N_HEADS)
        col = lambda blk, h: slice(blk * MIX_HALF + h * HEAD_DIM, blk * MIX_HALF + (h + 1) * HEAD_DIM)
        q = [p_ref[rows, col(0, h)] for h in heads]
        k = [p_ref[rows, col(1, h)] for h in heads]
        ct = [c_s[h][...] for h in heads]
        vt = [p_ref[rows, col(2, h)].astype(F32).T.astype(BF16) for h in heads]
        b_row = [b_t[N_HEADS + h:N_HEADS + h + 1, :] for h in heads]
        log_d = [jnp.where(src_le_out, b_row[h] + a_all[:, h:h + 1], -jnp.inf) for h in heads]
        log_inter = [b_row[h] + m_row[:, h:h + 1] for h in heads]
        m_t = [jnp.maximum(log_inter[h], jnp.max(log_d[h], axis=0, keepdims=True)) for h in heads]
        inter = [jnp.exp(log_inter[h] - m_t[h]) for h in heads]
        st = [lax.dot_general(k[h], q[h], nt, preferred_element_type=F32) * jnp.exp(log_d[h] - m_t[h])
              for h in heads]
        cq = [lax.dot_general(ct[h].astype(BF16), q[h], nt, preferred_element_type=F32) for h in heads]
        nq = []
        for h in heads:
            n_hi = n_all[h].astype(BF16).astype(F32)
            n_mat = jnp.where(split_row == 0, n_hi,
                              jnp.where(split_row == 1, n_all[h] - n_hi, 0.0)).astype(BF16)
            nq.append(lax.dot_general(n_mat, q[h], nt, preferred_element_type=F32))
        num = [jnp.dot(vt[h], st[h].astype(BF16), preferred_element_type=F32) + inter[h] * cq[h]
               for h in heads]
        den = [jnp.sum(st[h], axis=0, keepdims=True) + inter[h] * (nq[h][0:1, :] + nq[h][1:2, :])
               for h in heads]
        hh = [(num[h] / jnp.maximum(jnp.abs(den[h]), jnp.exp(-m_t[h]))).T for h in heads]

        kw = [k[h].astype(F32) * w_all[:, h:h + 1] for h in heads]
        n_new = []
        for h in heads:
            decay = decay_row[:, h:h + 1]
            c_s[h][...] = decay * ct[h] + jnp.dot(vt[h], kw[h].astype(BF16), preferred_element_type=F32)
            n_new.append(decay * n_all[h] + jnp.sum(kw[h], axis=0, keepdims=True))
        for h in heads:
            hm = jax.nn.sigmoid(p_ref[rows, col(3, h)].astype(F32)) * hh[h]
            y = hm * lax.rsqrt(jnp.mean(hm * hm, axis=-1, keepdims=True) + EPS) * nw_ref[...]
            out_ref[rows, col(0, h)] = y.astype(BF16)
        return tuple(n_new), m_new

    init = (tuple(jnp.zeros((1, HEAD_DIM), F32) for _ in range(N_HEADS)), jnp.zeros((1, LANES), F32))
    lax.fori_loop(0, seq // chunk, step, init)


def _mlstm(proj, gates, gate_b, m_norm_w, batch, seq):
    T = batch * seq
    chunk = MLSTM_CHUNK
    assert chunk == LANES and seq % chunk == 0
    cst = lambda b: (0, 0)
    return pl.pallas_call(
        functools.partial(_mlstm_kernel, seq=seq, chunk=chunk),
        grid=(batch,),
        in_specs=[
            pl.BlockSpec((seq, 4 * MIX_HALF), lambda b: (b, 0)),
            pl.BlockSpec((seq, LANES), lambda b: (b, 0)),
            pl.BlockSpec((1, LANES), cst),
            pl.BlockSpec((1, HEAD_DIM), cst),
        ],
        out_specs=pl.BlockSpec((seq, MIX_HALF), lambda b: (b, 0)),
        out_shape=jax.ShapeDtypeStruct((T, MIX_HALF), BF16),
        scratch_shapes=[
            pltpu.VMEM((seq // chunk * 2 * N_HEADS, LANES), F32),
            pltpu.VMEM((seq // chunk * 2 * N_HEADS, LANES), F32),
            pltpu.VMEM((seq, LANES), F32),
        ] + [pltpu.VMEM((HEAD_DIM, HEAD_DIM), F32) for _ in range(N_HEADS)],
        compiler_params=_cparams(("arbitrary",)),
        name="mlstm",
    )(proj, gates, gate_b, m_norm_w)


def _attn_kernel(qt_ref, kn_ref, vt_ref, lq1_ref, lk1_ref, lq2_ref, lk2_ref, anw_ref, out_ref,
                 *, seq, tq, heads, lambda_init):
    nq = seq // tq
    hcol = lambda hd: slice(hd * HEAD_DIM, (hd + 1) * HEAD_DIM)
    vrows = HEAD_DIM + SUM_ROWS

    lam = (jnp.exp(jnp.sum(lq1_ref[...] * lk1_ref[...], axis=-1, keepdims=True))
           - jnp.exp(jnp.sum(lq2_ref[...] * lk2_ref[...], axis=-1, keepdims=True)) + lambda_init)
    first_map = lax.broadcasted_iota(jnp.int32, (HEAD_DIM, tq), 0) < QK_DIM
    q_in_tile = lax.rem(lax.broadcasted_iota(jnp.int32, (tq, 2 * tq), 1), tq)
    diag_ok = lax.broadcasted_iota(jnp.int32, (tq, 2 * tq), 0) // ATTN_CHUNK <= q_in_tile // ATTN_CHUNK

    def v_tile(hd, j):
        return vt_ref[hd * vrows:(hd + 1) * vrows, j * tq:(j + 1) * tq]

    def scores(hd, qi):
        r0 = qi * tq
        qt = qt_ref[hcol(hd), r0:r0 + tq]
        zero = jnp.zeros_like(qt)
        qc = jnp.concatenate([jnp.where(first_map, qt, zero), jnp.where(first_map, zero, qt)], axis=1)
        s_d = jnp.where(diag_ok, jnp.dot(kn_ref[r0:r0 + tq, hcol(hd)], qc, preferred_element_type=F32),
                        -jnp.inf)
        m = jnp.max(s_d, axis=0, keepdims=True)
        s_f = None
        if qi > 0:
            s_f = jnp.dot(kn_ref[0:r0, hcol(hd)], qc, preferred_element_type=F32)
            m = jnp.maximum(m, jnp.max(s_f, axis=0, keepdims=True))
        return s_d, s_f, m

    def weights(sc):
        s_d, s_f, m = sc
        p_d = jnp.exp2(s_d - m).astype(BF16)
        p_f = None if s_f is None else jnp.exp2(s_f - m).astype(BF16)
        return p_d, p_f

    def output(hd, qi, pw):
        p_d, p_f = pw
        r0 = qi * tq
        acc = jnp.dot(v_tile(hd, qi), p_d, preferred_element_type=F32)
        for j in range(qi):
            acc = acc + jnp.dot(v_tile(hd, j), p_f[j * tq:(j + 1) * tq], preferred_element_type=F32)
        on = acc[0:HEAD_DIM] * (1.0 / acc[HEAD_DIM:HEAD_DIM + 1])
        o = (on[:, 0:tq] - lam * on[:, tq:]).T
        y = o * lax.rsqrt(jnp.mean(o * o, axis=-1, keepdims=True) + EPS) * anw_ref[...] * (1.0 - lambda_init)
        out_ref[r0:r0 + tq, hcol(hd)] = y.astype(BF16)

    hs = range(heads)
    sc = {(hd, 0): scores(hd, 0) for hd in hs}
    pw = {}
    for step in range(nq + 2):
        for hd in hs:
            if step + 1 < nq:
                sc[hd, step + 1] = scores(hd, step + 1)
        for hd in hs:
            if 1 <= step <= nq:
                output(hd, step - 1, pw.pop((hd, step - 1)))
        for hd in hs:
            if step < nq:
                pw[hd, step] = weights(sc.pop((hd, step)))


ATTN_HEADS_PER_STEP = 2


def _diff_attn(qt, kn, vt, lq1, lk1, lq2, lk2, a_norm_w, batch, seq, lambda_init):
    T = batch * seq
    tq = min(TQ_ATTN, seq)
    hps = ATTN_HEADS_PER_STEP
    width = hps * HEAD_DIM
    cst = lambda b, g: (0, 0)
    return pl.pallas_call(
        functools.partial(_attn_kernel, seq=seq, tq=tq, heads=hps, lambda_init=lambda_init),
        grid=(batch, N_HEADS // hps),
        in_specs=[
            pl.BlockSpec((width, seq), lambda b, g: (g, b)),
            pl.BlockSpec((seq, width), lambda b, g: (b, g)),
            pl.BlockSpec((hps * (HEAD_DIM + SUM_ROWS), seq), lambda b, g: (g, b)),
            pl.BlockSpec((1, QK_DIM), cst),
            pl.BlockSpec((1, QK_DIM), cst),
            pl.BlockSpec((1, QK_DIM), cst),
            pl.BlockSpec((1, QK_DIM), cst),
            pl.BlockSpec((1, HEAD_DIM), cst),
        ],
        out_specs=pl.BlockSpec((seq, width), lambda b, g: (b, g)),
        out_shape=jax.ShapeDtypeStruct((T, MIX_HALF), BF16),
        compiler_params=_cparams(("arbitrary", "arbitrary")),
        name="diffattn",
    )(qt, kn, vt, lq1, lk1, lq2, lk2, a_norm_w)


def _outproj_router_kernel(hm_ref, ha_ref, x_ref, wo_ref, nw_ref, wr_hi_ref, wr_lo_ref, br_ref,
                           x1_ref, h2_ref, route_ref, route_t_ref, counts_ref, carry):
    i = pl.program_id(0)
    tm = x_ref.shape[0]
    tp = tm // OUT_PARTS
    parts = range(OUT_PARTS)
    rows = [slice(p * tp, (p + 1) * tp) for p in parts]

    @pl.when(i == 0)
    def _():
        carry[...] = jnp.zeros_like(carry)

    mix = [jnp.dot(hm_ref[rows[p], :], wo_ref[0:MIX_HALF, :], preferred_element_type=F32)
           + jnp.dot(ha_ref[rows[p], :], wo_ref[MIX_HALF:, :], preferred_element_type=F32) for p in parts]
    x1 = [x_ref[rows[p], :] + mix[p] for p in parts]
    h2 = [x1[p] * lax.rsqrt(jnp.mean(x1[p] * x1[p], axis=-1, keepdims=True) + EPS) * nw_ref[...]
          for p in parts]
    for p in parts:
        x1_ref[rows[p], :] = x1[p]
        h2_ref[rows[p], :] = h2[p]

    h_hi = [h2[p].astype(BF16) for p in parts]
    h_lo = [(h2[p] - h_hi[p].astype(F32)).astype(BF16) for p in parts]
    logits = [(jnp.dot(h_hi[p], wr_hi_ref[...], preferred_element_type=F32)
               + jnp.dot(h_lo[p], wr_hi_ref[...], preferred_element_type=F32)
               + jnp.dot(h_hi[p], wr_lo_ref[...], preferred_element_type=F32)) + br_ref[...] for p in parts]

    lane_i = lax.broadcasted_iota(jnp.int32, (tp, LANES), 1)
    lane = lane_i.astype(F32)
    big = float(LANES)
    gmask = (lane_i >= N_EXPERTS) & (lane_i < N_EXPERTS + N_GROUPS)
    gl = [jnp.where(gmask, logits[p], -jnp.inf) for p in parts]
    gmax = [jnp.max(gl[p], axis=-1, keepdims=True) for p in parts]
    g_sel = [jnp.min(jnp.where(gl[p] == gmax[p], lane, big), axis=-1, keepdims=True) - float(N_EXPERTS)
             for p in parts]
    g_w = [1.0 / jnp.sum(jnp.where(gmask, jnp.exp(logits[p] - gmax[p]), 0.0), axis=-1, keepdims=True)
           for p in parts]
    e_lo = [g_sel[p] * float(EXPERTS_PER_GROUP) for p in parts]
    el = [jnp.where((lane >= e_lo[p]) & (lane < e_lo[p] + float(EXPERTS_PER_GROUP)), logits[p], -jnp.inf)
          for p in parts]
    v1 = [jnp.max(el[p], axis=-1, keepdims=True) for p in parts]
    e1 = [jnp.min(jnp.where(el[p] == v1[p], lane, big), axis=-1, keepdims=True) for p in parts]
    el2 = [jnp.where(lane == e1[p], -jnp.inf, el[p]) for p in parts]
    v2 = [jnp.max(el2[p], axis=-1, keepdims=True) for p in parts]
    e2 = [jnp.min(jnp.where(el2[p] == v2[p], lane, big), axis=-1, keepdims=True) for p in parts]
    t2 = [jnp.exp(v2[p] - v1[p]) for p in parts]
    w1 = [g_w[p] * (1.0 / (1.0 + t2[p])) for p in parts]
    w2 = [g_w[p] * (t2[p] / (1.0 + t2[p])) for p in parts]

    oh1 = [(lane == e1[p]).astype(F32) for p in parts]
    oh2 = [(lane == e2[p]).astype(F32) for p in parts]
    both = [oh1[p] + oh2[p] for p in parts]
    strict = (lax.broadcasted_iota(jnp.int32, (tp, tp), 0)
              > lax.broadcasted_iota(jnp.int32, (tp, tp), 1)).astype(BF16)
    within = [jnp.dot(strict, both[p].astype(BF16), preferred_element_type=F32) for p in parts]
    before = carry[...]
    for p in parts:
        prefix = within[p] + before
        r1 = jnp.sum(oh1[p] * prefix, axis=-1, keepdims=True)
        r2 = jnp.sum(oh2[p] * prefix, axis=-1, keepdims=True)
        before = before + jnp.sum(both[p], axis=0, keepdims=True)
        route = jnp.where(lane_i == 0, e1[p], 0.0)
        route = jnp.where(lane_i == 1, e2[p], route)
        route = jnp.where(lane_i == 2, r1, route)
        route = jnp.where(lane_i == 3, r2, route)
        route = jnp.where(lane_i == 4, w1[p], route)
        route = jnp.where(lane_i == 5, w2[p], route)
        route_ref[rows[p], :] = route
        route_t_ref[:, rows[p]] = route.T[0:ROUTE_ROWS, :]
    carry[...] = before
    counts_ref[...] = before


def _outproj_router(hm, ha, x2d, wo, nw, wr_hi, wr_lo, br):
    T = x2d.shape[0]
    tm = min(TM_OUT, T)
    row = lambda i: (i, 0)
    cst = lambda i: (0, 0)
    return pl.pallas_call(
        _outproj_router_kernel,
        grid=(T // tm,),
        in_specs=[
            pl.BlockSpec((tm, MIX_HALF), row),
            pl.BlockSpec((tm, MIX_HALF), row),
            pl.BlockSpec((tm, D_MODEL), row),
            pl.BlockSpec((D_MODEL, D_MODEL), cst),
            pl.BlockSpec((1, D_MODEL), cst),
            pl.BlockSpec((D_MODEL, LANES), cst),
            pl.BlockSpec((D_MODEL, LANES), cst),
            pl.BlockSpec((1, LANES), cst),
        ],
        out_specs=[
            pl.BlockSpec((tm, D_MODEL), row),
            pl.BlockSpec((tm, D_MODEL), row),
            pl.BlockSpec((tm, LANES), row),
            pl.BlockSpec((ROUTE_ROWS, tm), lambda i: (0, i)),
            pl.BlockSpec((1, LANES), cst),
        ],
        out_shape=[
            jax.ShapeDtypeStruct((T, D_MODEL), F32),
            jax.ShapeDtypeStruct((T, D_MODEL), F32),
            jax.ShapeDtypeStruct((T, LANES), F32),
            jax.ShapeDtypeStruct((ROUTE_ROWS, T), F32),
            jax.ShapeDtypeStruct((1, LANES), F32),
        ],
        scratch_shapes=[pltpu.VMEM((1, LANES), F32)],
        compiler_params=_cparams(("arbitrary",)),
        name="outproj_router",
    )(hm, ha, x2d, wo, nw, wr_hi, wr_lo, br)


ROW_TILE = D_MODEL // LANES


def _to_tiles(dst_ref, x, n):
    for s in range(ROW_TILE):
        dst_ref[pl.ds(s, n, stride=ROW_TILE), :] = x[:, s * LANES:(s + 1) * LANES]


def _tile_piece(src_ref, s, n, base=0):
    return src_ref[pl.ds(base * ROW_TILE + s, n, stride=ROW_TILE), :]


def _row_dma(src, src_row, dst, dst_row, sem):
    def tile(row):
        start = row * ROW_TILE
        return pl.ds(start if isinstance(row, int) else pl.multiple_of(start, ROW_TILE), ROW_TILE)
    return pltpu.make_async_copy(src.at[tile(src_row)], dst.at[tile(dst_row)], sem)


def _dest_copy(dest_hbm, dsm, dsem, step):
    slot = lax.rem(step, 2)
    return pltpu.make_async_copy(dest_hbm.at[step], dsm.at[slot], dsem.at[slot])


def _scatter_kernel(pend_ref, padded_ref, dest_hbm, h2_ref, xbuf, dsm, dsem, stage, ssem, zbuf, zsem):
    i = pl.program_id(0)
    n = pl.num_programs(0)
    tm = h2_ref.shape[0] // 2
    blk = MOE_BLOCK

    @pl.when(i == 0)
    def _():
        zbuf[...] = jnp.zeros_like(zbuf)

        def zero_copy(e):
            row0 = pl.multiple_of((pend_ref[e] - blk) * ROW_TILE, blk * ROW_TILE)
            return pltpu.make_async_copy(zbuf, xbuf.at[pl.ds(row0, blk * ROW_TILE)], zsem)

        for e in range(N_EXPERTS):
            pl.when(padded_ref[e] > 0)(lambda e=e: zero_copy(e).start())
        for e in range(N_EXPERTS):
            pl.when(padded_ref[e] > 0)(lambda e=e: zero_copy(e).wait())

        def zero_tail(bi, _):
            row0 = pl.multiple_of(bi * (blk * ROW_TILE), blk * ROW_TILE)
            cp = pltpu.make_async_copy(zbuf, xbuf.at[pl.ds(row0, blk * ROW_TILE)], zsem)
            cp.start()
            cp.wait()
            return 0

        lax.fori_loop(pend_ref[N_EXPERTS - 1] // blk, xbuf.shape[0] // (blk * ROW_TILE), zero_tail, 0)
        _dest_copy(dest_hbm, dsm, dsem, i).start()

    _dest_copy(dest_hbm, dsm, dsem, i).wait()

    @pl.when(i + 1 < n)
    def _():
        _dest_copy(dest_hbm, dsm, dsem, i + 1).start()

    slot = lax.rem(i, 2)

    def drain(half):
        for _ in range(TOP_K):
            pltpu.make_async_copy(stage.at[half], xbuf.at[pl.ds(0, tm * ROW_TILE)], ssem.at[half]).wait()

    for half in range(2):
        pl.when(i > 0)(functools.partial(drain, half))
        _to_tiles(stage.at[half], h2_ref[half * tm:(half + 1) * tm, :], tm)
        for r in range(tm):
            for kk in range(TOP_K):
                d = dsm[slot, 0, kk * 2 * tm + half * tm + r]
                _row_dma(stage.at[half], r, xbuf, d, ssem.at[half]).start(priority=kk)

    @pl.when(i == n - 1)
    def _():
        for half in range(2):
            drain(half)


def _scatter_rows(pad_end, padded, dest3, h2, n_pos):
    T = h2.shape[0]
    n_steps = dest3.shape[0]
    tm2 = T // n_steps
    return pl.pallas_call(
        _scatter_kernel,
        grid_spec=pltpu.PrefetchScalarGridSpec(
            num_scalar_prefetch=2,
            grid=(n_steps,),
            in_specs=[
                pl.BlockSpec(memory_space=pl.ANY),
                pl.BlockSpec((tm2, D_MODEL), lambda i, pe, pd: (i, 0)),
            ],
            out_specs=pl.BlockSpec(memory_space=pl.ANY),
            scratch_shapes=[
                pltpu.SMEM((2, 1, dest3.shape[2]), jnp.int32),
                pltpu.SemaphoreType.DMA((2,)),
                pltpu.VMEM((2, (tm2 // 2) * ROW_TILE, LANES), F32),
                pltpu.SemaphoreType.DMA((2,)),
                pltpu.VMEM((MOE_BLOCK * ROW_TILE, LANES), F32),
                pltpu.SemaphoreType.DMA,
            ],
        ),
        out_shape=jax.ShapeDtypeStruct((n_pos * ROW_TILE, LANES), F32),
        compiler_params=_cparams(("arbitrary",)),
        name="scatter_rows",
    )(pad_end, padded, dest3, h2)


def _expert_kernel(be_ref, nv_ref, x_ref, wg_ref, wu_ref, wd_ref, y_ref, wg_s, wu_s, wd_s):
    b = pl.program_id(0)
    blk = MOE_BLOCK

    @pl.when((b == 0) | (be_ref[b] != be_ref[jnp.maximum(b - 1, 0)]))
    def _():
        wg_s[...] = wg_ref[0].astype(BF16)
        wu_s[...] = wu_ref[0].astype(BF16)
        wd_s[...] = wd_ref[0].astype(BF16)

    @pl.when(b < nv_ref[0])
    def _():
        parts = range(EXPERT_PARTS)
        tp = blk // EXPERT_PARTS
        x = [jnp.concatenate([_tile_piece(x_ref, s, tp, base=p * tp) for s in range(ROW_TILE)],
                             axis=1).astype(BF16) for p in parts]
        g = [jnp.dot(x[p], wg_s[...], preferred_element_type=F32) for p in parts]
        u = [jnp.dot(x[p], wu_s[...], preferred_element_type=F32) for p in parts]
        hid = [(g[p] * jax.nn.sigmoid(g[p]) * u[p]).astype(BF16) for p in parts]
        y = [jnp.dot(hid[p], wd_s[...], preferred_element_type=F32) for p in parts]
        for p in parts:
            _to_tiles(y_ref.at[pl.ds(p * tp * ROW_TILE, tp * ROW_TILE)], y[p], tp)

    @pl.when(b >= nv_ref[0])
    def _():
        y_ref[...] = jnp.zeros_like(y_ref)


def _experts(block_expert, n_valid, xbuf, wg, wu, wd):
    blk = MOE_BLOCK
    nb = xbuf.shape[0] // (blk * ROW_TILE)
    xmap = lambda b, be, nv: (jnp.minimum(b, nv[0] - 1), 0)
    wmap = lambda b, be, nv: (be[b], 0, 0)
    return pl.pallas_call(
        _expert_kernel,
        grid_spec=pltpu.PrefetchScalarGridSpec(
            num_scalar_prefetch=2,
            grid=(nb,),
            in_specs=[
                pl.BlockSpec((blk * ROW_TILE, LANES), xmap),
                pl.BlockSpec((1, D_MODEL, D_EXPERT), wmap),
                pl.BlockSpec((1, D_MODEL, D_EXPERT), wmap),
                pl.BlockSpec((1, D_EXPERT, D_MODEL), wmap),
            ],
            out_specs=pl.BlockSpec((blk * ROW_TILE, LANES), lambda b, be, nv: (b, 0)),
            scratch_shapes=[
                pltpu.VMEM((D_MODEL, D_EXPERT), BF16),
                pltpu.VMEM((D_MODEL, D_EXPERT), BF16),
                pltpu.VMEM((D_EXPERT, D_MODEL), BF16),
            ],
        ),
        out_shape=jax.ShapeDtypeStruct(xbuf.shape, F32),
        compiler_params=_cparams(("arbitrary",)),
        name="experts",
    )(block_expert, n_valid, xbuf, wg, wu, wd)


def _combine_kernel(dest_hbm, x1_ref, route_ref, ybuf, out_ref, dsm, dsem, buf, gsem):
    i = pl.program_id(0)
    n = pl.num_programs(0)
    tm = x1_ref.shape[0] // 2

    def issue_gather(step, half):
        slot = lax.rem(step, 2)
        for r in range(tm):
            for kk in range(TOP_K):
                d = dsm[slot, 0, kk * 2 * tm + half * tm + r]
                _row_dma(ybuf, d, buf.at[half], kk * tm + r, gsem.at[half]).start(priority=kk)

    def combine_tile(half):
        for _ in range(TOP_K):
            pltpu.make_async_copy(ybuf.at[pl.ds(0, tm * ROW_TILE)],
                                  buf.at[half, pl.ds(0, tm * ROW_TILE)], gsem.at[half]).wait()
        rows = slice(half * tm, (half + 1) * tm)
        w1 = route_ref[rows, 4:5]
        w2 = route_ref[rows, 5:6]
        for s in range(ROW_TILE):
            cols = slice(s * LANES, (s + 1) * LANES)
            out_ref[rows, cols] = x1_ref[rows, cols] + (_tile_piece(buf.at[half], s, tm) * w1
                                                        + _tile_piece(buf.at[half], s, tm, base=tm) * w2)

    @pl.when(i == 0)
    def _():
        _dest_copy(dest_hbm, dsm, dsem, i).start()
        _dest_copy(dest_hbm, dsm, dsem, i).wait()
        issue_gather(i, 0)

    @pl.when(i + 1 < n)
    def _():
        _dest_copy(dest_hbm, dsm, dsem, i + 1).start()

    issue_gather(i, 1)
    combine_tile(0)

    @pl.when(i + 1 < n)
    def _():
        _dest_copy(dest_hbm, dsm, dsem, i + 1).wait()
        issue_gather(i + 1, 0)

    combine_tile(1)


def _combine(dest3, x1, route, ybuf):
    T = x1.shape[0]
    n_steps = dest3.shape[0]
    tm2 = T // n_steps
    return pl.pallas_call(
        _combine_kernel,
        grid=(n_steps,),
        in_specs=[
            pl.BlockSpec(memory_space=pl.ANY),
            pl.BlockSpec((tm2, D_MODEL), lambda i: (i, 0)),
            pl.BlockSpec((tm2, LANES), lambda i: (i, 0)),
            pl.BlockSpec(memory_space=pl.ANY),
        ],
        out_specs=pl.BlockSpec((tm2, D_MODEL), lambda i: (i, 0)),
        out_shape=jax.ShapeDtypeStruct((T, D_MODEL), F32),
        scratch_shapes=[
            pltpu.SMEM((2, 1, dest3.shape[2]), jnp.int32),
            pltpu.SemaphoreType.DMA((2,)),
            pltpu.VMEM((2, TOP_K * (tm2 // 2) * ROW_TILE, LANES), F32),
            pltpu.SemaphoreType.DMA((2,)),
        ],
        compiler_params=_cparams(("arbitrary",)),
        name="combine",
    )(dest3, x1, route, ybuf)


def _layer(x2d, batch, seq, lambda_init, norm1_w, w_in, b_igate, b_fgate, conv_w, conv_b, m_norm_w,
           q_norm_w, k_norm_w, lq1, lk1, lq2, lk2, a_norm_w, w_out, norm2_w,
           w_group, b_group, w_expert, b_expert, w_gate, w_up, w_down):
    T = x2d.shape[0]
    row = lambda v: v.reshape(1, -1).astype(F32)
    g0 = 2 * MIX_HALF + 2 * MIX_HALF
    g1 = g0 + 2 * N_HEADS
    w_proj = jnp.concatenate(
        [w_in[:, :g0], w_in[:, g1:], w_in[:, g0:g1], jnp.zeros((D_MODEL, LANES - 2 * N_HEADS), F32)],
        axis=1).astype(BF16)
    gate_b = jnp.concatenate([b_igate, b_fgate, jnp.zeros((LANES - 2 * N_HEADS,), F32)]).reshape(1, LANES)

    qw = row(jnp.concatenate([q_norm_w, q_norm_w])) * (QK_DIM ** -0.5 * math.log2(math.e))
    kw = row(jnp.concatenate([k_norm_w, k_norm_w]))
    proj, gates, kn, qt, vt = _inproj(x2d, row(norm1_w), w_proj, conv_w, row(conv_b), qw, kw, seq)
    hm = _mlstm(proj, gates, gate_b, row(m_norm_w), batch, seq)
    ha = _diff_attn(qt, kn, vt, row(lq1), row(lk1), row(lq2), row(lk2), row(a_norm_w),
                    batch, seq, lambda_init)

    w_router = jnp.concatenate(
        [w_expert, w_group, jnp.zeros((D_MODEL, LANES - N_EXPERTS - N_GROUPS), F32)], axis=1)
    wr_hi = w_router.astype(BF16)
    wr_lo = (w_router - wr_hi.astype(F32)).astype(BF16)
    b_router = jnp.concatenate(
        [b_expert, b_group, jnp.zeros((LANES - N_EXPERTS - N_GROUPS,), F32)]).reshape(1, LANES)
    x1, h2, route, route_t, counts = _outproj_router(hm, ha, x2d, w_out.astype(BF16), row(norm2_w),
                                                     wr_hi, wr_lo, b_router)

    blk = MOE_BLOCK
    counts = counts[0, :N_EXPERTS].astype(jnp.int32)
    padded = ((counts + blk - 1) // blk) * blk
    pad_end = jnp.cumsum(padded)
    pad_start = pad_end - padded
    nb = (T * TOP_K) // blk + N_EXPERTS
    n_valid = (pad_end[-1] // blk).astype(jnp.int32)
    bstart = jnp.minimum(jnp.arange(nb, dtype=jnp.int32), n_valid - 1) * blk
    block_expert = jnp.minimum(
        jnp.sum(pad_end[None, :] <= bstart[:, None], axis=1), N_EXPERTS - 1).astype(jnp.int32)
    eid = route_t[0:TOP_K].astype(jnp.int32)
    rank = route_t[TOP_K:2 * TOP_K].astype(jnp.int32)
    experts = jnp.arange(N_EXPERTS, dtype=jnp.int32)[:, None, None]
    dest = rank + jnp.sum(jnp.where(eid[None] == experts, pad_start[:, None, None], 0), axis=0)
    tm2 = min(2 * TM_ROWS, T)
    dest3 = dest.reshape(TOP_K, T // tm2, tm2).transpose(1, 0, 2).reshape(T // tm2, 1, TOP_K * tm2)

    xbuf = _scatter_rows(pad_end.astype(jnp.int32), padded - counts, dest3, h2, nb * blk)
    ybuf = _experts(block_expert, n_valid.reshape(1), xbuf, w_gate, w_up, w_down)
    return _combine(dest3, x1, route, ybuf)


def kernel(x, norm1_w, w_in, b_igate, b_fgate, conv_w, conv_b, m_norm_w, q_norm_w, k_norm_w,
           lambda_q1, lambda_k1, lambda_q2, lambda_k2, a_norm_w, w_out, norm2_w,
           w_group, b_group, w_expert, b_expert, w_gate, w_up, w_down):
    batch, seq, d = x.shape
    x2d = x.reshape(batch * seq, d)
    for l in range(norm1_w.shape[0]):
        lambda_init = 0.8 - 0.6 * math.exp(-0.3 * l)
        x2d = _layer(x2d, batch, seq, lambda_init, norm1_w[l], w_in[l], b_igate[l], b_fgate[l],
                     conv_w[l], conv_b[l], m_norm_w[l], q_norm_w[l], k_norm_w[l],
                     lambda_q1[l], lambda_k1[l], lambda_q2[l], lambda_k2[l], a_norm_w[l],
                     w_out[l], norm2_w[l], w_group[l], b_group[l], w_expert[l], b_expert[l],
                     w_gate[l], w_up[l], w_down[l])
    return x2d.reshape(batch, seq, d)
```

```python
import functools
import math

import jax
import jax.numpy as jnp
from jax import lax
from jax.experimental import pallas as pl
from jax.experimental.pallas import tpu as pltpu

F32 = jnp.float32
BF16 = jnp.bfloat16
EPS = 1e-6

D_MODEL = 1024
HEAD_DIM = 128
N_HEADS = 4
MIX_HALF = N_HEADS * HEAD_DIM
QK_DIM = 64
CONV_WIDTH = 4
ATTN_CHUNK = 64
SUM_ROWS = 16
N_GROUPS = 4
EXPERTS_PER_GROUP = 8
N_EXPERTS = N_GROUPS * EXPERTS_PER_GROUP
TOP_K = 2
D_EXPERT = D_MODEL // 2

LANES = 128
PROJ_COLS = 7 * MIX_HALF
VMEM_LIMIT = 48 * 1024 * 1024

TM_PROJ = 1024
MLSTM_CHUNK = 128
TQ_ATTN = 256
TM_OUT = 1024
ROUTE_ROWS = 8
OUT_PARTS = 8
EXPERT_PARTS = 2
TM_ROWS = 256
MOE_BLOCK = 512


def _cparams(sem):
    return pltpu.CompilerParams(dimension_semantics=sem, vmem_limit_bytes=VMEM_LIMIT)


CONV_HALO = 8


def _half_norm(x, w):
    lo = lax.broadcasted_iota(jnp.int32, x.shape, 1) < QK_DIM
    x2 = x * x
    s_lo = jnp.sum(jnp.where(lo, x2, 0.0), axis=-1, keepdims=True)
    s_hi = jnp.sum(jnp.where(lo, 0.0, x2), axis=-1, keepdims=True)
    ms = jnp.where(lo, s_lo, s_hi) * (1.0 / QK_DIM)
    return x * lax.rsqrt(ms + EPS) * w


def _inproj_kernel(x_ref, nw_ref, w_ref, cw_ref, cb_ref, qw_ref, kw_ref,
                   proj_ref, gate_ref, mvt_ref, kn_ref, qt_ref, vt_ref, tail, *, tiles_per_seq):
    tm = x_ref.shape[0]

    @pl.when(lax.rem(pl.program_id(0), tiles_per_seq) == 0)
    def _():
        tail[...] = jnp.zeros_like(tail)

    x = x_ref[...]
    ms = jnp.mean(x * x, axis=-1, keepdims=True)
    h = (x * lax.rsqrt(ms + EPS) * nw_ref[...]).astype(BF16)
    nb = MIX_HALF
    head = lambda hd: slice(hd * HEAD_DIM, (hd + 1) * HEAD_DIM)
    vrows = HEAD_DIM + SUM_ROWS
    for j in range(PROJ_COLS // nb):
        cols = slice(j * nb, (j + 1) * nb)
        pre = jnp.dot(h, w_ref[:, cols], preferred_element_type=F32)
        if j < 2:
            win = jnp.concatenate([tail[:, cols], pre], axis=0)
            acc = cb_ref[:, cols]
            for t in range(CONV_WIDTH):
                sh = CONV_WIDTH - 1 - t
                xs = win if sh == 0 else pltpu.roll(win, sh, axis=0)
                acc = acc + xs[CONV_HALO:, :] * cw_ref[t:t + 1, cols]
            tail[:, cols] = pre[tm - CONV_HALO:, :]
            pre = acc * jax.nn.sigmoid(acc)
            if j == 1:
                pre = pre * (HEAD_DIM ** -0.5)
            proj_ref[:, cols] = pre.astype(BF16)
        elif j == 2:
            for c in range(tm // MLSTM_CHUNK):
                for hd in range(N_HEADS):
                    mvt_ref[c, head(hd), :] = pre[c * MLSTM_CHUNK:(c + 1) * MLSTM_CHUNK,
                                                  head(hd)].T.astype(BF16)
        elif j == 3:
            proj_ref[:, 2 * nb:3 * nb] = pre.astype(BF16)
        elif j == 4:
            for hd in range(N_HEADS):
                qt_ref[head(hd), :] = _half_norm(pre[:, head(hd)], qw_ref[...]).T.astype(BF16)
        elif j == 5:
            for hd in range(N_HEADS):
                kn_ref[:, head(hd)] = _half_norm(pre[:, head(hd)], kw_ref[...]).astype(BF16)
        else:
            for hd in range(N_HEADS):
                vt_ref[hd * vrows:hd * vrows + HEAD_DIM, :] = pre[:, head(hd)].T.astype(BF16)
                vt_ref[hd * vrows + HEAD_DIM:(hd + 1) * vrows, :] = jnp.ones((SUM_ROWS, tm), BF16)
    gate_ref[...] = jnp.dot(h, w_ref[:, PROJ_COLS:], preferred_element_type=F32)


def _inproj(x2d, nw, w, conv_w, conv_b, qw, kw, seq):
    T = x2d.shape[0]
    tm = min(TM_PROJ, seq)
    cst = lambda i: (0, 0)
    row = lambda i: (i, 0)
    col = lambda i: (0, i)
    vrows = N_HEADS * (HEAD_DIM + SUM_ROWS)
    return pl.pallas_call(
        functools.partial(_inproj_kernel, tiles_per_seq=seq // tm),
        grid=(T // tm,),
        in_specs=[
            pl.BlockSpec((tm, D_MODEL), row),
            pl.BlockSpec((1, D_MODEL), cst),
            pl.BlockSpec((D_MODEL, PROJ_COLS + LANES), cst),
            pl.BlockSpec((CONV_WIDTH, 2 * MIX_HALF), cst),
            pl.BlockSpec((1, 2 * MIX_HALF), cst),
            pl.BlockSpec((1, HEAD_DIM), cst),
            pl.BlockSpec((1, HEAD_DIM), cst),
        ],
        out_specs=[
            pl.BlockSpec((tm, 3 * MIX_HALF), row),
            pl.BlockSpec((tm, LANES), row),
            pl.BlockSpec((tm // MLSTM_CHUNK, MIX_HALF, MLSTM_CHUNK), lambda i: (i, 0, 0)),
            pl.BlockSpec((tm, MIX_HALF), row),
            pl.BlockSpec((MIX_HALF, tm), col),
            pl.BlockSpec((vrows, tm), col),
        ],
        out_shape=[
            jax.ShapeDtypeStruct((T, 3 * MIX_HALF), BF16),
            jax.ShapeDtypeStruct((T, LANES), F32),
            jax.ShapeDtypeStruct((T // MLSTM_CHUNK, MIX_HALF, MLSTM_CHUNK), BF16),
            jax.ShapeDtypeStruct((T, MIX_HALF), BF16),
            jax.ShapeDtypeStruct((MIX_HALF, T), BF16),
            jax.ShapeDtypeStruct((vrows, T), BF16),
        ],
        scratch_shapes=[pltpu.VMEM((CONV_HALO, 2 * MIX_HALF), F32)],
        compiler_params=_cparams(("arbitrary",)),
        name="inproj",
    )(x2d, nw, w, conv_w, conv_b, qw, kw)


def _log_sigmoid(x):
    return jnp.minimum(x, 0.0) - jnp.log(1.0 + jnp.exp(-jnp.abs(x)))


def _cumsum_lanes(x):
    n = x.shape[1]
    lane = lax.broadcasted_iota(jnp.int32, x.shape, 1)
    s = 1
    while s < n:
        x = x + jnp.where(lane >= s, pltpu.roll(x, s, axis=1), 0.0)
        s *= 2
    return x


def _mlstm_kernel(p_ref, vt_ref, g_ref, gb_ref, nw_ref, out_ref, gt_s, bt_s, bc_s, *c_s, seq, chunk):
    for c_h in c_s:
        c_h[...] = jnp.zeros_like(c_h)
    src_le_out = (lax.broadcasted_iota(jnp.int32, (chunk, chunk), 0)
                  <= lax.broadcasted_iota(jnp.int32, (chunk, chunk), 1))
    gate_rows = 2 * N_HEADS
    nt = (((1,), (1,)), ((), ()))
    split_row = lax.broadcasted_iota(jnp.int32, (16, HEAD_DIM), 0)

    n_chunks = seq // chunk
    for c in range(n_chunks):
        g = g_ref[c * chunk:(c + 1) * chunk, :] + gb_ref[...]
        gt_s[c * gate_rows:(c + 1) * gate_rows, :] = g.T[0:gate_rows, :]
    bt_s[...] = _cumsum_lanes(_log_sigmoid(gt_s[...]))
    for c in range(n_chunks):
        b_t = bt_s[c * gate_rows:(c + 1) * gate_rows, :]
        bc_s[c * chunk:(c + 1) * chunk, :] = pltpu.roll(
            jnp.concatenate([b_t, jnp.zeros((LANES - gate_rows, chunk), F32)], axis=0).T,
            LANES - N_HEADS, axis=1)

    def step(c, carry):
        n_all, m_row = carry
        r0 = pl.multiple_of(c * chunk, chunk)
        rows = pl.ds(r0, chunk)
        g = g_ref[rows, :] + gb_ref[...]
        b_t = bt_s[pl.ds(pl.multiple_of(c * gate_rows, gate_rows), gate_rows), :]
        b_c = bc_s[rows, :]
        a_all = g - b_c
        b_end = b_c[chunk - 1:chunk, :]
        log_w = b_end + a_all
        m_new = jnp.maximum(b_end + m_row, jnp.max(log_w, axis=0, keepdims=True))
        w_all = jnp.exp(log_w - m_new)
        decay_row = jnp.exp(b_end + m_row - m_new)

        heads = range(N_HEADS)
        col = lambda blk, h: slice(blk * MIX_HALF + h * HEAD_DIM, blk * MIX_HALF + (h + 1) * HEAD_DIM)
        q = [p_ref[rows, col(0, h)] for h in heads]
        k = [p_ref[rows, col(1, h)] for h in heads]
        ct = [c_s[h][...] for h in heads]
        vt = [vt_ref[c, h * HEAD_DIM:(h + 1) * HEAD_DIM, :] for h in heads]
        b_row = [b_t[N_HEADS + h:N_HEADS + h + 1, :] for h in heads]
        log_d = [jnp.where(src_le_out, b_row[h] + a_all[:, h:h + 1], -jnp.inf) for h in heads]
        log_inter = [b_row[h] + m_row[:, h:h + 1] for h in heads]
        m_t = [jnp.maximum(log_inter[h], jnp.max(log_d[h], axis=0, keepdims=True)) for h in heads]
        inter = [jnp.exp(log_inter[h] - m_t[h]) for h in heads]
        st = [lax.dot_general(k[h], q[h], nt, preferred_element_type=F32) * jnp.exp(log_d[h] - m_t[h])
              for h in heads]
        cq = [lax.dot_general(ct[h].astype(BF16), q[h], nt, preferred_element_type=F32) for h in heads]
        nq = []
        for h in heads:
            n_hi = n_all[h].astype(BF16).astype(F32)
            n_mat = jnp.where(split_row == 0, n_hi,
                              jnp.where(split_row == 1, n_all[h] - n_hi, 0.0)).astype(BF16)
            nq.append(lax.dot_general(n_mat, q[h], nt, preferred_element_type=F32))
        num = [jnp.dot(vt[h], st[h].astype(BF16), preferred_element_type=F32) + inter[h] * cq[h]
               for h in heads]
        den = [jnp.sum(st[h], axis=0, keepdims=True) + inter[h] * (nq[h][0:1, :] + nq[h][1:2, :])
               for h in heads]
        hh = [(num[h] / jnp.maximum(jnp.abs(den[h]), jnp.exp(-m_t[h]))).T for h in heads]

        kw = [k[h].astype(F32) * w_all[:, h:h + 1] for h in heads]
        n_new = []
        for h in heads:
            decay = decay_row[:, h:h + 1]
            c_s[h][...] = decay * ct[h] + jnp.dot(vt[h], kw[h].astype(BF16), preferred_element_type=F32)
            n_new.append(decay * n_all[h] + jnp.sum(kw[h], axis=0, keepdims=True))
        for h in heads:
            hm = jax.nn.sigmoid(p_ref[rows, col(2, h)].astype(F32)) * hh[h]
            y = hm * lax.rsqrt(jnp.mean(hm * hm, axis=-1, keepdims=True) + EPS) * nw_ref[...]
            out_ref[rows, col(0, h)] = y.astype(BF16)
        return tuple(n_new), m_new

    init = (tuple(jnp.zeros((1, HEAD_DIM), F32) for _ in range(N_HEADS)), jnp.zeros((1, LANES), F32))
    lax.fori_loop(0, seq // chunk, step, init)


def _mlstm(proj, mvt, gates, gate_b, m_norm_w, batch, seq):
    T = batch * seq
    chunk = MLSTM_CHUNK
    assert chunk == LANES and seq % chunk == 0
    cst = lambda b: (0, 0)
    return pl.pallas_call(
        functools.partial(_mlstm_kernel, seq=seq, chunk=chunk),
        grid=(batch,),
        in_specs=[
            pl.BlockSpec((seq, 3 * MIX_HALF), lambda b: (b, 0)),
            pl.BlockSpec((seq // chunk, MIX_HALF, chunk), lambda b: (b, 0, 0)),
            pl.BlockSpec((seq, LANES), lambda b: (b, 0)),
            pl.BlockSpec((1, LANES), cst),
            pl.BlockSpec((1, HEAD_DIM), cst),
        ],
        out_specs=pl.BlockSpec((seq, MIX_HALF), lambda b: (b, 0)),
        out_shape=jax.ShapeDtypeStruct((T, MIX_HALF), BF16),
        scratch_shapes=[
            pltpu.VMEM((seq // chunk * 2 * N_HEADS, LANES), F32),
            pltpu.VMEM((seq // chunk * 2 * N_HEADS, LANES), F32),
            pltpu.VMEM((seq, LANES), F32),
        ] + [pltpu.VMEM((HEAD_DIM, HEAD_DIM), F32) for _ in range(N_HEADS)],
        compiler_params=_cparams(("arbitrary",)),
        name="mlstm",
    )(proj, mvt, gates, gate_b, m_norm_w)


def _attn_kernel(qt_ref, kn_ref, vt_ref, lq1_ref, lk1_ref, lq2_ref, lk2_ref, anw_ref, out_ref,
                 *, seq, tq, heads, lambda_init):
    nq = seq // tq
    hcol = lambda hd: slice(hd * HEAD_DIM, (hd + 1) * HEAD_DIM)
    vrows = HEAD_DIM + SUM_ROWS

    lam = (jnp.exp(jnp.sum(lq1_ref[...] * lk1_ref[...], axis=-1, keepdims=True))
           - jnp.exp(jnp.sum(lq2_ref[...] * lk2_ref[...], axis=-1, keepdims=True)) + lambda_init)
    first_map = lax.broadcasted_iota(jnp.int32, (HEAD_DIM, tq), 0) < QK_DIM
    q_in_tile = lax.rem(lax.broadcasted_iota(jnp.int32, (tq, 2 * tq), 1), tq)
    diag_ok = lax.broadcasted_iota(jnp.int32, (tq, 2 * tq), 0) // ATTN_CHUNK <= q_in_tile // ATTN_CHUNK

    def v_tile(hd, j):
        return vt_ref[hd * vrows:(hd + 1) * vrows, j * tq:(j + 1) * tq]

    def scores(hd, qi):
        r0 = qi * tq
        qt = qt_ref[hcol(hd), r0:r0 + tq]
        zero = jnp.zeros_like(qt)
        qc = jnp.concatenate([jnp.where(first_map, qt, zero), jnp.where(first_map, zero, qt)], axis=1)
        s_d = jnp.where(diag_ok, jnp.dot(kn_ref[r0:r0 + tq, hcol(hd)], qc, preferred_element_type=F32),
                        -jnp.inf)
        m = jnp.max(s_d, axis=0, keepdims=True)
        s_f = None
        if qi > 0:
            s_f = jnp.dot(kn_ref[0:r0, hcol(hd)], qc, preferred_element_type=F32)
            m = jnp.maximum(m, jnp.max(s_f, axis=0, keepdims=True))
        return s_d, s_f, m

    def weights(sc):
        s_d, s_f, m = sc
        p_d = jnp.exp2(s_d - m).astype(BF16)
        p_f = None if s_f is None else jnp.exp2(s_f - m).astype(BF16)
        return p_d, p_f

    def output(hd, qi, pw):
        p_d, p_f = pw
        r0 = qi * tq
        acc = jnp.dot(v_tile(hd, qi), p_d, preferred_element_type=F32)
        for j in range(qi):
            acc = acc + jnp.dot(v_tile(hd, j), p_f[j * tq:(j + 1) * tq], preferred_element_type=F32)
        on = acc[0:HEAD_DIM] * (1.0 / acc[HEAD_DIM:HEAD_DIM + 1])
        o = (on[:, 0:tq] - lam * on[:, tq:]).T
        y = o * lax.rsqrt(jnp.mean(o * o, axis=-1, keepdims=True) + EPS) * anw_ref[...] * (1.0 - lambda_init)
        out_ref[r0:r0 + tq, hcol(hd)] = y.astype(BF16)

    hs = range(heads)
    sc = {(hd, 0): scores(hd, 0) for hd in hs}
    pw = {}
    for step in range(nq + 2):
        for hd in hs:
            if step + 1 < nq:
                sc[hd, step + 1] = scores(hd, step + 1)
        for hd in hs:
            if 1 <= step <= nq:
                output(hd, step - 1, pw.pop((hd, step - 1)))
        for hd in hs:
            if step < nq:
                pw[hd, step] = weights(sc.pop((hd, step)))


ATTN_HEADS_PER_STEP = 2


def _diff_attn(qt, kn, vt, lq1, lk1, lq2, lk2, a_norm_w, batch, seq, lambda_init):
    T = batch * seq
    tq = min(TQ_ATTN, seq)
    hps = ATTN_HEADS_PER_STEP
    width = hps * HEAD_DIM
    cst = lambda b, g: (0, 0)
    return pl.pallas_call(
        functools.partial(_attn_kernel, seq=seq, tq=tq, heads=hps, lambda_init=lambda_init),
        grid=(batch, N_HEADS // hps),
        in_specs=[
            pl.BlockSpec((width, seq), lambda b, g: (g, b)),
            pl.BlockSpec((seq, width), lambda b, g: (b, g)),
            pl.BlockSpec((hps * (HEAD_DIM + SUM_ROWS), seq), lambda b, g: (g, b)),
            pl.BlockSpec((1, QK_DIM), cst),
            pl.BlockSpec((1, QK_DIM), cst),
            pl.BlockSpec((1, QK_DIM), cst),
            pl.BlockSpec((1, QK_DIM), cst),
            pl.BlockSpec((1, HEAD_DIM), cst),
        ],
        out_specs=pl.BlockSpec((seq, width), lambda b, g: (b, g)),
        out_shape=jax.ShapeDtypeStruct((T, MIX_HALF), BF16),
        compiler_params=_cparams(("arbitrary", "arbitrary")),
        name="diffattn",
    )(qt, kn, vt, lq1, lk1, lq2, lk2, a_norm_w)


def _outproj_router_kernel(hm_ref, ha_ref, x_ref, wo_ref, nw_ref, wr_hi_ref, wr_lo_ref, br_ref,
                           x1_ref, h2_ref, route_ref, route_t_ref, counts_ref, carry):
    i = pl.program_id(0)
    tm = x_ref.shape[0]
    tp = tm // OUT_PARTS
    parts = range(OUT_PARTS)
    rows = [slice(p * tp, (p + 1) * tp) for p in parts]

    @pl.when(i == 0)
    def _():
        carry[...] = jnp.zeros_like(carry)

    mix = [jnp.dot(hm_ref[rows[p], :], wo_ref[0:MIX_HALF, :], preferred_element_type=F32)
           + jnp.dot(ha_ref[rows[p], :], wo_ref[MIX_HALF:, :], preferred_element_type=F32) for p in parts]
    x1 = [x_ref[rows[p], :] + mix[p] for p in parts]
    h2 = [x1[p] * lax.rsqrt(jnp.mean(x1[p] * x1[p], axis=-1, keepdims=True) + EPS) * nw_ref[...]
          for p in parts]
    for p in parts:
        x1_ref[rows[p], :] = x1[p]
        h2_ref[rows[p], :] = h2[p]

    h_hi = [h2[p].astype(BF16) for p in parts]
    h_lo = [(h2[p] - h_hi[p].astype(F32)).astype(BF16) for p in parts]
    logits = [(jnp.dot(h_hi[p], wr_hi_ref[...], preferred_element_type=F32)
               + jnp.dot(h_lo[p], wr_hi_ref[...], preferred_element_type=F32)
               + jnp.dot(h_hi[p], wr_lo_ref[...], preferred_element_type=F32)) + br_ref[...] for p in parts]

    lane_i = lax.broadcasted_iota(jnp.int32, (tp, LANES), 1)
    lane = lane_i.astype(F32)
    big = float(LANES)
    gmask = (lane_i >= N_EXPERTS) & (lane_i < N_EXPERTS + N_GROUPS)
    gl = [jnp.where(gmask, logits[p], -jnp.inf) for p in parts]
    gmax = [jnp.max(gl[p], axis=-1, keepdims=True) for p in parts]
    g_sel = [jnp.min(jnp.where(gl[p] == gmax[p], lane, big), axis=-1, keepdims=True) - float(N_EXPERTS)
             for p in parts]
    g_w = [1.0 / jnp.sum(jnp.where(gmask, jnp.exp(logits[p] - gmax[p]), 0.0), axis=-1, keepdims=True)
           for p in parts]
    e_lo = [g_sel[p] * float(EXPERTS_PER_GROUP) for p in parts]
    el = [jnp.where((lane >= e_lo[p]) & (lane < e_lo[p] + float(EXPERTS_PER_GROUP)), logits[p], -jnp.inf)
          for p in parts]
    v1 = [jnp.max(el[p], axis=-1, keepdims=True) for p in parts]
    e1 = [jnp.min(jnp.where(el[p] == v1[p], lane, big), axis=-1, keepdims=True) for p in parts]
    el2 = [jnp.where(lane == e1[p], -jnp.inf, el[p]) for p in parts]
    v2 = [jnp.max(el2[p], axis=-1, keepdims=True) for p in parts]
    e2 = [jnp.min(jnp.where(el2[p] == v2[p], lane, big), axis=-1, keepdims=True) for p in parts]
    t2 = [jnp.exp(v2[p] - v1[p]) for p in parts]
    w1 = [g_w[p] * (1.0 / (1.0 + t2[p])) for p in parts]
    w2 = [g_w[p] * (t2[p] / (1.0 + t2[p])) for p in parts]

    oh1 = [(lane == e1[p]).astype(F32) for p in parts]
    oh2 = [(lane == e2[p]).astype(F32) for p in parts]
    both = [oh1[p] + oh2[p] for p in parts]
    strict = (lax.broadcasted_iota(jnp.int32, (tp, tp), 0)
              > lax.broadcasted_iota(jnp.int32, (tp, tp), 1)).astype(BF16)
    within = [jnp.dot(strict, both[p].astype(BF16), preferred_element_type=F32) for p in parts]
    before = carry[...]
    for p in parts:
        prefix = within[p] + before
        r1 = jnp.sum(oh1[p] * prefix, axis=-1, keepdims=True)
        r2 = jnp.sum(oh2[p] * prefix, axis=-1, keepdims=True)
        before = before + jnp.sum(both[p], axis=0, keepdims=True)
        route = jnp.where(lane_i == 0, e1[p], 0.0)
        route = jnp.where(lane_i == 1, e2[p], route)
        route = jnp.where(lane_i == 2, r1, route)
        route = jnp.where(lane_i == 3, r2, route)
        route = jnp.where(lane_i == 4, w1[p], route)
        route = jnp.where(lane_i == 5, w2[p], route)
        route_ref[rows[p], :] = route
        route_t_ref[:, rows[p]] = route.T[0:ROUTE_ROWS, :]
    carry[...] = before
    counts_ref[...] = before


def _outproj_router(hm, ha, x2d, wo, nw, wr_hi, wr_lo, br):
    T = x2d.shape[0]
    tm = min(TM_OUT, T)
    row = lambda i: (i, 0)
    cst = lambda i: (0, 0)
    return pl.pallas_call(
        _outproj_router_kernel,
        grid=(T // tm,),
        in_specs=[
            pl.BlockSpec((tm, MIX_HALF), row),
            pl.BlockSpec((tm, MIX_HALF), row),
            pl.BlockSpec((tm, D_MODEL), row),
            pl.BlockSpec((D_MODEL, D_MODEL), cst),
            pl.BlockSpec((1, D_MODEL), cst),
            pl.BlockSpec((D_MODEL, LANES), cst),
            pl.BlockSpec((D_MODEL, LANES), cst),
            pl.BlockSpec((1, LANES), cst),
        ],
        out_specs=[
            pl.BlockSpec((tm, D_MODEL), row),
            pl.BlockSpec((tm, D_MODEL), row),
            pl.BlockSpec((tm, LANES), row),
            pl.BlockSpec((ROUTE_ROWS, tm), lambda i: (0, i)),
            pl.BlockSpec((1, LANES), cst),
        ],
        out_shape=[
            jax.ShapeDtypeStruct((T, D_MODEL), F32),
            jax.ShapeDtypeStruct((T, D_MODEL), F32),
            jax.ShapeDtypeStruct((T, LANES), F32),
            jax.ShapeDtypeStruct((ROUTE_ROWS, T), F32),
            jax.ShapeDtypeStruct((1, LANES), F32),
        ],
        scratch_shapes=[pltpu.VMEM((1, LANES), F32)],
        compiler_params=_cparams(("arbitrary",)),
        name="outproj_router",
    )(hm, ha, x2d, wo, nw, wr_hi, wr_lo, br)


ROW_TILE = D_MODEL // LANES


def _to_tiles(dst_ref, x, n):
    for s in range(ROW_TILE):
        dst_ref[pl.ds(s, n, stride=ROW_TILE), :] = x[:, s * LANES:(s + 1) * LANES]


def _tile_piece(src_ref, s, n, base=0):
    return src_ref[pl.ds(base * ROW_TILE + s, n, stride=ROW_TILE), :]


def _row_dma(src, src_row, dst, dst_row, sem):
    def tile(row):
        start = row * ROW_TILE
        return pl.ds(start if isinstance(row, int) else pl.multiple_of(start, ROW_TILE), ROW_TILE)
    return pltpu.make_async_copy(src.at[tile(src_row)], dst.at[tile(dst_row)], sem)


def _dest_copy(dest_hbm, dsm, dsem, step):
    slot = lax.rem(step, 2)
    return pltpu.make_async_copy(dest_hbm.at[step], dsm.at[slot], dsem.at[slot])


def _scatter_kernel(pend_ref, padded_ref, dest_hbm, h2_ref, xbuf, dsm, dsem, stage, ssem, zbuf, zsem):
    i = pl.program_id(0)
    n = pl.num_programs(0)
    tm = h2_ref.shape[0] // 2
    blk = MOE_BLOCK

    @pl.when(i == 0)
    def _():
        zbuf[...] = jnp.zeros_like(zbuf)

        def zero_copy(e):
            row0 = pl.multiple_of((pend_ref[e] - blk) * ROW_TILE, blk * ROW_TILE)
            return pltpu.make_async_copy(zbuf, xbuf.at[pl.ds(row0, blk * ROW_TILE)], zsem)

        for e in range(N_EXPERTS):
            pl.when(padded_ref[e] > 0)(lambda e=e: zero_copy(e).start())
        for e in range(N_EXPERTS):
            pl.when(padded_ref[e] > 0)(lambda e=e: zero_copy(e).wait())

        def zero_tail(bi, _):
            row0 = pl.multiple_of(bi * (blk * ROW_TILE), blk * ROW_TILE)
            cp = pltpu.make_async_copy(zbuf, xbuf.at[pl.ds(row0, blk * ROW_TILE)], zsem)
            cp.start()
            cp.wait()
            return 0

        lax.fori_loop(pend_ref[N_EXPERTS - 1] // blk, xbuf.shape[0] // (blk * ROW_TILE), zero_tail, 0)
        _dest_copy(dest_hbm, dsm, dsem, i).start()

    _dest_copy(dest_hbm, dsm, dsem, i).wait()

    @pl.when(i + 1 < n)
    def _():
        _dest_copy(dest_hbm, dsm, dsem, i + 1).start()

    slot = lax.rem(i, 2)

    def drain(half):
        for _ in range(TOP_K):
            pltpu.make_async_copy(stage.at[half], xbuf.at[pl.ds(0, tm * ROW_TILE)], ssem.at[half]).wait()

    for half in range(2):
        pl.when(i > 0)(functools.partial(drain, half))
        _to_tiles(stage.at[half], h2_ref[half * tm:(half + 1) * tm, :], tm)
        for r in range(tm):
            for kk in range(TOP_K):
                d = dsm[slot, 0, kk * 2 * tm + half * tm + r]
                _row_dma(stage.at[half], r, xbuf, d, ssem.at[half]).start(priority=kk)

    @pl.when(i == n - 1)
    def _():
        for half in range(2):
            drain(half)


def _scatter_rows(pad_end, padded, dest3, h2, n_pos):
    T = h2.shape[0]
    n_steps = dest3.shape[0]
    tm2 = T // n_steps
    return pl.pallas_call(
        _scatter_kernel,
        grid_spec=pltpu.PrefetchScalarGridSpec(
            num_scalar_prefetch=2,
            grid=(n_steps,),
            in_specs=[
                pl.BlockSpec(memory_space=pl.ANY),
                pl.BlockSpec((tm2, D_MODEL), lambda i, pe, pd: (i, 0)),
            ],
            out_specs=pl.BlockSpec(memory_space=pl.ANY),
            scratch_shapes=[
                pltpu.SMEM((2, 1, dest3.shape[2]), jnp.int32),
                pltpu.SemaphoreType.DMA((2,)),
                pltpu.VMEM((2, (tm2 // 2) * ROW_TILE, LANES), F32),
                pltpu.SemaphoreType.DMA((2,)),
                pltpu.VMEM((MOE_BLOCK * ROW_TILE, LANES), F32),
                pltpu.SemaphoreType.DMA,
            ],
        ),
        out_shape=jax.ShapeDtypeStruct((n_pos * ROW_TILE, LANES), F32),
        compiler_params=_cparams(("arbitrary",)),
        name="scatter_rows",
    )(pad_end, padded, dest3, h2)


def _expert_kernel(be_ref, nv_ref, x_ref, wg_ref, wu_ref, wd_ref, y_ref, wg_s, wu_s, wd_s):
    b = pl.program_id(0)
    blk = MOE_BLOCK

    @pl.when((b == 0) | (be_ref[b] != be_ref[jnp.maximum(b - 1, 0)]))
    def _():
        wg_s[...] = wg_ref[0].astype(BF16)
        wu_s[...] = wu_ref[0].astype(BF16)
        wd_s[...] = wd_ref[0].astype(BF16)

    @pl.when(b < nv_ref[0])
    def _():
        parts = range(EXPERT_PARTS)
        tp = blk // EXPERT_PARTS
        x = [jnp.concatenate([_tile_piece(x_ref, s, tp, base=p * tp) for s in range(ROW_TILE)],
                             axis=1).astype(BF16) for p in parts]
        g = [jnp.dot(x[p], wg_s[...], preferred_element_type=F32) for p in parts]
        u = [jnp.dot(x[p], wu_s[...], preferred_element_type=F32) for p in parts]
        hid = [(g[p] * jax.nn.sigmoid(g[p]) * u[p]).astype(BF16) for p in parts]
        y = [jnp.dot(hid[p], wd_s[...], preferred_element_type=F32) for p in parts]
        for p in parts:
            _to_tiles(y_ref.at[pl.ds(p * tp * ROW_TILE, tp * ROW_TILE)], y[p], tp)

    @pl.when(b >= nv_ref[0])
    def _():
        y_ref[...] = jnp.zeros_like(y_ref)


def _experts(block_expert, n_valid, xbuf, wg, wu, wd):
    blk = MOE_BLOCK
    nb = xbuf.shape[0] // (blk * ROW_TILE)
    xmap = lambda b, be, nv: (jnp.minimum(b, nv[0] - 1), 0)
    wmap = lambda b, be, nv: (be[b], 0, 0)
    return pl.pallas_call(
        _expert_kernel,
        grid_spec=pltpu.PrefetchScalarGridSpec(
            num_scalar_prefetch=2,
            grid=(nb,),
            in_specs=[
                pl.BlockSpec((blk * ROW_TILE, LANES), xmap),
                pl.BlockSpec((1, D_MODEL, D_EXPERT), wmap),
                pl.BlockSpec((1, D_MODEL, D_EXPERT), wmap),
                pl.BlockSpec((1, D_EXPERT, D_MODEL), wmap),
            ],
            out_specs=pl.BlockSpec((blk * ROW_TILE, LANES), lambda b, be, nv: (b, 0)),
            scratch_shapes=[
                pltpu.VMEM((D_MODEL, D_EXPERT), BF16),
                pltpu.VMEM((D_MODEL, D_EXPERT), BF16),
                pltpu.VMEM((D_EXPERT, D_MODEL), BF16),
            ],
        ),
        out_shape=jax.ShapeDtypeStruct(xbuf.shape, F32),
        compiler_params=_cparams(("arbitrary",)),
        name="experts",
    )(block_expert, n_valid, xbuf, wg, wu, wd)


def _combine_kernel(dest_hbm, x1_ref, route_ref, ybuf, out_ref, dsm, dsem, buf, gsem):
    i = pl.program_id(0)
    n = pl.num_programs(0)
    tm = x1_ref.shape[0] // 2

    def issue_gather(step, half):
        slot = lax.rem(step, 2)
        for r in range(tm):
            for kk in range(TOP_K):
                d = dsm[slot, 0, kk * 2 * tm + half * tm + r]
                _row_dma(ybuf, d, buf.at[half], kk * tm + r, gsem.at[half]).start(priority=kk)

    def combine_tile(half):
        for _ in range(TOP_K):
            pltpu.make_async_copy(ybuf.at[pl.ds(0, tm * ROW_TILE)],
                                  buf.at[half, pl.ds(0, tm * ROW_TILE)], gsem.at[half]).wait()
        rows = slice(half * tm, (half + 1) * tm)
        w1 = route_ref[rows, 4:5]
        w2 = route_ref[rows, 5:6]
        for s in range(ROW_TILE):
            cols = slice(s * LANES, (s + 1) * LANES)
            out_ref[rows, cols] = x1_ref[rows, cols] + (_tile_piece(buf.at[half], s, tm) * w1
                                                        + _tile_piece(buf.at[half], s, tm, base=tm) * w2)

    @pl.when(i == 0)
    def _():
        _dest_copy(dest_hbm, dsm, dsem, i).start()
        _dest_copy(dest_hbm, dsm, dsem, i).wait()
        issue_gather(i, 0)

    @pl.when(i + 1 < n)
    def _():
        _dest_copy(dest_hbm, dsm, dsem, i + 1).start()

    issue_gather(i, 1)
    combine_tile(0)

    @pl.when(i + 1 < n)
    def _():
        _dest_copy(dest_hbm, dsm, dsem, i + 1).wait()
        issue_gather(i + 1, 0)

    combine_tile(1)


def _combine(dest3, x1, route, ybuf):
    T = x1.shape[0]
    n_steps = dest3.shape[0]
    tm2 = T // n_steps
    return pl.pallas_call(
        _combine_kernel,
        grid=(n_steps,),
        in_specs=[
            pl.BlockSpec(memory_space=pl.ANY),
            pl.BlockSpec((tm2, D_MODEL), lambda i: (i, 0)),
            pl.BlockSpec((tm2, LANES), lambda i: (i, 0)),
            pl.BlockSpec(memory_space=pl.ANY),
        ],
        out_specs=pl.BlockSpec((tm2, D_MODEL), lambda i: (i, 0)),
        out_shape=jax.ShapeDtypeStruct((T, D_MODEL), F32),
        scratch_shapes=[
            pltpu.SMEM((2, 1, dest3.shape[2]), jnp.int32),
            pltpu.SemaphoreType.DMA((2,)),
            pltpu.VMEM((2, TOP_K * (tm2 // 2) * ROW_TILE, LANES), F32),
            pltpu.SemaphoreType.DMA((2,)),
        ],
        compiler_params=_cparams(("arbitrary",)),
        name="combine",
    )(dest3, x1, route, ybuf)


def _layer(x2d, batch, seq, lambda_init, norm1_w, w_in, b_igate, b_fgate, conv_w, conv_b, m_norm_w,
           q_norm_w, k_norm_w, lq1, lk1, lq2, lk2, a_norm_w, w_out, norm2_w,
           w_group, b_group, w_expert, b_expert, w_gate, w_up, w_down):
    T = x2d.shape[0]
    row = lambda v: v.reshape(1, -1).astype(F32)
    g0 = 2 * MIX_HALF + 2 * MIX_HALF
    g1 = g0 + 2 * N_HEADS
    w_proj = jnp.concatenate(
        [w_in[:, :g0], w_in[:, g1:], w_in[:, g0:g1], jnp.zeros((D_MODEL, LANES - 2 * N_HEADS), F32)],
        axis=1).astype(BF16)
    gate_b = jnp.concatenate([b_igate, b_fgate, jnp.zeros((LANES - 2 * N_HEADS,), F32)]).reshape(1, LANES)

    qw = row(jnp.concatenate([q_norm_w, q_norm_w])) * (QK_DIM ** -0.5 * math.log2(math.e))
    kw = row(jnp.concatenate([k_norm_w, k_norm_w]))
    proj, gates, mvt, kn, qt, vt = _inproj(x2d, row(norm1_w), w_proj, conv_w, row(conv_b), qw, kw, seq)
    hm = _mlstm(proj, mvt, gates, gate_b, row(m_norm_w), batch, seq)
    ha = _diff_attn(qt, kn, vt, row(lq1), row(lk1), row(lq2), row(lk2), row(a_norm_w),
                    batch, seq, lambda_init)

    w_router = jnp.concatenate(
        [w_expert, w_group, jnp.zeros((D_MODEL, LANES - N_EXPERTS - N_GROUPS), F32)], axis=1)
    wr_hi = w_router.astype(BF16)
    wr_lo = (w_router - wr_hi.astype(F32)).astype(BF16)
    b_router = jnp.concatenate(
        [b_expert, b_group, jnp.zeros((LANES - N_EXPERTS - N_GROUPS,), F32)]).reshape(1, LANES)
    x1, h2, route, route_t, counts = _outproj_router(hm, ha, x2d, w_out.astype(BF16), row(norm2_w),
                                                     wr_hi, wr_lo, b_router)

    blk = MOE_BLOCK
    counts = counts[0, :N_EXPERTS].astype(jnp.int32)
    padded = ((counts + blk - 1) // blk) * blk
    pad_end = jnp.cumsum(padded)
    pad_start = pad_end - padded
    nb = (T * TOP_K) // blk + N_EXPERTS
    n_valid = (pad_end[-1] // blk).astype(jnp.int32)
    bstart = jnp.minimum(jnp.arange(nb, dtype=jnp.int32), n_valid - 1) * blk
    block_expert = jnp.minimum(
        jnp.sum(pad_end[None, :] <= bstart[:, None], axis=1), N_EXPERTS - 1).astype(jnp.int32)
    eid = route_t[0:TOP_K].astype(jnp.int32)
    rank = route_t[TOP_K:2 * TOP_K].astype(jnp.int32)
    experts = jnp.arange(N_EXPERTS, dtype=jnp.int32)[:, None, None]
    dest = rank + jnp.sum(jnp.where(eid[None] == experts, pad_start[:, None, None], 0), axis=0)
    tm2 = min(2 * TM_ROWS, T)
    dest3 = dest.reshape(TOP_K, T // tm2, tm2).transpose(1, 0, 2).reshape(T // tm2, 1, TOP_K * tm2)

    xbuf = _scatter_rows(pad_end.astype(jnp.int32), padded - counts, dest3, h2, nb * blk)
    ybuf = _experts(block_expert, n_valid.reshape(1), xbuf, w_gate, w_up, w_down)
    return _combine(dest3, x1, route, ybuf)


def kernel(x, norm1_w, w_in, b_igate, b_fgate, conv_w, conv_b, m_norm_w, q_norm_w, k_norm_w,
           lambda_q1, lambda_k1, lambda_q2, lambda_k2, a_norm_w, w_out, norm2_w,
           w_group, b_group, w_expert, b_expert, w_gate, w_up, w_down):
    batch, seq, d = x.shape
    x2d = x.reshape(batch * seq, d)
    for l in range(norm1_w.shape[0]):
        lambda_init = 0.8 - 0.6 * math.exp(-0.3 * l)
        x2d = _layer(x2d, batch, seq, lambda_init, norm1_w[l], w_in[l], b_igate[l], b_fgate[l],
                     conv_w[l], conv_b[l], m_norm_w[l], q_norm_w[l], k_norm_w[l],
                     lambda_q1[l], lambda_k1[l], lambda_q2[l], lambda_k2[l], a_norm_w[l],
                     w_out[l], norm2_w[l], w_group[l], b_group[l], w_expert[l], b_expert[l],
                     w_gate[l], w_up[l], w_down[l])
    return x2d.reshape(batch, seq, d)
```

```python
import functools
import math

import jax
import jax.numpy as jnp
from jax import lax
from jax.experimental import pallas as pl
from jax.experimental.pallas import tpu as pltpu

F32 = jnp.float32
BF16 = jnp.bfloat16
EPS = 1e-6

D_MODEL = 1024
HEAD_DIM = 128
N_HEADS = 4
MIX_HALF = N_HEADS * HEAD_DIM
QK_DIM = 64
CONV_WIDTH = 4
ATTN_CHUNK = 64
SUM_ROWS = 16
N_GROUPS = 4
EXPERTS_PER_GROUP = 8
N_EXPERTS = N_GROUPS * EXPERTS_PER_GROUP
TOP_K = 2
D_EXPERT = D_MODEL // 2

LANES = 128
PROJ_COLS = 7 * MIX_HALF
VMEM_LIMIT = 48 * 1024 * 1024

TM_PROJ = 1024
MLSTM_CHUNK = 128
TQ_ATTN = 256
TM_OUT = 1024
ROUTE_ROWS = 8
OUT_PARTS = 8
EXPERT_PARTS = 2
TM_ROWS = 512
MOE_BLOCK = 512


def _cparams(sem):
    return pltpu.CompilerParams(dimension_semantics=sem, vmem_limit_bytes=VMEM_LIMIT)


CONV_HALO = 8


def _half_norm(x, w):
    lo = lax.broadcasted_iota(jnp.int32, x.shape, 1) < QK_DIM
    x2 = x * x
    s_lo = jnp.sum(jnp.where(lo, x2, 0.0), axis=-1, keepdims=True)
    s_hi = jnp.sum(jnp.where(lo, 0.0, x2), axis=-1, keepdims=True)
    ms = jnp.where(lo, s_lo, s_hi) * (1.0 / QK_DIM)
    return x * lax.rsqrt(ms + EPS) * w


def _inproj_kernel(x_ref, nw_ref, w_ref, cw_ref, cb_ref, qw_ref, kw_ref,
                   proj_ref, gate_ref, mvt_ref, kn_ref, qt_ref, vt_ref, tail, *, tiles_per_seq):
    tm = x_ref.shape[0]

    @pl.when(lax.rem(pl.program_id(0), tiles_per_seq) == 0)
    def _():
        tail[...] = jnp.zeros_like(tail)

    x = x_ref[...]
    ms = jnp.mean(x * x, axis=-1, keepdims=True)
    h = (x * lax.rsqrt(ms + EPS) * nw_ref[...]).astype(BF16)
    nb = MIX_HALF
    head = lambda hd: slice(hd * HEAD_DIM, (hd + 1) * HEAD_DIM)
    vrows = HEAD_DIM + SUM_ROWS
    for j in range(PROJ_COLS // nb):
        cols = slice(j * nb, (j + 1) * nb)
        pre = jnp.dot(h, w_ref[:, cols], preferred_element_type=F32)
        if j < 2:
            win = jnp.concatenate([tail[:, cols], pre], axis=0)
            acc = cb_ref[:, cols]
            for t in range(CONV_WIDTH):
                sh = CONV_WIDTH - 1 - t
                xs = win if sh == 0 else pltpu.roll(win, sh, axis=0)
                acc = acc + xs[CONV_HALO:, :] * cw_ref[t:t + 1, cols]
            tail[:, cols] = pre[tm - CONV_HALO:, :]
            pre = acc * jax.nn.sigmoid(acc)
            if j == 1:
                pre = pre * (HEAD_DIM ** -0.5)
            proj_ref[:, cols] = pre.astype(BF16)
        elif j == 2:
            for c in range(tm // MLSTM_CHUNK):
                for hd in range(N_HEADS):
                    mvt_ref[c, head(hd), :] = pre[c * MLSTM_CHUNK:(c + 1) * MLSTM_CHUNK,
                                                  head(hd)].T.astype(BF16)
        elif j == 3:
            proj_ref[:, 2 * nb:3 * nb] = pre.astype(BF16)
        elif j == 4:
            for hd in range(N_HEADS):
                qt_ref[head(hd), :] = _half_norm(pre[:, head(hd)], qw_ref[...]).T.astype(BF16)
        elif j == 5:
            for hd in range(N_HEADS):
                kn_ref[:, head(hd)] = _half_norm(pre[:, head(hd)], kw_ref[...]).astype(BF16)
        else:
            for hd in range(N_HEADS):
                vt_ref[hd * vrows:hd * vrows + HEAD_DIM, :] = pre[:, head(hd)].T.astype(BF16)
                vt_ref[hd * vrows + HEAD_DIM:(hd + 1) * vrows, :] = jnp.ones((SUM_ROWS, tm), BF16)
    gate_ref[...] = jnp.dot(h, w_ref[:, PROJ_COLS:], preferred_element_type=F32)


def _inproj(x2d, nw, w, conv_w, conv_b, qw, kw, seq):
    T = x2d.shape[0]
    tm = min(TM_PROJ, seq)
    cst = lambda i: (0, 0)
    row = lambda i: (i, 0)
    col = lambda i: (0, i)
    vrows = N_HEADS * (HEAD_DIM + SUM_ROWS)
    return pl.pallas_call(
        functools.partial(_inproj_kernel, tiles_per_seq=seq // tm),
        grid=(T // tm,),
        in_specs=[
            pl.BlockSpec((tm, D_MODEL), row),
            pl.BlockSpec((1, D_MODEL), cst),
            pl.BlockSpec((D_MODEL, PROJ_COLS + LANES), cst),
            pl.BlockSpec((CONV_WIDTH, 2 * MIX_HALF), cst),
            pl.BlockSpec((1, 2 * MIX_HALF), cst),
            pl.BlockSpec((1, HEAD_DIM), cst),
            pl.BlockSpec((1, HEAD_DIM), cst),
        ],
        out_specs=[
            pl.BlockSpec((tm, 3 * MIX_HALF), row),
            pl.BlockSpec((tm, LANES), row),
            pl.BlockSpec((tm // MLSTM_CHUNK, MIX_HALF, MLSTM_CHUNK), lambda i: (i, 0, 0)),
            pl.BlockSpec((tm, MIX_HALF), row),
            pl.BlockSpec((MIX_HALF, tm), col),
            pl.BlockSpec((vrows, tm), col),
        ],
        out_shape=[
            jax.ShapeDtypeStruct((T, 3 * MIX_HALF), BF16),
            jax.ShapeDtypeStruct((T, LANES), F32),
            jax.ShapeDtypeStruct((T // MLSTM_CHUNK, MIX_HALF, MLSTM_CHUNK), BF16),
            jax.ShapeDtypeStruct((T, MIX_HALF), BF16),
            jax.ShapeDtypeStruct((MIX_HALF, T), BF16),
            jax.ShapeDtypeStruct((vrows, T), BF16),
        ],
        scratch_shapes=[pltpu.VMEM((CONV_HALO, 2 * MIX_HALF), F32)],
        compiler_params=_cparams(("arbitrary",)),
        name="inproj",
    )(x2d, nw, w, conv_w, conv_b, qw, kw)


def _log_sigmoid(x):
    return jnp.minimum(x, 0.0) - jnp.log(1.0 + jnp.exp(-jnp.abs(x)))


def _cumsum_lanes(x):
    n = x.shape[1]
    lane = lax.broadcasted_iota(jnp.int32, x.shape, 1)
    s = 1
    while s < n:
        x = x + jnp.where(lane >= s, pltpu.roll(x, s, axis=1), 0.0)
        s *= 2
    return x


def _mlstm_kernel(p_ref, vt_ref, g_ref, gb_ref, nw_ref, out_ref, gt_s, bt_s, bc_s, *c_s, seq, chunk):
    for c_h in c_s:
        c_h[...] = jnp.zeros_like(c_h)
    src_le_out = (lax.broadcasted_iota(jnp.int32, (chunk, chunk), 0)
                  <= lax.broadcasted_iota(jnp.int32, (chunk, chunk), 1))
    gate_rows = 2 * N_HEADS
    nt = (((1,), (1,)), ((), ()))
    split_row = lax.broadcasted_iota(jnp.int32, (16, HEAD_DIM), 0)

    n_chunks = seq // chunk
    for c in range(n_chunks):
        g = g_ref[c * chunk:(c + 1) * chunk, :] + gb_ref[...]
        gt_s[c * gate_rows:(c + 1) * gate_rows, :] = g.T[0:gate_rows, :]
    bt_s[...] = _cumsum_lanes(_log_sigmoid(gt_s[...]))
    for c in range(n_chunks):
        b_t = bt_s[c * gate_rows:(c + 1) * gate_rows, :]
        bc_s[c * chunk:(c + 1) * chunk, :] = pltpu.roll(
            jnp.concatenate([b_t, jnp.zeros((LANES - gate_rows, chunk), F32)], axis=0).T,
            LANES - N_HEADS, axis=1)

    def step(c, carry):
        n_all, m_row = carry
        r0 = pl.multiple_of(c * chunk, chunk)
        rows = pl.ds(r0, chunk)
        g = g_ref[rows, :] + gb_ref[...]
        b_t = bt_s[pl.ds(pl.multiple_of(c * gate_rows, gate_rows), gate_rows), :]
        b_c = bc_s[rows, :]
        a_all = g - b_c
        b_end = b_c[chunk - 1:chunk, :]
        log_w = b_end + a_all
        m_new = jnp.maximum(b_end + m_row, jnp.max(log_w, axis=0, keepdims=True))
        w_all = jnp.exp(log_w - m_new)
        decay_row = jnp.exp(b_end + m_row - m_new)

        heads = range(N_HEADS)
        col = lambda blk, h: slice(blk * MIX_HALF + h * HEAD_DIM, blk * MIX_HALF + (h + 1) * HEAD_DIM)
        q = [p_ref[rows, col(0, h)] for h in heads]
        k = [p_ref[rows, col(1, h)] for h in heads]
        ct = [c_s[h][...] for h in heads]
        vt = [vt_ref[c, h * HEAD_DIM:(h + 1) * HEAD_DIM, :] for h in heads]
        b_row = [b_t[N_HEADS + h:N_HEADS + h + 1, :] for h in heads]
        log_d = [jnp.where(src_le_out, b_row[h] + a_all[:, h:h + 1], -jnp.inf) for h in heads]
        log_inter = [b_row[h] + m_row[:, h:h + 1] for h in heads]
        m_t = [jnp.maximum(log_inter[h], jnp.max(log_d[h], axis=0, keepdims=True)) for h in heads]
        inter = [jnp.exp(log_inter[h] - m_t[h]) for h in heads]
        st = [lax.dot_general(k[h], q[h], nt, preferred_element_type=F32) * jnp.exp(log_d[h] - m_t[h])
              for h in heads]
        cq = [lax.dot_general(ct[h].astype(BF16), q[h], nt, preferred_element_type=F32) for h in heads]
        nq = []
        for h in heads:
            n_hi = n_all[h].astype(BF16).astype(F32)
            n_mat = jnp.where(split_row == 0, n_hi,
                              jnp.where(split_row == 1, n_all[h] - n_hi, 0.0)).astype(BF16)
            nq.append(lax.dot_general(n_mat, q[h], nt, preferred_element_type=F32))
        num = [jnp.dot(vt[h], st[h].astype(BF16), preferred_element_type=F32) + inter[h] * cq[h]
               for h in heads]
        den = [jnp.sum(st[h], axis=0, keepdims=True) + inter[h] * (nq[h][0:1, :] + nq[h][1:2, :])
               for h in heads]
        hh = [(num[h] / jnp.maximum(jnp.abs(den[h]), jnp.exp(-m_t[h]))).T for h in heads]

        kw = [k[h].astype(F32) * w_all[:, h:h + 1] for h in heads]
        n_new = []
        for h in heads:
            decay = decay_row[:, h:h + 1]
            c_s[h][...] = decay * ct[h] + jnp.dot(vt[h], kw[h].astype(BF16), preferred_element_type=F32)
            n_new.append(decay * n_all[h] + jnp.sum(kw[h], axis=0, keepdims=True))
        for h in heads:
            hm = jax.nn.sigmoid(p_ref[rows, col(2, h)].astype(F32)) * hh[h]
            y = hm * lax.rsqrt(jnp.mean(hm * hm, axis=-1, keepdims=True) + EPS) * nw_ref[...]
            out_ref[rows, col(0, h)] = y.astype(BF16)
        return tuple(n_new), m_new

    init = (tuple(jnp.zeros((1, HEAD_DIM), F32) for _ in range(N_HEADS)), jnp.zeros((1, LANES), F32))
    lax.fori_loop(0, seq // chunk, step, init)


def _mlstm(proj, mvt, gates, gate_b, m_norm_w, batch, seq):
    T = batch * seq
    chunk = MLSTM_CHUNK
    assert chunk == LANES and seq % chunk == 0
    cst = lambda b: (0, 0)
    return pl.pallas_call(
        functools.partial(_mlstm_kernel, seq=seq, chunk=chunk),
        grid=(batch,),
        in_specs=[
            pl.BlockSpec((seq, 3 * MIX_HALF), lambda b: (b, 0)),
            pl.BlockSpec((seq // chunk, MIX_HALF, chunk), lambda b: (b, 0, 0)),
            pl.BlockSpec((seq, LANES), lambda b: (b, 0)),
            pl.BlockSpec((1, LANES), cst),
            pl.BlockSpec((1, HEAD_DIM), cst),
        ],
        out_specs=pl.BlockSpec((seq, MIX_HALF), lambda b: (b, 0)),
        out_shape=jax.ShapeDtypeStruct((T, MIX_HALF), BF16),
        scratch_shapes=[
            pltpu.VMEM((seq // chunk * 2 * N_HEADS, LANES), F32),
            pltpu.VMEM((seq // chunk * 2 * N_HEADS, LANES), F32),
            pltpu.VMEM((seq, LANES), F32),
        ] + [pltpu.VMEM((HEAD_DIM, HEAD_DIM), F32) for _ in range(N_HEADS)],
        compiler_params=_cparams(("arbitrary",)),
        name="mlstm",
    )(proj, mvt, gates, gate_b, m_norm_w)


def _attn_kernel(qt_ref, kn_ref, vt_ref, lq1_ref, lk1_ref, lq2_ref, lk2_ref, anw_ref, out_ref,
                 *, seq, tq, heads, lambda_init):
    nq = seq // tq
    hcol = lambda hd: slice(hd * HEAD_DIM, (hd + 1) * HEAD_DIM)
    vrows = HEAD_DIM + SUM_ROWS

    lam = (jnp.exp(jnp.sum(lq1_ref[...] * lk1_ref[...], axis=-1, keepdims=True))
           - jnp.exp(jnp.sum(lq2_ref[...] * lk2_ref[...], axis=-1, keepdims=True)) + lambda_init)
    first_map = lax.broadcasted_iota(jnp.int32, (HEAD_DIM, tq), 0) < QK_DIM
    q_in_tile = lax.rem(lax.broadcasted_iota(jnp.int32, (tq, 2 * tq), 1), tq)
    diag_ok = lax.broadcasted_iota(jnp.int32, (tq, 2 * tq), 0) // ATTN_CHUNK <= q_in_tile // ATTN_CHUNK

    def v_tile(hd, j):
        return vt_ref[hd * vrows:(hd + 1) * vrows, j * tq:(j + 1) * tq]

    def scores(hd, qi):
        r0 = qi * tq
        qt = qt_ref[hcol(hd), r0:r0 + tq]
        zero = jnp.zeros_like(qt)
        qc = jnp.concatenate([jnp.where(first_map, qt, zero), jnp.where(first_map, zero, qt)], axis=1)
        s_d = jnp.where(diag_ok, jnp.dot(kn_ref[r0:r0 + tq, hcol(hd)], qc, preferred_element_type=F32),
                        -jnp.inf)
        m = jnp.max(s_d, axis=0, keepdims=True)
        s_f = None
        if qi > 0:
            s_f = jnp.dot(kn_ref[0:r0, hcol(hd)], qc, preferred_element_type=F32)
            m = jnp.maximum(m, jnp.max(s_f, axis=0, keepdims=True))
        return s_d, s_f, m

    def weights(sc):
        s_d, s_f, m = sc
        p_d = jnp.exp2(s_d - m).astype(BF16)
        p_f = None if s_f is None else jnp.exp2(s_f - m).astype(BF16)
        return p_d, p_f

    def output(hd, qi, pw):
        p_d, p_f = pw
        r0 = qi * tq
        acc = jnp.dot(v_tile(hd, qi), p_d, preferred_element_type=F32)
        for j in range(qi):
            acc = acc + jnp.dot(v_tile(hd, j), p_f[j * tq:(j + 1) * tq], preferred_element_type=F32)
        on = acc[0:HEAD_DIM] * (1.0 / acc[HEAD_DIM:HEAD_DIM + 1])
        o = (on[:, 0:tq] - lam * on[:, tq:]).T
        y = o * lax.rsqrt(jnp.mean(o * o, axis=-1, keepdims=True) + EPS) * anw_ref[...] * (1.0 - lambda_init)
        out_ref[r0:r0 + tq, hcol(hd)] = y.astype(BF16)

    hs = range(heads)
    sc = {(hd, 0): scores(hd, 0) for hd in hs}
    pw = {}
    for step in range(nq + 2):
        for hd in hs:
            if step + 1 < nq:
                sc[hd, step + 1] = scores(hd, step + 1)
        for hd in hs:
            if 1 <= step <= nq:
                output(hd, step - 1, pw.pop((hd, step - 1)))
        for hd in hs:
            if step < nq:
                pw[hd, step] = weights(sc.pop((hd, step)))


ATTN_HEADS_PER_STEP = 2


def _diff_attn(qt, kn, vt, lq1, lk1, lq2, lk2, a_norm_w, batch, seq, lambda_init):
    T = batch * seq
    tq = min(TQ_ATTN, seq)
    hps = ATTN_HEADS_PER_STEP
    width = hps * HEAD_DIM
    cst = lambda b, g: (0, 0)
    return pl.pallas_call(
        functools.partial(_attn_kernel, seq=seq, tq=tq, heads=hps, lambda_init=lambda_init),
        grid=(batch, N_HEADS // hps),
        in_specs=[
            pl.BlockSpec((width, seq), lambda b, g: (g, b)),
            pl.BlockSpec((seq, width), lambda b, g: (b, g)),
            pl.BlockSpec((hps * (HEAD_DIM + SUM_ROWS), seq), lambda b, g: (g, b)),
            pl.BlockSpec((1, QK_DIM), cst),
            pl.BlockSpec((1, QK_DIM), cst),
            pl.BlockSpec((1, QK_DIM), cst),
            pl.BlockSpec((1, QK_DIM), cst),
            pl.BlockSpec((1, HEAD_DIM), cst),
        ],
        out_specs=pl.BlockSpec((seq, width), lambda b, g: (b, g)),
        out_shape=jax.ShapeDtypeStruct((T, MIX_HALF), BF16),
        compiler_params=_cparams(("arbitrary", "arbitrary")),
        name="diffattn",
    )(qt, kn, vt, lq1, lk1, lq2, lk2, a_norm_w)


def _outproj_router_kernel(hm_ref, ha_ref, x_ref, wo_ref, nw_ref, wr_hi_ref, wr_lo_ref, br_ref,
                           x1_ref, h2_ref, route_ref, route_t_ref, counts_ref, carry):
    i = pl.program_id(0)
    tm = x_ref.shape[0]
    tp = tm // OUT_PARTS
    parts = range(OUT_PARTS)
    rows = [slice(p * tp, (p + 1) * tp) for p in parts]

    @pl.when(i == 0)
    def _():
        carry[...] = jnp.zeros_like(carry)

    mix = [jnp.dot(hm_ref[rows[p], :], wo_ref[0:MIX_HALF, :], preferred_element_type=F32)
           + jnp.dot(ha_ref[rows[p], :], wo_ref[MIX_HALF:, :], preferred_element_type=F32) for p in parts]
    x1 = [x_ref[rows[p], :] + mix[p] for p in parts]
    h2 = [x1[p] * lax.rsqrt(jnp.mean(x1[p] * x1[p], axis=-1, keepdims=True) + EPS) * nw_ref[...]
          for p in parts]
    for p in parts:
        x1_ref[rows[p], :] = x1[p]
        h2_ref[rows[p], :] = h2[p]

    h_hi = [h2[p].astype(BF16) for p in parts]
    h_lo = [(h2[p] - h_hi[p].astype(F32)).astype(BF16) for p in parts]
    logits = [(jnp.dot(h_hi[p], wr_hi_ref[...], preferred_element_type=F32)
               + jnp.dot(h_lo[p], wr_hi_ref[...], preferred_element_type=F32)
               + jnp.dot(h_hi[p], wr_lo_ref[...], preferred_element_type=F32)) + br_ref[...] for p in parts]

    lane_i = lax.broadcasted_iota(jnp.int32, (tp, LANES), 1)
    lane = lane_i.astype(F32)
    big = float(LANES)
    gmask = (lane_i >= N_EXPERTS) & (lane_i < N_EXPERTS + N_GROUPS)
    gl = [jnp.where(gmask, logits[p], -jnp.inf) for p in parts]
    gmax = [jnp.max(gl[p], axis=-1, keepdims=True) for p in parts]
    g_sel = [jnp.min(jnp.where(gl[p] == gmax[p], lane, big), axis=-1, keepdims=True) - float(N_EXPERTS)
             for p in parts]
    g_w = [1.0 / jnp.sum(jnp.where(gmask, jnp.exp(logits[p] - gmax[p]), 0.0), axis=-1, keepdims=True)
           for p in parts]
    e_lo = [g_sel[p] * float(EXPERTS_PER_GROUP) for p in parts]
    el = [jnp.where((lane >= e_lo[p]) & (lane < e_lo[p] + float(EXPERTS_PER_GROUP)), logits[p], -jnp.inf)
          for p in parts]
    v1 = [jnp.max(el[p], axis=-1, keepdims=True) for p in parts]
    e1 = [jnp.min(jnp.where(el[p] == v1[p], lane, big), axis=-1, keepdims=True) for p in parts]
    el2 = [jnp.where(lane == e1[p], -jnp.inf, el[p]) for p in parts]
    v2 = [jnp.max(el2[p], axis=-1, keepdims=True) for p in parts]
    e2 = [jnp.min(jnp.where(el2[p] == v2[p], lane, big), axis=-1, keepdims=True) for p in parts]
    t2 = [jnp.exp(v2[p] - v1[p]) for p in parts]
    w1 = [g_w[p] * (1.0 / (1.0 + t2[p])) for p in parts]
    w2 = [g_w[p] * (t2[p] / (1.0 + t2[p])) for p in parts]

    oh1 = [(lane == e1[p]).astype(F32) for p in parts]
    oh2 = [(lane == e2[p]).astype(F32) for p in parts]
    both = [oh1[p] + oh2[p] for p in parts]
    strict = (lax.broadcasted_iota(jnp.int32, (tp, tp), 0)
              > lax.broadcasted_iota(jnp.int32, (tp, tp), 1)).astype(BF16)
    within = [jnp.dot(strict, both[p].astype(BF16), preferred_element_type=F32) for p in parts]
    before = carry[...]
    for p in parts:
        prefix = within[p] + before
        r1 = jnp.sum(oh1[p] * prefix, axis=-1, keepdims=True)
        r2 = jnp.sum(oh2[p] * prefix, axis=-1, keepdims=True)
        before = before + jnp.sum(both[p], axis=0, keepdims=True)
        route = jnp.where(lane_i == 0, e1[p], 0.0)
        route = jnp.where(lane_i == 1, e2[p], route)
        route = jnp.where(lane_i == 2, r1, route)
        route = jnp.where(lane_i == 3, r2, route)
        route = jnp.where(lane_i == 4, w1[p], route)
        route = jnp.where(lane_i == 5, w2[p], route)
        route_ref[rows[p], :] = route
        route_t_ref[:, rows[p]] = route.T[0:ROUTE_ROWS, :]
    carry[...] = before
    counts_ref[...] = before


def _outproj_router(hm, ha, x2d, wo, nw, wr_hi, wr_lo, br):
    T = x2d.shape[0]
    tm = min(TM_OUT, T)
    row = lambda i: (i, 0)
    cst = lambda i: (0, 0)
    return pl.pallas_call(
        _outproj_router_kernel,
        grid=(T // tm,),
        in_specs=[
            pl.BlockSpec((tm, MIX_HALF), row),
            pl.BlockSpec((tm, MIX_HALF), row),
            pl.BlockSpec((tm, D_MODEL), row),
            pl.BlockSpec((D_MODEL, D_MODEL), cst),
            pl.BlockSpec((1, D_MODEL), cst),
            pl.BlockSpec((D_MODEL, LANES), cst),
            pl.BlockSpec((D_MODEL, LANES), cst),
            pl.BlockSpec((1, LANES), cst),
        ],
        out_specs=[
            pl.BlockSpec((tm, D_MODEL), row),
            pl.BlockSpec((tm, D_MODEL), row),
            pl.BlockSpec((tm, LANES), row),
            pl.BlockSpec((ROUTE_ROWS, tm), lambda i: (0, i)),
            pl.BlockSpec((1, LANES), cst),
        ],
        out_shape=[
            jax.ShapeDtypeStruct((T, D_MODEL), F32),
            jax.ShapeDtypeStruct((T, D_MODEL), F32),
            jax.ShapeDtypeStruct((T, LANES), F32),
            jax.ShapeDtypeStruct((ROUTE_ROWS, T), F32),
            jax.ShapeDtypeStruct((1, LANES), F32),
        ],
        scratch_shapes=[pltpu.VMEM((1, LANES), F32)],
        compiler_params=_cparams(("arbitrary",)),
        name="outproj_router",
    )(hm, ha, x2d, wo, nw, wr_hi, wr_lo, br)


ROW_TILE = D_MODEL // LANES


def _to_tiles(dst_ref, x, n):
    for s in range(ROW_TILE):
        dst_ref[pl.ds(s, n, stride=ROW_TILE), :] = x[:, s * LANES:(s + 1) * LANES]


def _tile_piece(src_ref, s, n, base=0):
    return src_ref[pl.ds(base * ROW_TILE + s, n, stride=ROW_TILE), :]


def _row_dma(src, src_row, dst, dst_row, sem):
    def tile(row):
        start = row * ROW_TILE
        return pl.ds(start if isinstance(row, int) else pl.multiple_of(start, ROW_TILE), ROW_TILE)
    return pltpu.make_async_copy(src.at[tile(src_row)], dst.at[tile(dst_row)], sem)


def _dest_copy(dest_hbm, dsm, dsem, step):
    slot = lax.rem(step, 2)
    return pltpu.make_async_copy(dest_hbm.at[step], dsm.at[slot], dsem.at[slot])


def _scatter_kernel(pend_ref, padded_ref, dest_hbm, h2_ref, xbuf, dsm, dsem, stage, ssem, zbuf, zsem):
    i = pl.program_id(0)
    n = pl.num_programs(0)
    tm = h2_ref.shape[0] // 2
    blk = MOE_BLOCK

    @pl.when(i == 0)
    def _():
        zbuf[...] = jnp.zeros_like(zbuf)

        def zero_copy(e):
            row0 = pl.multiple_of((pend_ref[e] - blk) * ROW_TILE, blk * ROW_TILE)
            return pltpu.make_async_copy(zbuf, xbuf.at[pl.ds(row0, blk * ROW_TILE)], zsem)

        for e in range(N_EXPERTS):
            pl.when(padded_ref[e] > 0)(lambda e=e: zero_copy(e).start())
        for e in range(N_EXPERTS):
            pl.when(padded_ref[e] > 0)(lambda e=e: zero_copy(e).wait())

        def zero_tail(bi, _):
            row0 = pl.multiple_of(bi * (blk * ROW_TILE), blk * ROW_TILE)
            cp = pltpu.make_async_copy(zbuf, xbuf.at[pl.ds(row0, blk * ROW_TILE)], zsem)
            cp.start()
            cp.wait()
            return 0

        lax.fori_loop(pend_ref[N_EXPERTS - 1] // blk, xbuf.shape[0] // (blk * ROW_TILE), zero_tail, 0)
        _dest_copy(dest_hbm, dsm, dsem, i).start()

    _dest_copy(dest_hbm, dsm, dsem, i).wait()

    @pl.when(i + 1 < n)
    def _():
        _dest_copy(dest_hbm, dsm, dsem, i + 1).start()

    slot = lax.rem(i, 2)

    def drain(half):
        for _ in range(TOP_K):
            pltpu.make_async_copy(stage.at[half], xbuf.at[pl.ds(0, tm * ROW_TILE)], ssem.at[half]).wait()

    for half in range(2):
        pl.when(i > 0)(functools.partial(drain, half))
        _to_tiles(stage.at[half], h2_ref[half * tm:(half + 1) * tm, :], tm)
        for r in range(tm):
            for kk in range(TOP_K):
                d = dsm[slot, 0, kk * 2 * tm + half * tm + r]
                _row_dma(stage.at[half], r, xbuf, d, ssem.at[half]).start(priority=kk)

    @pl.when(i == n - 1)
    def _():
        for half in range(2):
            drain(half)


def _scatter_rows(pad_end, padded, dest3, h2, n_pos):
    T = h2.shape[0]
    n_steps = dest3.shape[0]
    tm2 = T // n_steps
    return pl.pallas_call(
        _scatter_kernel,
        grid_spec=pltpu.PrefetchScalarGridSpec(
            num_scalar_prefetch=2,
            grid=(n_steps,),
            in_specs=[
                pl.BlockSpec(memory_space=pl.ANY),
                pl.BlockSpec((tm2, D_MODEL), lambda i, pe, pd: (i, 0)),
            ],
            out_specs=pl.BlockSpec(memory_space=pl.ANY),
            scratch_shapes=[
                pltpu.SMEM((2, 1, dest3.shape[2]), jnp.int32),
                pltpu.SemaphoreType.DMA((2,)),
                pltpu.VMEM((2, (tm2 // 2) * ROW_TILE, LANES), F32),
                pltpu.SemaphoreType.DMA((2,)),
                pltpu.VMEM((MOE_BLOCK * ROW_TILE, LANES), F32),
                pltpu.SemaphoreType.DMA,
            ],
        ),
        out_shape=jax.ShapeDtypeStruct((n_pos * ROW_TILE, LANES), F32),
        compiler_params=_cparams(("arbitrary",)),
        name="scatter_rows",
    )(pad_end, padded, dest3, h2)


def _expert_kernel(be_ref, nv_ref, x_ref, wg_ref, wu_ref, wd_ref, y_ref, wg_s, wu_s, wd_s):
    b = pl.program_id(0)
    blk = MOE_BLOCK

    @pl.when((b == 0) | (be_ref[b] != be_ref[jnp.maximum(b - 1, 0)]))
    def _():
        wg_s[...] = wg_ref[0].astype(BF16)
        wu_s[...] = wu_ref[0].astype(BF16)
        wd_s[...] = wd_ref[0].astype(BF16)

    @pl.when(b < nv_ref[0])
    def _():
        parts = range(EXPERT_PARTS)
        tp = blk // EXPERT_PARTS
        x = [jnp.concatenate([_tile_piece(x_ref, s, tp, base=p * tp) for s in range(ROW_TILE)],
                             axis=1).astype(BF16) for p in parts]
        g = [jnp.dot(x[p], wg_s[...], preferred_element_type=F32) for p in parts]
        u = [jnp.dot(x[p], wu_s[...], preferred_element_type=F32) for p in parts]
        hid = [(g[p] * jax.nn.sigmoid(g[p]) * u[p]).astype(BF16) for p in parts]
        y = [jnp.dot(hid[p], wd_s[...], preferred_element_type=F32) for p in parts]
        for p in parts:
            _to_tiles(y_ref.at[pl.ds(p * tp * ROW_TILE, tp * ROW_TILE)], y[p], tp)

    @pl.when(b >= nv_ref[0])
    def _():
        y_ref[...] = jnp.zeros_like(y_ref)


def _experts(block_expert, n_valid, xbuf, wg, wu, wd):
    blk = MOE_BLOCK
    nb = xbuf.shape[0] // (blk * ROW_TILE)
    xmap = lambda b, be, nv: (jnp.minimum(b, nv[0] - 1), 0)
    wmap = lambda b, be, nv: (be[b], 0, 0)
    return pl.pallas_call(
        _expert_kernel,
        grid_spec=pltpu.PrefetchScalarGridSpec(
            num_scalar_prefetch=2,
            grid=(nb,),
            in_specs=[
                pl.BlockSpec((blk * ROW_TILE, LANES), xmap),
                pl.BlockSpec((1, D_MODEL, D_EXPERT), wmap),
                pl.BlockSpec((1, D_MODEL, D_EXPERT), wmap),
                pl.BlockSpec((1, D_EXPERT, D_MODEL), wmap),
            ],
            out_specs=pl.BlockSpec((blk * ROW_TILE, LANES), lambda b, be, nv: (b, 0)),
            scratch_shapes=[
                pltpu.VMEM((D_MODEL, D_EXPERT), BF16),
                pltpu.VMEM((D_MODEL, D_EXPERT), BF16),
                pltpu.VMEM((D_EXPERT, D_MODEL), BF16),
            ],
        ),
        out_shape=jax.ShapeDtypeStruct(xbuf.shape, F32),
        compiler_params=_cparams(("arbitrary",)),
        name="experts",
    )(block_expert, n_valid, xbuf, wg, wu, wd)


def _combine_kernel(dest_hbm, x1_ref, route_ref, ybuf, out_ref, dsm, dsem, buf, gsem):
    i = pl.program_id(0)
    n = pl.num_programs(0)
    tm = x1_ref.shape[0] // 2

    def issue_gather(step, half):
        slot = lax.rem(step, 2)
        for r in range(tm):
            for kk in range(TOP_K):
                d = dsm[slot, 0, kk * 2 * tm + half * tm + r]
                _row_dma(ybuf, d, buf.at[half], kk * tm + r, gsem.at[half]).start(priority=kk)

    def combine_tile(half):
        for _ in range(TOP_K):
            pltpu.make_async_copy(ybuf.at[pl.ds(0, tm * ROW_TILE)],
                                  buf.at[half, pl.ds(0, tm * ROW_TILE)], gsem.at[half]).wait()
        rows = slice(half * tm, (half + 1) * tm)
        w1 = route_ref[rows, 4:5]
        w2 = route_ref[rows, 5:6]
        for s in range(ROW_TILE):
            cols = slice(s * LANES, (s + 1) * LANES)
            out_ref[rows, cols] = x1_ref[rows, cols] + (_tile_piece(buf.at[half], s, tm) * w1
                                                        + _tile_piece(buf.at[half], s, tm, base=tm) * w2)

    @pl.when(i == 0)
    def _():
        _dest_copy(dest_hbm, dsm, dsem, i).start()
        _dest_copy(dest_hbm, dsm, dsem, i).wait()
        issue_gather(i, 0)

    @pl.when(i + 1 < n)
    def _():
        _dest_copy(dest_hbm, dsm, dsem, i + 1).start()

    issue_gather(i, 1)
    combine_tile(0)

    @pl.when(i + 1 < n)
    def _():
        _dest_copy(dest_hbm, dsm, dsem, i + 1).wait()
        issue_gather(i + 1, 0)

    combine_tile(1)


def _combine(dest3, x1, route, ybuf):
    T = x1.shape[0]
    n_steps = dest3.shape[0]
    tm2 = T // n_steps
    return pl.pallas_call(
        _combine_kernel,
        grid=(n_steps,),
        in_specs=[
            pl.BlockSpec(memory_space=pl.ANY),
            pl.BlockSpec((tm2, D_MODEL), lambda i: (i, 0)),
            pl.BlockSpec((tm2, LANES), lambda i: (i, 0)),
            pl.BlockSpec(memory_space=pl.ANY),
        ],
        out_specs=pl.BlockSpec((tm2, D_MODEL), lambda i: (i, 0)),
        out_shape=jax.ShapeDtypeStruct((T, D_MODEL), F32),
        scratch_shapes=[
            pltpu.SMEM((2, 1, dest3.shape[2]), jnp.int32),
            pltpu.SemaphoreType.DMA((2,)),
            pltpu.VMEM((2, TOP_K * (tm2 // 2) * ROW_TILE, LANES), F32),
            pltpu.SemaphoreType.DMA((2,)),
        ],
        compiler_params=_cparams(("arbitrary",)),
        name="combine",
    )(dest3, x1, route, ybuf)


def _layer(x2d, batch, seq, lambda_init, norm1_w, w_in, b_igate, b_fgate, conv_w, conv_b, m_norm_w,
           q_norm_w, k_norm_w, lq1, lk1, lq2, lk2, a_norm_w, w_out, norm2_w,
           w_group, b_group, w_expert, b_expert, w_gate, w_up, w_down):
    T = x2d.shape[0]
    row = lambda v: v.reshape(1, -1).astype(F32)
    g0 = 2 * MIX_HALF + 2 * MIX_HALF
    g1 = g0 + 2 * N_HEADS
    w_proj = jnp.concatenate(
        [w_in[:, :g0], w_in[:, g1:], w_in[:, g0:g1], jnp.zeros((D_MODEL, LANES - 2 * N_HEADS), F32)],
        axis=1).astype(BF16)
    gate_b = jnp.concatenate([b_igate, b_fgate, jnp.zeros((LANES - 2 * N_HEADS,), F32)]).reshape(1, LANES)

    qw = row(jnp.concatenate([q_norm_w, q_norm_w])) * (QK_DIM ** -0.5 * math.log2(math.e))
    kw = row(jnp.concatenate([k_norm_w, k_norm_w]))
    proj, gates, mvt, kn, qt, vt = _inproj(x2d, row(norm1_w), w_proj, conv_w, row(conv_b), qw, kw, seq)
    hm = _mlstm(proj, mvt, gates, gate_b, row(m_norm_w), batch, seq)
    ha = _diff_attn(qt, kn, vt, row(lq1), row(lk1), row(lq2), row(lk2), row(a_norm_w),
                    batch, seq, lambda_init)

    w_router = jnp.concatenate(
        [w_expert, w_group, jnp.zeros((D_MODEL, LANES - N_EXPERTS - N_GROUPS), F32)], axis=1)
    wr_hi = w_router.astype(BF16)
    wr_lo = (w_router - wr_hi.astype(F32)).astype(BF16)
    b_router = jnp.concatenate(
        [b_expert, b_group, jnp.zeros((LANES - N_EXPERTS - N_GROUPS,), F32)]).reshape(1, LANES)
    x1, h2, route, route_t, counts = _outproj_router(hm, ha, x2d, w_out.astype(BF16), row(norm2_w),
                                                     wr_hi, wr_lo, b_router)

    blk = MOE_BLOCK
    counts = counts[0, :N_EXPERTS].astype(jnp.int32)
    padded = ((counts + blk - 1) // blk) * blk
    pad_end = jnp.cumsum(padded)
    pad_start = pad_end - padded
    nb = (T * TOP_K) // blk + N_EXPERTS
    n_valid = (pad_end[-1] // blk).astype(jnp.int32)
    bstart = jnp.minimum(jnp.arange(nb, dtype=jnp.int32), n_valid - 1) * blk
    block_expert = jnp.minimum(
        jnp.sum(pad_end[None, :] <= bstart[:, None], axis=1), N_EXPERTS - 1).astype(jnp.int32)
    eid = route_t[0:TOP_K].astype(jnp.int32)
    rank = route_t[TOP_K:2 * TOP_K].astype(jnp.int32)
    experts = jnp.arange(N_EXPERTS, dtype=jnp.int32)[:, None, None]
    dest = rank + jnp.sum(jnp.where(eid[None] == experts, pad_start[:, None, None], 0), axis=0)
    tm2 = min(2 * TM_ROWS, T)
    dest3 = dest.reshape(TOP_K, T // tm2, tm2).transpose(1, 0, 2).reshape(T // tm2, 1, TOP_K * tm2)

    xbuf = _scatter_rows(pad_end.astype(jnp.int32), padded - counts, dest3, h2, nb * blk)
    ybuf = _experts(block_expert, n_valid.reshape(1), xbuf, w_gate, w_up, w_down)
    return _combine(dest3, x1, route, ybuf)


def kernel(x, norm1_w, w_in, b_igate, b_fgate, conv_w, conv_b, m_norm_w, q_norm_w, k_norm_w,
           lambda_q1, lambda_k1, lambda_q2, lambda_k2, a_norm_w, w_out, norm2_w,
           w_group, b_group, w_expert, b_expert, w_gate, w_up, w_down):
    batch, seq, d = x.shape
    x2d = x.reshape(batch * seq, d)
    for l in range(norm1_w.shape[0]):
        lambda_init = 0.8 - 0.6 * math.exp(-0.3 * l)
        x2d = _layer(x2d, batch, seq, lambda_init, norm1_w[l], w_in[l], b_igate[l], b_fgate[l],
                     conv_w[l], conv_b[l], m_norm_w[l], q_norm_w[l], k_norm_w[l],
                     lambda_q1[l], lambda_k1[l], lambda_q2[l], lambda_k2[l], a_norm_w[l],
                     w_out[l], norm2_w[l], w_group[l], b_group[l], w_expert[l], b_expert[l],
                     w_gate[l], w_up[l], w_down[l])
    return x2d.reshape(batch, seq, d)
```
